```python
import math
import jax, jax.numpy as jnp
from jax import lax
import numpy as np

D_MODEL = 1024
BATCH = 2
SEQ = 8192
DEPTH = 2

PLE_DIM = 256
N_HEADS_DIFF = 4
DIFF_HEAD_DIM = 128
DIFF_HALF = DIFF_HEAD_DIM // 2
N_HEADS_FOX = 4
FOX_HEAD_DIM = 128
DIFF_W = N_HEADS_DIFF * DIFF_HEAD_DIM
FOX_W = N_HEADS_FOX * FOX_HEAD_DIM
MIX_W_EVEN = DIFF_W + FOX_W
EVEN_IN = 3 * DIFF_W + 3 * FOX_W + N_HEADS_FOX
Q_BLOCK = 128
GLA_HEADS = 4
GLA_DK = (D_MODEL // 2) // GLA_HEADS
GLA_DV = D_MODEL // GLA_HEADS
GLA_LOWRANK = 16
GLA_GATE_NORM = 16.0
GLA_CHUNK = 64
MIX_W_ODD = GLA_HEADS * GLA_DV
ODD_IN = 2 * GLA_HEADS * GLA_DK + 2 * GLA_HEADS * GLA_DV + GLA_LOWRANK
D_FF = 4 * D_MODEL
EPS = 1e-6
N_EVEN = (DEPTH + 1) // 2
N_ODD = DEPTH // 2

kernel_name = "hybrid_diff_fox_gla_trunk"


def rmsnorm(x, g):
    xf = x.astype(jnp.float32)
    y = xf * lax.rsqrt(jnp.mean(xf * xf, axis=-1, keepdims=True) + EPS)
    return (y * g.astype(jnp.float32)).astype(x.dtype)


def split_heads(t, n_heads):
    b, s, _ = t.shape
    return t.reshape(b, s, n_heads, -1).transpose(0, 2, 1, 3)


def merge_heads(t):
    b, h, s, d = t.shape
    return t.transpose(0, 2, 1, 3).reshape(b, s, h * d)


def split_cols(t, sizes):
    idx = [int(v) for v in np.cumsum(sizes)[:-1]]
    return jnp.split(t, idx, axis=-1)


def diff_attention(q, k, v, lam):
    b, h, s, _ = q.shape
    scale = DIFF_HALF ** -0.5
    q1, q2 = q[..., :DIFF_HALF] * scale, q[..., DIFF_HALF:] * scale
    k1, k2 = k[..., :DIFF_HALF], k[..., DIFF_HALF:]
    kpos = jnp.arange(s)

    def block(start):
        qb1 = lax.dynamic_slice_in_dim(q1, start, Q_BLOCK, axis=2)
        qb2 = lax.dynamic_slice_in_dim(q2, start, Q_BLOCK, axis=2)
        qpos = start + jnp.arange(Q_BLOCK)
        mask = kpos[None, :] <= qpos[:, None]
        s1 = jnp.einsum('bhqd,bhkd->bhqk', qb1, k1).astype(jnp.float32)
        s2 = jnp.einsum('bhqd,bhkd->bhqk', qb2, k2).astype(jnp.float32)
        a1 = jax.nn.softmax(jnp.where(mask, s1, -jnp.inf), axis=-1)
        a2 = jax.nn.softmax(jnp.where(mask, s2, -jnp.inf), axis=-1)
        w = (a1 - lam * a2).astype(v.dtype)
        return jnp.einsum('bhqk,bhkd->bhqd', w, v)

    out = lax.map(block, jnp.arange(s // Q_BLOCK) * Q_BLOCK)
    return out.transpose(1, 2, 0, 3, 4).reshape(b, h, s, -1)


def forgetting_attention(q, k, v, log_f):
    b, h, s, d = q.shape
    scale = d ** -0.5
    c = jnp.cumsum(log_f, axis=-1)
    kpos = jnp.arange(s)

    def block(start):
        qb = lax.dynamic_slice_in_dim(q, start, Q_BLOCK, axis=2)
        cb = lax.dynamic_slice_in_dim(c, start, Q_BLOCK, axis=2)
        qpos = start + jnp.arange(Q_BLOCK)
        mask = kpos[None, :] <= qpos[:, None]
        sc = jnp.einsum('bhqd,bhkd->bhqk', qb, k).astype(jnp.float32) * scale
        sc = sc + cb[..., :, None] - c[..., None, :]
        a = jax.nn.softmax(jnp.where(mask, sc, -jnp.inf), axis=-1)
        return jnp.einsum('bhqk,bhkd->bhqd', a.astype(v.dtype), v)

    out = lax.map(block, jnp.arange(s // Q_BLOCK) * Q_BLOCK)
    return out.transpose(1, 2, 0, 3, 4).reshape(b, h, s, d)


def gla_chunked(q, k, v, gk):
    b, h, s, dk = q.shape
    dv = v.shape[-1]
    L = GLA_CHUNK
    n = s // L

    def chunks(t):
        return t.reshape(b, h, n, L, t.shape[-1]).transpose(2, 0, 1, 3, 4)

    causal = jnp.arange(L)[:, None] >= jnp.arange(L)[None, :]

    def step(state, inp):
        qc, kc, vc, gc = inp
        bcum = jnp.cumsum(gc, axis=-2)
        o_inter = jnp.einsum('bhld,bhde->bhle', qc * jnp.exp(bcum), state)
        rel = bcum[:, :, :, None, :] - bcum[:, :, None, :, :]
        decay = jnp.exp(jnp.where(causal[:, :, None], rel, -jnp.inf))
        scores = jnp.einsum('bhid,bhjd,bhijd->bhij', qc, kc, decay)
        o_intra = jnp.einsum('bhij,bhjd->bhid', scores, vc)
        b_last = bcum[:, :, -1:, :]
        state = (jnp.exp(b_last[:, :, 0, :])[..., None] * state
                 + jnp.einsum('bhld,bhle->bhde', kc * jnp.exp(b_last - bcum), vc))
        return state, o_inter + o_intra

    s0 = jnp.zeros((b, h, dk, dv), jnp.float32)
    _, o = lax.scan(step, s0, (chunks(q), chunks(k), chunks(v), chunks(gk)))
    return o.transpose(1, 2, 0, 3, 4).reshape(b, h, s, dv)


def even_mixer(hn, w_in, b_f, lam_q1, lam_k1, lam_q2, lam_k2, g_sub, w_o, layer_idx):
    proj = hn @ w_in
    qa, ka, va, qb, kb, vb, fl = split_cols(
        proj, [DIFF_W, DIFF_W, DIFF_W, FOX_W, FOX_W, FOX_W, N_HEADS_FOX])
    lam_init = 0.8 - 0.6 * math.exp(-0.3 * layer_idx)
    f32 = jnp.float32
    lam = (jnp.exp(jnp.sum(lam_q1.astype(f32) * lam_k1.astype(f32)))
           - jnp.exp(jnp.sum(lam_q2.astype(f32) * lam_k2.astype(f32))) + lam_init)
    oa = diff_attention(split_heads(qa, N_HEADS_DIFF), split_heads(ka, N_HEADS_DIFF),
                        split_heads(va, N_HEADS_DIFF), lam)
    oa = rmsnorm(oa, g_sub) * (1.0 - lam_init)
    log_f = jax.nn.log_sigmoid((fl + b_f).astype(f32)).transpose(0, 2, 1)
    ob = forgetting_attention(split_heads(qb, N_HEADS_FOX), split_heads(kb, N_HEADS_FOX),
                              split_heads(vb, N_HEADS_FOX), log_f)
    o = jnp.concatenate([merge_heads(oa), merge_heads(ob).astype(oa.dtype)], axis=-1)
    return o @ w_o


def odd_mixer(hn, w_in, w_gk, b_gk, g_onorm, w_o):
    proj = hn @ w_in
    q, k, v, g, gl = split_cols(
        proj, [GLA_HEADS * GLA_DK, GLA_HEADS * GLA_DK, GLA_HEADS * GLA_DV,
               GLA_HEADS * GLA_DV, GLA_LOWRANK])
    f32 = jnp.float32
    gk = jax.nn.log_sigmoid((gl @ w_gk + b_gk).astype(f32)) / GLA_GATE_NORM
    o = gla_chunked(split_heads(q.astype(f32), GLA_HEADS) * (GLA_DK ** -0.5),
                    split_heads(k.astype(f32), GLA_HEADS),
                    split_heads(v.astype(f32), GLA_HEADS),
                    split_heads(gk, GLA_HEADS))
    o = rmsnorm(o, g_onorm).astype(hn.dtype)
    o = merge_heads(o) * jax.nn.silu(g)
    return o @ w_o


def mlp_sq_relu(hn, w_up, w_down):
    return jnp.square(jax.nn.relu(hn @ w_up)) @ w_down


def setup_inputs(seed: int = 0) -> dict:
    key = jax.random.key(seed)
    ks = jax.random.split(key, 32)

    def nrm(k, shape, scale):
        return jax.random.normal(k, shape, jnp.float32) * scale

    def gain(k, shape):
        return 1.0 + 0.02 * jax.random.normal(k, shape, jnp.float32)

    return {
        "x": nrm(ks[0], (BATCH, SEQ, D_MODEL), 1.0),
        "p": nrm(ks[1], (DEPTH, BATCH, SEQ, PLE_DIM), 1.0),
        "norm_mix": gain(ks[2], (DEPTH, D_MODEL)),
        "norm_mlp": gain(ks[3], (DEPTH, D_MODEL)),
        "norm_ple": gain(ks[4], (DEPTH, D_MODEL)),
        "norm_final": gain(ks[5], (D_MODEL,)),
        "even_w_in": nrm(ks[6], (N_EVEN, D_MODEL, EVEN_IN), D_MODEL ** -0.5),
        "even_b_f": jax.random.uniform(ks[7], (N_EVEN, N_HEADS_FOX), jnp.float32, 1.0, 4.0),
        "lam_q1": nrm(ks[8], (N_EVEN, DIFF_HALF), 0.1),
        "lam_k1": nrm(ks[9], (N_EVEN, DIFF_HALF), 0.1),
        "lam_q2": nrm(ks[10], (N_EVEN, DIFF_HALF), 0.1),
        "lam_k2": nrm(ks[11], (N_EVEN, DIFF_HALF), 0.1),
        "diff_subln": gain(ks[12], (N_EVEN, DIFF_HEAD_DIM)),
        "even_w_o": nrm(ks[13], (N_EVEN, MIX_W_EVEN, D_MODEL), MIX_W_EVEN ** -0.5),
        "odd_w_in": nrm(ks[14], (N_ODD, D_MODEL, ODD_IN), D_MODEL ** -0.5),
        "odd_w_gk": nrm(ks[15], (N_ODD, GLA_LOWRANK, GLA_HEADS * GLA_DK), GLA_LOWRANK ** -0.5),
        "odd_b_gk": nrm(ks[16], (N_ODD, GLA_HEADS * GLA_DK), 0.1),
        "gla_onorm": gain(ks[17], (N_ODD, GLA_DV)),
        "odd_w_o": nrm(ks[18], (N_ODD, MIX_W_ODD, D_MODEL), MIX_W_ODD ** -0.5),
        "w_up": nrm(ks[19], (DEPTH, D_MODEL, D_FF), D_MODEL ** -0.5),
        "w_down": nrm(ks[20], (DEPTH, D_FF, D_MODEL), D_FF ** -0.5),
        "w_ple_gate": nrm(ks[21], (DEPTH, D_MODEL, D_MODEL), D_MODEL ** -0.5),
        "w_ple_proj": nrm(ks[22], (DEPTH, PLE_DIM, D_MODEL), PLE_DIM ** -0.5),
    }


def reference(x, p, norm_mix, norm_mlp, norm_ple, norm_final,
              even_w_in, even_b_f, lam_q1, lam_k1, lam_q2, lam_k2, diff_subln, even_w_o,
              odd_w_in, odd_w_gk, odd_b_gk, gla_onorm, odd_w_o,
              w_up, w_down, w_ple_gate, w_ple_proj):
    h = x
    for i in range(DEPTH):
        j = i // 2
        hn = rmsnorm(h, norm_mix[i])
        if i % 2 == 0:
            h = h + even_mixer(hn, even_w_in[j], even_b_f[j], lam_q1[j], lam_k1[j],
                               lam_q2[j], lam_k2[j], diff_subln[j], even_w_o[j], i)
        else:
            h = h + odd_mixer(hn, odd_w_in[j], odd_w_gk[j], odd_b_gk[j],
                              gla_onorm[j], odd_w_o[j])
        h = h + mlp_sq_relu(rmsnorm(h, norm_mlp[i]), w_up[i], w_down[i])
        gate = jax.nn.sigmoid(rmsnorm(h, norm_ple[i]) @ w_ple_gate[i])
        h = h + gate * (p[i] @ w_ple_proj[i])
    return rmsnorm(h, norm_final)
```

```python
import functools
import math

import jax
import jax.numpy as jnp
from jax import lax
from jax.experimental import pallas as pl
from jax.experimental.pallas import tpu as pltpu

F32 = jnp.float32
BF16 = jnp.bfloat16

D_MODEL = 1024
PLE_DIM = 256
N_HEADS = 4
HEAD_DIM = 128
DIFF_HALF = 64
GLA_DV = 256
GLA_LOWRANK = 16
GLA_GATE_NORM = 16.0
GLA_CHUNK = 64
GLA_SUB = 16
D_FF = 4 * D_MODEL
FF_CHUNK = 1024
EPS = 1e-6
LOG2E = 1.4426950408889634
NEG_BIG = -1e30
LANE = 128
MAIN_W = 3072

VMEM_LIMIT = 56 * 1024 * 1024


def _cparams(sem):
    return pltpu.CompilerParams(dimension_semantics=sem, vmem_limit_bytes=VMEM_LIMIT)


def _const_spec(shape):
    nd = len(shape)
    return pl.BlockSpec(shape, lambda *_: (0,) * nd, pipeline_mode=pl.Buffered(1))


def _rms(x, g):
    return x * lax.rsqrt(jnp.mean(x * x, axis=-1, keepdims=True) + EPS) * g


def _log_sigmoid(x):
    return jnp.minimum(x, 0.0) - jnp.log1p(jnp.exp(-jnp.abs(x)))


def _split3(x):
    hi = x.astype(BF16)
    r1 = x - hi.astype(F32)
    mid = r1.astype(BF16)
    lo = (r1 - mid.astype(F32)).astype(BF16)
    return hi, mid, lo


def _dot(a, b):
    return jnp.dot(a, b, preferred_element_type=F32)


def _dot_nt(a, b):
    return lax.dot_general(a, b, (((1,), (1,)), ((), ())), preferred_element_type=F32)


def _inproj_kernel(x_ref, g_ref, w_ref, wg_ref, cs_ref, o_ref, og_ref, *, n_chunk):
    hn = _rms(x_ref[...], g_ref[...]).astype(BF16)
    for c0 in range(0, MAIN_W, n_chunk):
        y = _dot(hn, w_ref[:, c0:c0 + n_chunk]) * cs_ref[:, c0:c0 + n_chunk]
        o_ref[:, c0:c0 + n_chunk] = y.astype(o_ref.dtype)
    og_ref[...] = _dot(hn, wg_ref[...])


def _inproj(x, g, w_main, w_gate, colscale, tm=512):
    t = x.shape[0]
    return pl.pallas_call(
        functools.partial(_inproj_kernel, n_chunk=512),
        grid=(t // tm,),
        in_specs=[
            pl.BlockSpec((tm, D_MODEL), lambda i: (i, 0)),
            _const_spec((1, D_MODEL)),
            _const_spec((D_MODEL, MAIN_W)),
            _const_spec((D_MODEL, LANE)),
            _const_spec((1, MAIN_W)),
        ],
        out_specs=[
            pl.BlockSpec((tm, MAIN_W), lambda i: (i, 0)),
            pl.BlockSpec((tm, LANE), lambda i: (i, 0)),
        ],
        out_shape=[
            jax.ShapeDtypeStruct((t, MAIN_W), BF16),
            jax.ShapeDtypeStruct((t, LANE), F32),
        ],
        compiler_params=_cparams(("arbitrary",)),
        name="inproj",
    )(x, g, w_main, w_gate, colscale)


def _fox_prep_kernel(gl_ref, bf_ref, tri_ref, eq_ref, ek_ref, oq_ref, ok_ref,
                     qx_ref, kx_ref, carry_ref):
    i = pl.program_id(1)

    @pl.when(i == 0)
    def _():
        carry_ref[...] = jnp.zeros_like(carry_ref)

    logf = _log_sigmoid(gl_ref[0] + bf_ref[...]) * LOG2E
    hi, mid, lo = _split3(logf)
    tri = tri_ref[...]
    c = carry_ref[...] + _dot(tri, hi) + _dot(tri, mid) + _dot(tri, lo)
    tp = c.shape[0]
    carry_ref[...] = c[tp - 1:tp, :]
    terms = jnp.concatenate(_split3(c), axis=1)
    qx = _dot(terms, eq_ref[...]) + oq_ref[...]
    kx = ok_ref[...] - _dot(terms, ek_ref[...])
    for h in range(N_HEADS):
        qx_ref[0, h] = qx[:, h * LANE:(h + 1) * LANE].astype(BF16)
        kx_ref[0, h] = kx[:, h * LANE:(h + 1) * LANE].astype(BF16)


def _fox_prep(gate, b_f, batch, seq, tp=512):
    bf = jnp.zeros((1, LANE), F32).at[0, :N_HEADS].set(b_f.astype(F32))
    r = jnp.arange(tp)
    tri = (r[:, None] >= r[None, :]).astype(BF16)
    eq = jnp.zeros((3 * LANE, N_HEADS * LANE), F32)
    ek = jnp.zeros((3 * LANE, N_HEADS * LANE), F32)
    oq = jnp.zeros((1, N_HEADS * LANE), F32)
    ok = jnp.zeros((1, N_HEADS * LANE), F32)
    for h in range(N_HEADS):
        for j in range(3):
            eq = eq.at[j * LANE + h, h * LANE + j].set(1.0)
            ek = ek.at[j * LANE + h, h * LANE + 3 + j].set(1.0)
            oq = oq.at[0, h * LANE + 3 + j].set(1.0)
            ok = ok.at[0, h * LANE + j].set(1.0)
    out_sds = jax.ShapeDtypeStruct((batch, N_HEADS, seq, LANE), BF16)
    return pl.pallas_call(
        _fox_prep_kernel,
        grid=(batch, seq // tp),
        in_specs=[
            pl.BlockSpec((1, tp, LANE), lambda b, i: (b, i, 0)),
            _const_spec((1, LANE)),
            _const_spec((tp, tp)),
            _const_spec((3 * LANE, N_HEADS * LANE)),
            _const_spec((3 * LANE, N_HEADS * LANE)),
            _const_spec((1, N_HEADS * LANE)),
            _const_spec((1, N_HEADS * LANE)),
        ],
        out_specs=[
            pl.BlockSpec((1, N_HEADS, tp, LANE), lambda b, i: (b, 0, i, 0)),
            pl.BlockSpec((1, N_HEADS, tp, LANE), lambda b, i: (b, 0, i, 0)),
        ],
        out_shape=[out_sds, out_sds],
        scratch_shapes=[pltpu.VMEM((1, LANE), F32)],
        compiler_params=_cparams(("arbitrary", "arbitrary")),
        name="fox_prep",
    )(gate, bf, tri, eq.astype(BF16), ek.astype(BF16), oq, ok)


def _softmax_step(s, m, l, acc, v):
    m_new = jnp.maximum(m, jnp.max(s, axis=-1, keepdims=True))
    alpha = jnp.exp2(m - m_new)
    p = jnp.exp2(s - m_new)
    l = alpha * l + jnp.sum(p, axis=-1, keepdims=True)
    acc = alpha * acc + _dot(p.astype(BF16), v)
    return m_new, l, acc


def _causal_mask(t):
    r = lax.broadcasted_iota(jnp.int32, (t, t), 0)
    c = lax.broadcasted_iota(jnp.int32, (t, t), 1)
    return c <= r


def _fox_attn_kernel(q_ref, qx_ref, k_ref, kx_ref, v_ref, o_ref, *, t):
    i = pl.program_id(2)
    q = jnp.concatenate([q_ref[0], qx_ref[0, 0]], axis=1)

    def logits(j):
        r = pl.multiple_of(j * t, t)
        k = jnp.concatenate([k_ref[0, pl.ds(r, t), :], kx_ref[0, 0, pl.ds(r, t), :]], axis=1)
        return _dot_nt(q, k), v_ref[0, pl.ds(r, t), :]

    def body(j, carry):
        s, v = logits(j)
        return _softmax_step(s, *carry, v)

    init = (jnp.full((t, 1), NEG_BIG, F32), jnp.zeros((t, 1), F32), jnp.zeros((t, HEAD_DIM), F32))
    carry = lax.fori_loop(0, i, body, init)
    s, v = logits(i)
    s = jnp.where(_causal_mask(t), s, NEG_BIG)
    _, l, acc = _softmax_step(s, *carry, v)
    o_ref[0] = (acc / l).astype(o_ref.dtype)


def _fox_attention(p3, qx, kx, t=512):
    batch, seq, _ = p3.shape
    return pl.pallas_call(
        functools.partial(_fox_attn_kernel, t=t),
        grid=(batch, N_HEADS, seq // t),
        in_specs=[
            pl.BlockSpec((1, t, LANE), lambda b, h, i: (b, i, 12 + h)),
            pl.BlockSpec((1, 1, t, LANE), lambda b, h, i: (b, h, i, 0)),
            pl.BlockSpec((1, seq, LANE), lambda b, h, i: (b, 0, 16 + h)),
            pl.BlockSpec((1, 1, seq, LANE), lambda b, h, i: (b, h, 0, 0)),
            pl.BlockSpec((1, seq, LANE), lambda b, h, i: (b, 0, 20 + h)),
        ],
        out_specs=pl.BlockSpec((1, t, LANE), lambda b, h, i: (b, i, h)),
        out_shape=jax.ShapeDtypeStruct((batch, seq, N_HEADS * HEAD_DIM), BF16),
        compiler_params=_cparams(("arbitrary", "arbitrary", "arbitrary")),
        name="fox_attn",
    )(p3, qx, p3, kx, p3)


def _diff_attn_kernel(q_ref, k_ref, v_ref, lq1_ref, lk1_ref, lq2_ref, lk2_ref, gs_ref, o_ref,
                      *, t, lam_init):
    i = pl.program_id(2)
    q = q_ref[0]
    lane = lax.broadcasted_iota(jnp.int32, q.shape, 1)
    zero = jnp.zeros_like(q)
    q1 = jnp.where(lane < DIFF_HALF, q, zero)
    q2 = jnp.where(lane >= DIFF_HALF, q, zero)

    def block(j):
        r = pl.multiple_of(j * t, t)
        return k_ref[0, pl.ds(r, t), :], v_ref[0, pl.ds(r, t), :]

    def body(j, carry):
        k, v = block(j)
        c1 = _softmax_step(_dot_nt(q1, k), *carry[:3], v)
        c2 = _softmax_step(_dot_nt(q2, k), *carry[3:], v)
        return c1 + c2

    one = (jnp.full((t, 1), NEG_BIG, F32), jnp.zeros((t, 1), F32), jnp.zeros((t, HEAD_DIM), F32))
    carry = lax.fori_loop(0, i, body, one + one)
    k, v = block(i)
    mask = _causal_mask(t)
    _, l1, a1 = _softmax_step(jnp.where(mask, _dot_nt(q1, k), NEG_BIG), *carry[:3], v)
    _, l2, a2 = _softmax_step(jnp.where(mask, _dot_nt(q2, k), NEG_BIG), *carry[3:], v)
    lam = (jnp.exp(jnp.sum(lq1_ref[...] * lk1_ref[...], axis=-1, keepdims=True))
           - jnp.exp(jnp.sum(lq2_ref[...] * lk2_ref[...], axis=-1, keepdims=True)) + lam_init)
    o = a1 / l1 - lam * (a2 / l2)
    o_ref[0] = (_rms(o, gs_ref[...]) * (1.0 - lam_init)).astype(o_ref.dtype)


def _diff_attention(p3, lq1, lk1, lq2, lk2, g_sub, lam_init, t=512):
    batch, seq, _ = p3.shape
    vec = lambda a: a.reshape(1, -1).astype(F32)
    return pl.pallas_call(
        functools.partial(_diff_attn_kernel, t=t, lam_init=lam_init),
        grid=(batch, N_HEADS, seq // t),
        in_specs=[
            pl.BlockSpec((1, t, LANE), lambda b, h, i: (b, i, h)),
            pl.BlockSpec((1, seq, LANE), lambda b, h, i: (b, 0, 4 + h)),
            pl.BlockSpec((1, seq, LANE), lambda b, h, i: (b, 0, 8 + h)),
            _const_spec((1, DIFF_HALF)), _const_spec((1, DIFF_HALF)),
            _const_spec((1, DIFF_HALF)), _const_spec((1, DIFF_HALF)),
            _const_spec((1, HEAD_DIM)),
        ],
        out_specs=pl.BlockSpec((1, t, LANE), lambda b, h, i: (b, i, h)),
        out_shape=jax.ShapeDtypeStruct((batch, seq, N_HEADS * HEAD_DIM), BF16),
        compiler_params=_cparams(("arbitrary", "arbitrary", "arbitrary")),
        name="diff_attn",
    )(p3, p3, p3, vec(lq1), vec(lk1), vec(lq2), vec(lk2), vec(g_sub))


def _gla_kernel(q_ref, k_ref, v_ref, g_ref, gl_ref, wgk_ref, bgk_ref, on_ref, tri_ref,
                o_ref, st_ref, bc_ref, *, tc):
    i = pl.program_id(2)

    @pl.when(i == 0)
    def _():
        st_ref[...] = jnp.zeros_like(st_ref)

    x = _dot(gl_ref[0].astype(BF16), wgk_ref[...]) + bgk_ref[...]
    gk = _log_sigmoid(x) * (1.0 / GLA_GATE_NORM)
    hi, mid, lo = _split3(gk)
    tri = tri_ref[...]
    bc_ref[...] = _dot(tri, hi) + _dot(tri, mid) + _dot(tri, lo)

    L, SB = GLA_CHUNK, GLA_SUB
    row = lax.broadcasted_iota(jnp.int32, (L, HEAD_DIM), 0)
    sub_row = lax.broadcasted_iota(jnp.int32, (SB, 1), 0)

    def chunk(c, _):
        r = pl.multiple_of(c * L, L)
        q = q_ref[0, pl.ds(r, L), :].astype(F32)
        k = k_ref[0, pl.ds(r, L), :].astype(F32)
        v = v_ref[0, pl.ds(r, L), :]
        vf = v.astype(F32)
        b = bc_ref[pl.ds(r, L), :]
        st = st_ref[...]

        o = _dot_nt((q * jnp.exp(b)).astype(BF16), st.astype(BF16))

        qs, ks = [], []
        for s in range(1, L // SB):
            ref = b[s * SB - 1:s * SB, :]
            in_sub = (row >= s * SB) & (row < (s + 1) * SB)
            qs.append(jnp.where(in_sub, q * jnp.exp(jnp.minimum(b - ref, 0.0)), 0.0))
            ks.append(jnp.where(row < s * SB, k * jnp.exp(jnp.minimum(ref - b, 0.0)), 0.0))
        a = _dot_nt(jnp.concatenate(qs, axis=1).astype(BF16),
                    jnp.concatenate(ks, axis=1).astype(BF16))
        o = o + _dot(a.astype(BF16), v)

        diag = []
        for s in range(L // SB):
            sl = slice(s * SB, (s + 1) * SB)
            qd, kd, bd, vd = q[sl], k[sl], b[sl], vf[sl]
            acc = jnp.zeros((SB, GLA_DV), F32)
            for j in range(SB):
                w = qd * kd[j:j + 1] * jnp.exp(jnp.minimum(bd - bd[j:j + 1], 0.0))
                d = jnp.sum(w, axis=-1, keepdims=True)
                acc = acc + jnp.where(sub_row >= j, d, 0.0) * vd[j:j + 1]
            diag.append(acc)
        o = o + jnp.concatenate(diag, axis=0)

        b_last = b[L - 1:L, :]
        k_dec = (k * jnp.exp(b_last - b)).astype(BF16)
        st_ref[...] = st * jnp.exp(b_last) + _dot(vf.T.astype(BF16), k_dec)

        y = _rms(o, on_ref[...])
        g = g_ref[0, pl.ds(r, L), :].astype(F32)
        o_ref[0, pl.ds(r, L), :] = (y * (g * jax.nn.sigmoid(g))).astype(o_ref.dtype)
        return 0

    lax.fori_loop(0, tc // L, chunk, 0)


def _gla(p3, gl3, w_gk, b_gk, g_onorm, tc=512):
    batch, seq, _ = p3.shape
    wgk = jnp.zeros((LANE, N_HEADS * HEAD_DIM), F32).at[:GLA_LOWRANK].set(w_gk).astype(BF16)
    r = jnp.arange(tc)
    tri = ((r[:, None] >= r[None, :]) & (r[:, None] // GLA_CHUNK == r[None, :] // GLA_CHUNK)).astype(BF16)
    return pl.pallas_call(
        functools.partial(_gla_kernel, tc=tc),
        grid=(batch, N_HEADS, seq // tc),
        in_specs=[
            pl.BlockSpec((1, tc, HEAD_DIM), lambda b, h, i: (b, i, h)),
            pl.BlockSpec((1, tc, HEAD_DIM), lambda b, h, i: (b, i, 4 + h)),
            pl.BlockSpec((1, tc, GLA_DV), lambda b, h, i: (b, i, 4 + h)),
            pl.BlockSpec((1, tc, GLA_DV), lambda b, h, i: (b, i, 8 + h)),
            pl.BlockSpec((1, tc, LANE), lambda b, h, i: (b, i, 0)),
            pl.BlockSpec((LANE, HEAD_DIM), lambda b, h, i: (0, h)),
            pl.BlockSpec((1, HEAD_DIM), lambda b, h, i: (0, h)),
            _const_spec((1, GLA_DV)),
            _const_spec((tc, tc)),
        ],
        out_specs=pl.BlockSpec((1, tc, GLA_DV), lambda b, h, i: (b, i, h)),
        out_shape=jax.ShapeDtypeStruct((batch, seq, N_HEADS * GLA_DV), BF16),
        scratch_shapes=[pltpu.VMEM((GLA_DV, HEAD_DIM), F32), pltpu.VMEM((tc, HEAD_DIM), F32)],
        compiler_params=_cparams(("arbitrary", "arbitrary", "arbitrary")),
        name="gla",
    )(p3, p3, p3, p3, gl3, wgk, b_gk.reshape(1, -1).astype(F32),
      g_onorm.reshape(1, -1).astype(F32), tri)


def _tail_kernel(*refs, n_mix, final_norm):
    h_ref = refs[0]
    o_refs = refs[1:1 + n_mix]
    wo_refs = refs[1 + n_mix:1 + 2 * n_mix]
    rest = refs[1 + 2 * n_mix:]
    gm_ref, wu_ref, wd_ref, gp_ref, wg_ref, p_ref, wp_ref = rest[:7]
    gf_ref = rest[7] if final_norm else None
    out_ref = rest[-1]

    mixed = _dot(o_refs[0][...], wo_refs[0][...])
    for o_ref, wo_ref in zip(o_refs[1:], wo_refs[1:]):
        mixed = mixed + _dot(o_ref[...], wo_ref[...])
    h = h_ref[...] + mixed
    hn =_rms(h, gm_ref[...]).astype(BF16)
    acc = h
    for c0 in range(0, D_FF, FF_CHUNK):
        u = jnp.square(jnp.maximum(_dot(hn, wu_ref[:, c0:c0 + FF_CHUNK]), 0.0))
        acc = acc + _dot(u.astype(BF16), wd_ref[c0:c0 + FF_CHUNK, :])
    h = acc
    gate = jax.nn.sigmoid(_dot(_rms(h, gp_ref[...]).astype(BF16), wg_ref[...]))
    h = h + gate * _dot(p_ref[...].astype(BF16), wp_ref[...])
    if final_norm:
        h = _rms(h, gf_ref[...])
    out_ref[...] = h


def _tail(h, mix, w_o_parts, g_mlp, w_up, w_down, g_ple, w_gate, p, w_proj, g_final, tm=512):
    t = h.shape[0]
    n_mix = len(mix)
    row = lambda a: a.reshape(1, -1).astype(F32)
    args = [h, *mix, *w_o_parts, row(g_mlp), w_up, w_down, row(g_ple), w_gate, p, w_proj]
    in_specs = [pl.BlockSpec((tm, D_MODEL), lambda i: (i, 0))]
    in_specs += [pl.BlockSpec((tm, m.shape[1]), lambda i: (i, 0)) for m in mix]
    in_specs += [_const_spec(w.shape) for w in w_o_parts]
    in_specs += [_const_spec((1, D_MODEL)), _const_spec(w_up.shape), _const_spec(w_down.shape),
                 _const_spec((1, D_MODEL)), _const_spec(w_gate.shape),
                 pl.BlockSpec((tm, PLE_DIM), lambda i: (i, 0)), _const_spec(w_proj.shape)]
    if g_final is not None:
        args.append(row(g_final))
        in_specs.append(_const_spec((1, D_MODEL)))
    return pl.pallas_call(
        functools.partial(_tail_kernel, n_mix=n_mix, final_norm=g_final is not None),
        grid=(t // tm,),
        in_specs=in_specs,
        out_specs=pl.BlockSpec((tm, D_MODEL), lambda i: (i, 0)),
        out_shape=jax.ShapeDtypeStruct((t, D_MODEL), F32),
        compiler_params=_cparams(("arbitrary",)),
        name="tail",
    )(*args)


def _split_in_weight(w_in, n_gate):
    w_gate = jnp.zeros((D_MODEL, LANE), F32).at[:, :n_gate].set(w_in[:, MAIN_W:])
    return w_in[:, :MAIN_W].astype(BF16), w_gate.astype(BF16)


def kernel(x, p, norm_mix, norm_mlp, norm_ple, norm_final, even_w_in, even_b_f, lam_q1, lam_k1, lam_q2, lam_k2, diff_subln, even_w_o, odd_w_in, odd_w_gk, odd_b_gk, gla_onorm, odd_w_o, w_up, w_down, w_ple_gate, w_ple_proj):
    batch, seq, _ = x.shape
    depth = p.shape[0]
    t = batch * seq
    h = x.reshape(t, D_MODEL)
    for i in range(depth):
        j = i // 2
        g_mix = norm_mix[i].reshape(1, -1)
        if i % 2 == 0:
            w_main, w_gate = _split_in_weight(even_w_in[j], N_HEADS)
            cs = jnp.ones((MAIN_W,), F32)
            cs = cs.at[0:512].set(DIFF_HALF ** -0.5 * LOG2E)
            cs = cs.at[1536:2048].set(HEAD_DIM ** -0.5 * LOG2E)
            proj, gate = _inproj(h, g_mix, w_main, w_gate, cs.reshape(1, -1))
            p3 = proj.reshape(batch, seq, MAIN_W)
            lam_init = 0.8 - 0.6 * math.exp(-0.3 * i)
            oa = _diff_attention(p3, lam_q1[j], lam_k1[j], lam_q2[j], lam_k2[j],
                                 diff_subln[j], lam_init)
            qx, kx = _fox_prep(gate.reshape(batch, seq, LANE), even_b_f[j], batch, seq)
            ob = _fox_attention(p3, qx, kx)
            mix = [oa.reshape(t, -1), ob.reshape(t, -1)]
            w_o = even_w_o[j].astype(BF16)
            w_o_parts = [w_o[:512], w_o[512:]]
        else:
            w_main, w_gate = _split_in_weight(odd_w_in[j], GLA_LOWRANK)
            cs = jnp.ones((MAIN_W,), F32).at[0:512].set(HEAD_DIM ** -0.5)
            proj, gate = _inproj(h, g_mix, w_main, w_gate, cs.reshape(1, -1))
            o = _gla(proj.reshape(batch, seq, MAIN_W), gate.reshape(batch, seq, LANE),
                     odd_w_gk[j], odd_b_gk[j], gla_onorm[j])
            mix = [o.reshape(t, -1)]
            w_o_parts = [odd_w_o[j].astype(BF16)]
        h = _tail(h, mix, w_o_parts, norm_mlp[i], w_up[i].astype(BF16), w_down[i].astype(BF16),
                  norm_ple[i], w_ple_gate[i].astype(BF16), p[i].reshape(t, PLE_DIM),
                  w_ple_proj[i].astype(BF16), norm_final if i == depth - 1 else None)
    return h.reshape(batch, seq, D_MODEL)
```

```python
import functools
import math

import jax
import jax.numpy as jnp
import numpy as np
from jax import lax
from jax.experimental import pallas as pl
from jax.experimental.pallas import tpu as pltpu

F32 = jnp.float32
BF16 = jnp.bfloat16

D_MODEL = 1024
PLE_DIM = 256
N_HEADS = 4
HEAD_DIM = 128
DIFF_HALF = 64
GLA_DV = 256
GLA_LOWRANK = 16
GLA_GATE_NORM = 16.0
GLA_CHUNK = 64
GLA_SUB = 16
D_FF = 4 * D_MODEL
FF_CHUNK = 1024
EPS = 1e-6
LOG2E = 1.4426950408889634
NEG_BIG = -1e30
LANE = 128
MAIN_W = 3072

VMEM_LIMIT = 56 * 1024 * 1024


def _cparams(sem):
    return pltpu.CompilerParams(dimension_semantics=sem, vmem_limit_bytes=VMEM_LIMIT)


def _const_spec(shape):
    nd = len(shape)
    return pl.BlockSpec(shape, lambda *_: (0,) * nd, pipeline_mode=pl.Buffered(1))


def _rms(x, g):
    return x * lax.rsqrt(jnp.mean(x * x, axis=-1, keepdims=True) + EPS) * g


def _log_sigmoid(x):
    return jnp.minimum(x, 0.0) - jnp.log1p(jnp.exp(-jnp.abs(x)))


def _split3(x):
    hi = x.astype(BF16)
    r1 = x - hi.astype(F32)
    mid = r1.astype(BF16)
    lo = (r1 - mid.astype(F32)).astype(BF16)
    return hi, mid, lo


def _dot(a, b):
    return jnp.dot(a, b, preferred_element_type=F32)


def _dot_nt(a, b):
    return lax.dot_general(a, b, (((1,), (1,)), ((), ())), preferred_element_type=F32)


def _inproj_kernel(x_ref, g_ref, w_ref, wg_ref, cs_ref, o_ref, og_ref, *, n_chunk):
    hn = _rms(x_ref[...], g_ref[...]).astype(BF16)
    for c0 in range(0, MAIN_W, n_chunk):
        y = _dot(hn, w_ref[:, c0:c0 + n_chunk]) * cs_ref[:, c0:c0 + n_chunk]
        o_ref[:, c0:c0 + n_chunk] = y.astype(o_ref.dtype)
    og_ref[...] = _dot(hn, wg_ref[...])


def _inproj(x, g, w_main, w_gate, colscale, tm=512):
    t = x.shape[0]
    return pl.pallas_call(
        functools.partial(_inproj_kernel, n_chunk=512),
        grid=(t // tm,),
        in_specs=[
            pl.BlockSpec((tm, D_MODEL), lambda i: (i, 0)),
            _const_spec((1, D_MODEL)),
            _const_spec((D_MODEL, MAIN_W)),
            _const_spec((D_MODEL, LANE)),
            _const_spec((1, MAIN_W)),
        ],
        out_specs=[
            pl.BlockSpec((tm, MAIN_W), lambda i: (i, 0)),
            pl.BlockSpec((tm, LANE), lambda i: (i, 0)),
        ],
        out_shape=[
            jax.ShapeDtypeStruct((t, MAIN_W), BF16),
            jax.ShapeDtypeStruct((t, LANE), F32),
        ],
        compiler_params=_cparams(("arbitrary",)),
        name="inproj",
    )(x, g, w_main, w_gate, colscale)


def _fox_prep_kernel(gl_ref, bf_ref, tri_ref, eq_ref, ek_ref, oq_ref, ok_ref,
                     qx_ref, kx_ref, carry_ref):
    i = pl.program_id(1)

    @pl.when(i == 0)
    def _():
        carry_ref[...] = jnp.zeros_like(carry_ref)

    logf = _log_sigmoid(gl_ref[0] + bf_ref[...]) * LOG2E
    hi, mid, lo = _split3(logf)
    tri = tri_ref[...]
    c = carry_ref[...] + _dot(tri, hi) + _dot(tri, mid) + _dot(tri, lo)
    tp = c.shape[0]
    carry_ref[...] = c[tp - 1:tp, :]
    terms = jnp.concatenate(_split3(c), axis=1)
    qx = _dot(terms, eq_ref[...]) + oq_ref[...]
    kx = ok_ref[...] - _dot(terms, ek_ref[...])
    for h in range(N_HEADS):
        qx_ref[0, h] = qx[:, h * LANE:(h + 1) * LANE].astype(BF16)
        kx_ref[0, h] = kx[:, h * LANE:(h + 1) * LANE].astype(BF16)


def _fox_prep(gate, b_f, batch, seq, tp=512):
    bf = jnp.pad(b_f.astype(F32).reshape(1, -1), ((0, 0), (0, LANE - N_HEADS)))
    r = np.arange(tp)
    tri = jnp.asarray(r[:, None] >= r[None, :], BF16)
    eq = np.zeros((3 * LANE, N_HEADS * LANE), np.float32)
    ek = np.zeros((3 * LANE, N_HEADS * LANE), np.float32)
    oq = np.zeros((1, N_HEADS * LANE), np.float32)
    ok = np.zeros((1, N_HEADS * LANE), np.float32)
    for h in range(N_HEADS):
        for j in range(3):
            eq[j * LANE + h, h * LANE + j] = 1.0
            ek[j * LANE + h, h * LANE + 3 + j] = 1.0
            oq[0, h * LANE + 3 + j] = 1.0
            ok[0, h * LANE + j] = 1.0
    out_sds = jax.ShapeDtypeStruct((batch, N_HEADS, seq, LANE), BF16)
    return pl.pallas_call(
        _fox_prep_kernel,
        grid=(batch, seq // tp),
        in_specs=[
            pl.BlockSpec((1, tp, LANE), lambda b, i: (b, i, 0)),
            _const_spec((1, LANE)),
            _const_spec((tp, tp)),
            _const_spec((3 * LANE, N_HEADS * LANE)),
            _const_spec((3 * LANE, N_HEADS * LANE)),
            _const_spec((1, N_HEADS * LANE)),
            _const_spec((1, N_HEADS * LANE)),
        ],
        out_specs=[
            pl.BlockSpec((1, N_HEADS, tp, LANE), lambda b, i: (b, 0, i, 0)),
            pl.BlockSpec((1, N_HEADS, tp, LANE), lambda b, i: (b, 0, i, 0)),
        ],
        out_shape=[out_sds, out_sds],
        scratch_shapes=[pltpu.VMEM((1, LANE), F32)],
        compiler_params=_cparams(("arbitrary", "arbitrary")),
        name="fox_prep",
    )(gate, bf, tri, jnp.asarray(eq, BF16), jnp.asarray(ek, BF16), jnp.asarray(oq), jnp.asarray(ok))


def _flash_causal(i, t, u, qs, k_block, v_block, scr):
    n = len(qs)
    g = t // u
    per = g * n
    assert per % 2 == 0
    s_slot, p_slot, (acc, m_ref, alpha_ref) = scr[0:2], scr[2:4], scr[4:]

    def lanes(x, width):
        return jnp.concatenate([x] * (width // LANE), axis=1)

    def logits(a, j, slot, r0=0):
        s_slot[slot][r0:, :] = _dot_nt(qs[a][r0:], k_block(j))

    def pv(a, j, slot, r0=0):
        acc[a, r0:, :] = (lanes(alpha_ref[slot, r0:, :], 2 * HEAD_DIM) * acc[a, r0:, :]
                          + _dot(p_slot[slot][r0:, :], v_block(jnp.maximum(j, 0))))

    def softmax(a, slot, mask, r0=0):
        s_ref = s_slot[slot]
        if mask is None:
            load = lambda: s_ref[r0:, :]
        else:
            load = lambda: jnp.where(mask, s_ref[r0:, :], NEG_BIG)
        m_old = m_ref[a, r0:, :]
        m_new = jnp.maximum(m_old, jnp.max(load(), axis=-1, keepdims=True))
        m_ref[a, r0:, :] = m_new
        alpha_ref[slot, r0:, :] = jnp.exp2(m_old - m_new)
        p_slot[slot][r0:, :] = jnp.exp2(load() - lanes(m_new, u)).astype(BF16)

    def group(j0, tri, last):
        row0 = lambda e: 0 if tri is None or e < 0 else (e // n) * u
        for e in range(per):
            slot = e & 1
            if e + 1 < per:
                logits((e + 1) % n, j0 + (e + 1) // n, 1 - slot, row0(e + 1))
            elif not last:
                logits(0, j0 + g, 1 - slot)
            if e > 0:
                pv((e - 1) % n, j0 + (e - 1) // n, 1 - slot, row0(e - 1))
            else:
                pv(n - 1, j0 - 1, 1 - slot)
            softmax(e % n, slot, None if tri is None else tri[:t - row0(e)], row0(e))

    m_ref[...] = jnp.full(m_ref.shape, NEG_BIG, F32)
    alpha_ref[...] = jnp.ones(alpha_ref.shape, F32)
    acc[...] = jnp.zeros(acc.shape, F32)
    p_slot[1][...] = jnp.zeros(p_slot[1].shape, BF16)
    logits(0, 0, 0)

    def body(k, _):
        group(k * g, None, False)
        return 0

    lax.fori_loop(0, i, body, 0)
    r = lax.broadcasted_iota(jnp.int32, (t, u), 0)
    c = lax.broadcasted_iota(jnp.int32, (t, u), 1)
    group(i * g, c <= r, True)
    pv(n - 1, i * g + g - 1, (per - 1) & 1, (g - 1) * u)
    return [acc[a] for a in range(n)]


def _flash_scratch(n, t, u):
    return [pltpu.VMEM((t, u), F32), pltpu.VMEM((t, u), F32),
            pltpu.VMEM((t, u), BF16), pltpu.VMEM((t, u), BF16),
            pltpu.VMEM((n, t, 2 * HEAD_DIM), F32),
            pltpu.VMEM((n, t, LANE), F32), pltpu.VMEM((2, t, LANE), F32)]


def _ones_column(u):
    lane = lax.broadcasted_iota(jnp.int32, (u, LANE), 1)
    return jnp.where(lane == 0, 1.0, 0.0).astype(BF16)


def _fox_attn_kernel(q_ref, qx_ref, k_ref, kx_ref, v_ref, o_ref, *scr, t, u):
    i = pl.program_id(2)
    q = jnp.concatenate([q_ref[0], qx_ref[0, 0]], axis=1)
    ones = _ones_column(u)

    def k_block(j):
        r = pl.multiple_of(j * u, u)
        return jnp.concatenate([k_ref[0, pl.ds(r, u), :], kx_ref[0, 0, pl.ds(r, u), :]], axis=1)

    def v_block(j):
        return jnp.concatenate([v_ref[0, pl.ds(pl.multiple_of(j * u, u), u), :], ones], axis=1)

    acc, = _flash_causal(i, t, u, [q], k_block, v_block, scr)
    o_ref[0] = (acc[:, :HEAD_DIM] / acc[:, HEAD_DIM:HEAD_DIM + 1]).astype(o_ref.dtype)


def _fox_attention(p3, qx, kx, t=1024, u=256):
    batch, seq, _ = p3.shape
    return pl.pallas_call(
        functools.partial(_fox_attn_kernel, t=t, u=u),
        grid=(batch, N_HEADS, seq // t),
        in_specs=[
            pl.BlockSpec((1, t, LANE), lambda b, h, i: (b, i, 12 + h)),
            pl.BlockSpec((1, 1, t, LANE), lambda b, h, i: (b, h, i, 0)),
            pl.BlockSpec((1, seq, LANE), lambda b, h, i: (b, 0, 16 + h)),
            pl.BlockSpec((1, 1, seq, LANE), lambda b, h, i: (b, h, 0, 0)),
            pl.BlockSpec((1, seq, LANE), lambda b, h, i: (b, 0, 20 + h)),
        ],
        out_specs=pl.BlockSpec((1, t, LANE), lambda b, h, i: (b, i, h)),
        out_shape=jax.ShapeDtypeStruct((batch, seq, N_HEADS * HEAD_DIM), BF16),
        scratch_shapes=_flash_scratch(1, t, u),
        compiler_params=_cparams(("arbitrary", "arbitrary", "arbitrary")),
        name="fox_attn",
    )(p3, qx, p3, kx, p3)


def _diff_attn_kernel(q_ref, k_ref, v_ref, lq1_ref, lk1_ref, lq2_ref, lk2_ref, gs_ref, o_ref,
                      *scr, t, u, lam_init):
    i = pl.program_id(2)
    q = q_ref[0]
    lane = lax.broadcasted_iota(jnp.int32, q.shape, 1)
    zero = jnp.zeros_like(q)
    q1 = jnp.where(lane < DIFF_HALF, q, zero)
    q2 = jnp.where(lane >= DIFF_HALF, q, zero)
    ones = _ones_column(u)
    k_block = lambda j: k_ref[0, pl.ds(pl.multiple_of(j * u, u), u), :]
    v_block = lambda j: jnp.concatenate(
        [v_ref[0, pl.ds(pl.multiple_of(j * u, u), u), :], ones], axis=1)
    a1, a2 = _flash_causal(i, t, u, [q1, q2], k_block, v_block, scr)
    lam = (jnp.exp(jnp.sum(lq1_ref[...] * lk1_ref[...], axis=-1, keepdims=True))
           - jnp.exp(jnp.sum(lq2_ref[...] * lk2_ref[...], axis=-1, keepdims=True)) + lam_init)
    o = (a1[:, :HEAD_DIM] / a1[:, HEAD_DIM:HEAD_DIM + 1]
         - lam * (a2[:, :HEAD_DIM] / a2[:, HEAD_DIM:HEAD_DIM + 1]))
    o_ref[0] = (_rms(o, gs_ref[...]) * (1.0 - lam_init)).astype(o_ref.dtype)


def _diff_attention(p3, lq1, lk1, lq2, lk2, g_sub, lam_init, t=1024, u=256):
    batch, seq, _ = p3.shape
    vec = lambda a: a.reshape(1, -1).astype(F32)
    return pl.pallas_call(
        functools.partial(_diff_attn_kernel, t=t, u=u, lam_init=lam_init),
        grid=(batch, N_HEADS, seq // t),
        in_specs=[
            pl.BlockSpec((1, t, LANE), lambda b, h, i: (b, i, h)),
            pl.BlockSpec((1, seq, LANE), lambda b, h, i: (b, 0, 4 + h)),
            pl.BlockSpec((1, seq, LANE), lambda b, h, i: (b, 0, 8 + h)),
            _const_spec((1, DIFF_HALF)), _const_spec((1, DIFF_HALF)),
            _const_spec((1, DIFF_HALF)), _const_spec((1, DIFF_HALF)),
            _const_spec((1, HEAD_DIM)),
        ],
        out_specs=pl.BlockSpec((1, t, LANE), lambda b, h, i: (b, i, h)),
        out_shape=jax.ShapeDtypeStruct((batch, seq, N_HEADS * HEAD_DIM), BF16),
        scratch_shapes=_flash_scratch(2, t, u),
        compiler_params=_cparams(("arbitrary", "arbitrary", "arbitrary")),
        name="diff_attn",
    )(p3, p3, p3, vec(lq1), vec(lk1), vec(lq2), vec(lk2), vec(g_sub))


def _gla_kernel(q_ref, k_ref, v_ref, g_ref, gl_ref, wgk_ref, bgk_ref, on_ref, tri_ref,
                o_ref, st_ref, bc_ref, *, tc):
    i = pl.program_id(2)

    @pl.when(i == 0)
    def _():
        st_ref[...] = jnp.zeros_like(st_ref)

    x = _dot(gl_ref[0].astype(BF16), wgk_ref[...]) + bgk_ref[...]
    gk = _log_sigmoid(x) * (1.0 / GLA_GATE_NORM)
    hi, mid, lo = _split3(gk)
    tri = tri_ref[...]
    bc_ref[...] = _dot(tri, hi) + _dot(tri, mid) + _dot(tri, lo)

    L, SB = GLA_CHUNK, GLA_SUB
    row = lax.broadcasted_iota(jnp.int32, (L, HEAD_DIM), 0)
    sub_row = lax.broadcasted_iota(jnp.int32, (SB, 1), 0)

    def chunk(c, _):
        r = pl.multiple_of(c * L, L)
        q = q_ref[0, pl.ds(r, L), :].astype(F32)
        k = k_ref[0, pl.ds(r, L), :].astype(F32)
        v = v_ref[0, pl.ds(r, L), :]
        vf = v.astype(F32)
        b = bc_ref[pl.ds(r, L), :]
        st = st_ref[...]

        o = _dot_nt((q * jnp.exp(b)).astype(BF16), st.astype(BF16))

        qs, ks = [], []
        for s in range(1, L // SB):
            ref = b[s * SB - 1:s * SB, :]
            in_sub = (row >= s * SB) & (row < (s + 1) * SB)
            qs.append(jnp.where(in_sub, q * jnp.exp(jnp.minimum(b - ref, 0.0)), 0.0))
            ks.append(jnp.where(row < s * SB, k * jnp.exp(jnp.minimum(ref - b, 0.0)), 0.0))
        a = _dot_nt(jnp.concatenate(qs, axis=1).astype(BF16),
                    jnp.concatenate(ks, axis=1).astype(BF16))
        o = o + _dot(a.astype(BF16), v)

        diag = []
        for s in range(L // SB):
            sl = slice(s * SB, (s + 1) * SB)
            qd, kd, bd, vd = q[sl], k[sl], b[sl], vf[sl]
            acc = jnp.zeros((SB, GLA_DV), F32)
            for j in range(SB):
                w = qd * kd[j:j + 1] * jnp.exp(jnp.minimum(bd - bd[j:j + 1], 0.0))
                d = jnp.sum(w, axis=-1, keepdims=True)
                acc = acc + jnp.where(sub_row >= j, d, 0.0) * vd[j:j + 1]
            diag.append(acc)
        o = o + jnp.concatenate(diag, axis=0)

        b_last = b[L - 1:L, :]
        k_dec = (k * jnp.exp(b_last - b)).astype(BF16)
        st_ref[...] = st * jnp.exp(b_last) + _dot(vf.T.astype(BF16), k_dec)

        y = _rms(o, on_ref[...])
        g = g_ref[0, pl.ds(r, L), :].astype(F32)
        o_ref[0, pl.ds(r, L), :] = (y * (g * jax.nn.sigmoid(g))).astype(o_ref.dtype)
        return 0

    lax.fori_loop(0, tc // L, chunk, 0)


def _gla(p3, gl3, w_gk, b_gk, g_onorm, tc=512):
    batch, seq, _ = p3.shape
    wgk = jnp.pad(w_gk, ((0, LANE - GLA_LOWRANK), (0, 0))).astype(BF16)
    r = np.arange(tc)
    tri = jnp.asarray((r[:, None] >= r[None, :])
                      & (r[:, None] // GLA_CHUNK == r[None, :] // GLA_CHUNK), BF16)
    return pl.pallas_call(
        functools.partial(_gla_kernel, tc=tc),
        grid=(batch, N_HEADS, seq // tc),
        in_specs=[
            pl.BlockSpec((1, tc, HEAD_DIM), lambda b, h, i: (b, i, h)),
            pl.BlockSpec((1, tc, HEAD_DIM), lambda b, h, i: (b, i, 4 + h)),
            pl.BlockSpec((1, tc, GLA_DV), lambda b, h, i: (b, i, 4 + h)),
            pl.BlockSpec((1, tc, GLA_DV), lambda b, h, i: (b, i, 8 + h)),
            pl.BlockSpec((1, tc, LANE), lambda b, h, i: (b, i, 0)),
            pl.BlockSpec((LANE, HEAD_DIM), lambda b, h, i: (0, h)),
            pl.BlockSpec((1, HEAD_DIM), lambda b, h, i: (0, h)),
            _const_spec((1, GLA_DV)),
            _const_spec((tc, tc)),
        ],
        out_specs=pl.BlockSpec((1, tc, GLA_DV), lambda b, h, i: (b, i, h)),
        out_shape=jax.ShapeDtypeStruct((batch, seq, N_HEADS * GLA_DV), BF16),
        scratch_shapes=[pltpu.VMEM((GLA_DV, HEAD_DIM), F32), pltpu.VMEM((tc, HEAD_DIM), F32)],
        compiler_params=_cparams(("arbitrary", "arbitrary", "arbitrary")),
        name="gla",
    )(p3, p3, p3, p3, gl3, wgk, b_gk.reshape(1, -1).astype(F32),
      g_onorm.reshape(1, -1).astype(F32), tri)


def _tail_kernel(*refs, n_mix, final_norm):
    h_ref = refs[0]
    o_refs = refs[1:1 + n_mix]
    wo_refs = refs[1 + n_mix:1 + 2 * n_mix]
    rest = refs[1 + 2 * n_mix:]
    gm_ref, wu_ref, wd_ref, gp_ref, wg_ref, p_ref, wp_ref = rest[:7]
    gf_ref = rest[7] if final_norm else None
    out_ref = rest[-1]

    mixed = _dot(o_refs[0][...], wo_refs[0][...])
    for o_ref, wo_ref in zip(o_refs[1:], wo_refs[1:]):
        mixed = mixed + _dot(o_ref[...], wo_ref[...])
    h = h_ref[...] + mixed
    hn =_rms(h, gm_ref[...]).astype(BF16)
    acc = h
    for c0 in range(0, D_FF, FF_CHUNK):
        u = jnp.square(jnp.maximum(_dot(hn, wu_ref[:, c0:c0 + FF_CHUNK]), 0.0))
        acc = acc + _dot(u.astype(BF16), wd_ref[c0:c0 + FF_CHUNK, :])
    h = acc
    gate = jax.nn.sigmoid(_dot(_rms(h, gp_ref[...]).astype(BF16), wg_ref[...]))
    h = h + gate * _dot(p_ref[...].astype(BF16), wp_ref[...])
    if final_norm:
        h = _rms(h, gf_ref[...])
    out_ref[...] = h


def _tail(h, mix, w_o_parts, g_mlp, w_up, w_down, g_ple, w_gate, p, w_proj, g_final, tm=512):
    t = h.shape[0]
    n_mix = len(mix)
    row = lambda a: a.reshape(1, -1).astype(F32)
    args = [h, *mix, *w_o_parts, row(g_mlp), w_up, w_down, row(g_ple), w_gate, p, w_proj]
    in_specs = [pl.BlockSpec((tm, D_MODEL), lambda i: (i, 0))]
    in_specs += [pl.BlockSpec((tm, m.shape[1]), lambda i: (i, 0)) for m in mix]
    in_specs += [_const_spec(w.shape) for w in w_o_parts]
    in_specs += [_const_spec((1, D_MODEL)), _const_spec(w_up.shape), _const_spec(w_down.shape),
                 _const_spec((1, D_MODEL)), _const_spec(w_gate.shape),
                 pl.BlockSpec((tm, PLE_DIM), lambda i: (i, 0)), _const_spec(w_proj.shape)]
    if g_final is not None:
        args.append(row(g_final))
        in_specs.append(_const_spec((1, D_MODEL)))
    return pl.pallas_call(
        functools.partial(_tail_kernel, n_mix=n_mix, final_norm=g_final is not None),
        grid=(t // tm,),
        in_specs=in_specs,
        out_specs=pl.BlockSpec((tm, D_MODEL), lambda i: (i, 0)),
        out_shape=jax.ShapeDtypeStruct((t, D_MODEL), F32),
        compiler_params=_cparams(("arbitrary",)),
        name="tail",
    )(*args)


def _split_in_weight(w_in, n_gate):
    w_gate = jnp.pad(w_in[:, MAIN_W:], ((0, 0), (0, LANE - n_gate)))
    return w_in[:, :MAIN_W].astype(BF16), w_gate.astype(BF16)


def _col_scale(scales):
    cs = np.ones((1, MAIN_W), np.float32)
    for (c0, c1), v in scales.items():
        cs[0, c0:c1] = v
    return jnp.asarray(cs)


def kernel(x, p, norm_mix, norm_mlp, norm_ple, norm_final, even_w_in, even_b_f, lam_q1, lam_k1, lam_q2, lam_k2, diff_subln, even_w_o, odd_w_in, odd_w_gk, odd_b_gk, gla_onorm, odd_w_o, w_up, w_down, w_ple_gate, w_ple_proj):
    batch, seq, _ = x.shape
    depth = p.shape[0]
    t = batch * seq
    h = x.reshape(t, D_MODEL)
    for i in range(depth):
        j = i // 2
        g_mix = norm_mix[i].reshape(1, -1)
        if i % 2 == 0:
            w_main, w_gate = _split_in_weight(even_w_in[j], N_HEADS)
            cs = _col_scale({(0, 512): DIFF_HALF ** -0.5 * LOG2E,
                             (1536, 2048): HEAD_DIM ** -0.5 * LOG2E})
            proj, gate = _inproj(h, g_mix, w_main, w_gate, cs)
            p3 = proj.reshape(batch, seq, MAIN_W)
            lam_init = 0.8 - 0.6 * math.exp(-0.3 * i)
            oa = _diff_attention(p3, lam_q1[j], lam_k1[j], lam_q2[j], lam_k2[j],
                                 diff_subln[j], lam_init)
            qx, kx = _fox_prep(gate.reshape(batch, seq, LANE), even_b_f[j], batch, seq)
            ob = _fox_attention(p3, qx, kx)
            mix = [oa.reshape(t, -1), ob.reshape(t, -1)]
            w_o = even_w_o[j].astype(BF16)
            w_o_parts = [w_o[:512], w_o[512:]]
        else:
            w_main, w_gate = _split_in_weight(odd_w_in[j], GLA_LOWRANK)
            cs = _col_scale({(0, 512): HEAD_DIM ** -0.5})
            proj, gate = _inproj(h, g_mix, w_main, w_gate, cs)
            o = _gla(proj.reshape(batch, seq, MAIN_W), gate.reshape(batch, seq, LANE),
                     odd_w_gk[j], odd_b_gk[j], gla_onorm[j])
            mix = [o.reshape(t, -1)]
            w_o_parts = [odd_w_o[j].astype(BF16)]
        h = _tail(h, mix, w_o_parts, norm_mlp[i], w_up[i].astype(BF16), w_down[i].astype(BF16),
                  norm_ple[i], w_ple_gate[i].astype(BF16), p[i].reshape(t, PLE_DIM),
                  w_ple_proj[i].astype(BF16), norm_final if i == depth - 1 else None)
    return h.reshape(batch, seq, D_MODEL)
```

```python
import functools
import math

import jax
import jax.numpy as jnp
import numpy as np
from jax import lax
from jax.experimental import pallas as pl
from jax.experimental.pallas import tpu as pltpu

F32 = jnp.float32
BF16 = jnp.bfloat16

D_MODEL = 1024
PLE_DIM = 256
N_HEADS = 4
HEAD_DIM = 128
DIFF_HALF = 64
GLA_DV = 256
GLA_LOWRANK = 16
GLA_GATE_NORM = 16.0
GLA_CHUNK = 64
GLA_SUB = 16
GLA_FAST_CHUNK = 128
GLA_SAFE_DECAY = 60.0
D_FF = 4 * D_MODEL
FF_CHUNK = 1024
EPS = 1e-6
LOG2E = 1.4426950408889634
NEG_BIG = -1e30
LANE = 128
MAIN_W = 3072

VMEM_LIMIT = 56 * 1024 * 1024


def _cparams(sem):
    return pltpu.CompilerParams(dimension_semantics=sem, vmem_limit_bytes=VMEM_LIMIT)


def _const_spec(shape):
    nd = len(shape)
    return pl.BlockSpec(shape, lambda *_: (0,) * nd, pipeline_mode=pl.Buffered(1))


def _rms(x, g):
    return x * lax.rsqrt(jnp.mean(x * x, axis=-1, keepdims=True) + EPS) * g


def _log_sigmoid(x):
    return jnp.minimum(x, 0.0) - jnp.log(1.0 + jnp.exp(-jnp.abs(x)))


def _split3(x):
    hi = x.astype(BF16)
    r1 = x - hi.astype(F32)
    mid = r1.astype(BF16)
    lo = (r1 - mid.astype(F32)).astype(BF16)
    return hi, mid, lo


def _dot(a, b):
    return jnp.dot(a, b, preferred_element_type=F32)


def _cumsum_rows(x, tri):
    y = _dot(tri, jnp.concatenate(_split3(x), axis=1))
    return y[:, :LANE] + y[:, LANE:2 * LANE] + y[:, 2 * LANE:]


def _dot_nt(a, b):
    return lax.dot_general(a, b, (((1,), (1,)), ((), ())), preferred_element_type=F32)


def _inproj_kernel(x_ref, g_ref, w_ref, wg_ref, cs_ref, o_ref, og_ref, *, n_chunk):
    hn = _rms(x_ref[...], g_ref[...]).astype(BF16)
    for c0 in range(0, MAIN_W, n_chunk):
        y = _dot(hn, w_ref[:, c0:c0 + n_chunk]) * cs_ref[:, c0:c0 + n_chunk]
        o_ref[:, c0:c0 + n_chunk] = y.astype(o_ref.dtype)
    og_ref[...] = _dot(hn, wg_ref[...])


def _inproj(x, g, w_main, w_gate, colscale, tm=512):
    t = x.shape[0]
    return pl.pallas_call(
        functools.partial(_inproj_kernel, n_chunk=512),
        grid=(t // tm,),
        in_specs=[
            pl.BlockSpec((tm, D_MODEL), lambda i: (i, 0)),
            _const_spec((1, D_MODEL)),
            _const_spec((D_MODEL, MAIN_W)),
            _const_spec((D_MODEL, LANE)),
            _const_spec((1, MAIN_W)),
        ],
        out_specs=[
            pl.BlockSpec((tm, MAIN_W), lambda i: (i, 0)),
            pl.BlockSpec((tm, LANE), lambda i: (i, 0)),
        ],
        out_shape=[
            jax.ShapeDtypeStruct((t, MAIN_W), BF16),
            jax.ShapeDtypeStruct((t, LANE), F32),
        ],
        compiler_params=_cparams(("arbitrary",)),
        name="inproj",
    )(x, g, w_main, w_gate, colscale)


def _fox_prep_kernel(gl_ref, bf_ref, tri_ref, eq_ref, ek_ref, oq_ref, ok_ref,
                     qx_ref, kx_ref, carry_ref):
    i = pl.program_id(1)

    @pl.when(i == 0)
    def _():
        carry_ref[...] = jnp.zeros_like(carry_ref)

    logf = _log_sigmoid(gl_ref[0] + bf_ref[...]) * LOG2E
    c = carry_ref[...] + _cumsum_rows(logf, tri_ref[...])
    tp = c.shape[0]
    carry_ref[...] = c[tp - 1:tp, :]
    terms = jnp.concatenate(_split3(c), axis=1)
    qx = _dot(terms, eq_ref[...]) + oq_ref[...]
    kx = ok_ref[...] - _dot(terms, ek_ref[...])
    for h in range(N_HEADS):
        qx_ref[0, h] = qx[:, h * LANE:(h + 1) * LANE].astype(BF16)
        kx_ref[0, h] = kx[:, h * LANE:(h + 1) * LANE].astype(BF16)


def _fox_prep(gate, b_f, batch, seq, tp=512):
    bf = jnp.pad(b_f.astype(F32).reshape(1, -1), ((0, 0), (0, LANE - N_HEADS)))
    r = np.arange(tp)
    tri = jnp.asarray(r[:, None] >= r[None, :], BF16)
    eq = np.zeros((3 * LANE, N_HEADS * LANE), np.float32)
    ek = np.zeros((3 * LANE, N_HEADS * LANE), np.float32)
    oq = np.zeros((1, N_HEADS * LANE), np.float32)
    ok = np.zeros((1, N_HEADS * LANE), np.float32)
    for h in range(N_HEADS):
        for j in range(3):
            eq[j * LANE + h, h * LANE + j] = 1.0
            ek[j * LANE + h, h * LANE + 3 + j] = 1.0
            oq[0, h * LANE + 3 + j] = 1.0
            ok[0, h * LANE + j] = 1.0
    out_sds = jax.ShapeDtypeStruct((batch, N_HEADS, seq, LANE), BF16)
    return pl.pallas_call(
        _fox_prep_kernel,
        grid=(batch, seq // tp),
        in_specs=[
            pl.BlockSpec((1, tp, LANE), lambda b, i: (b, i, 0)),
            _const_spec((1, LANE)),
            _const_spec((tp, tp)),
            _const_spec((3 * LANE, N_HEADS * LANE)),
            _const_spec((3 * LANE, N_HEADS * LANE)),
            _const_spec((1, N_HEADS * LANE)),
            _const_spec((1, N_HEADS * LANE)),
        ],
        out_specs=[
            pl.BlockSpec((1, N_HEADS, tp, LANE), lambda b, i: (b, 0, i, 0)),
            pl.BlockSpec((1, N_HEADS, tp, LANE), lambda b, i: (b, 0, i, 0)),
        ],
        out_shape=[out_sds, out_sds],
        scratch_shapes=[pltpu.VMEM((1, LANE), F32)],
        compiler_params=_cparams(("arbitrary", "arbitrary")),
        name="fox_prep",
    )(gate, bf, tri, jnp.asarray(eq, BF16), jnp.asarray(ek, BF16), jnp.asarray(oq), jnp.asarray(ok))


def _flash_causal(i, t, u, qs, k_block, v_block, scr):
    n = len(qs)
    g = t // u
    per = g * n
    assert per % 2 == 0
    s_slot, p_slot, (acc, m_ref, alpha_ref) = scr[0:2], scr[2:4], scr[4:]

    def lanes(x, width):
        return jnp.concatenate([x] * (width // LANE), axis=1)

    def logits(a, j, slot, r0=0):
        s_slot[slot][r0:, :] = _dot_nt(qs[a][r0:], k_block(j))

    def pv(a, j, slot, r0=0):
        acc[a, r0:, :] = (lanes(alpha_ref[slot, r0:, :], 2 * HEAD_DIM) * acc[a, r0:, :]
                          + _dot(p_slot[slot][r0:, :], v_block(jnp.maximum(j, 0))))

    def softmax(a, slot, mask, r0=0):
        s_ref = s_slot[slot]
        if mask is None:
            load = lambda: s_ref[r0:, :]
        else:
            load = lambda: jnp.where(mask, s_ref[r0:, :], NEG_BIG)
        m_old = m_ref[a, r0:, :]
        m_new = jnp.maximum(m_old, jnp.max(load(), axis=-1, keepdims=True))
        m_ref[a, r0:, :] = m_new
        alpha_ref[slot, r0:, :] = jnp.exp2(m_old - m_new)
        p_slot[slot][r0:, :] = jnp.exp2(load() - lanes(m_new, u)).astype(BF16)

    def group(j0, tri, last):
        row0 = lambda e: 0 if tri is None or e < 0 else (e // n) * u
        for e in range(per):
            slot = e & 1
            if e + 1 < per:
                logits((e + 1) % n, j0 + (e + 1) // n, 1 - slot, row0(e + 1))
            elif not last:
                logits(0, j0 + g, 1 - slot)
            if e > 0:
                pv((e - 1) % n, j0 + (e - 1) // n, 1 - slot, row0(e - 1))
            else:
                pv(n - 1, j0 - 1, 1 - slot)
            softmax(e % n, slot, None if tri is None else tri[:t - row0(e)], row0(e))

    m_ref[...] = jnp.full(m_ref.shape, NEG_BIG, F32)
    alpha_ref[...] = jnp.ones(alpha_ref.shape, F32)
    acc[...] = jnp.zeros(acc.shape, F32)
    p_slot[1][...] = jnp.zeros(p_slot[1].shape, BF16)
    logits(0, 0, 0)

    def body(k, _):
        group(k * g, None, False)
        return 0

    lax.fori_loop(0, i, body, 0)
    r = lax.broadcasted_iota(jnp.int32, (t, u), 0)
    c = lax.broadcasted_iota(jnp.int32, (t, u), 1)
    group(i * g, c <= r, True)
    pv(n - 1, i * g + g - 1, (per - 1) & 1, (g - 1) * u)
    return [acc[a] for a in range(n)]


def _flash_scratch(n, t, u):
    return [pltpu.VMEM((t, u), F32), pltpu.VMEM((t, u), F32),
            pltpu.VMEM((t, u), BF16), pltpu.VMEM((t, u), BF16),
            pltpu.VMEM((n, t, 2 * HEAD_DIM), F32),
            pltpu.VMEM((n, t, LANE), F32), pltpu.VMEM((2, t, LANE), F32)]


def _ones_column(u):
    lane = lax.broadcasted_iota(jnp.int32, (u, LANE), 1)
    return jnp.where(lane == 0, 1.0, 0.0).astype(BF16)


def _fox_attn_kernel(q_ref, qx_ref, k_ref, kx_ref, v_ref, o_ref, *scr, t, u):
    i = pl.program_id(2)
    q = jnp.concatenate([q_ref[0], qx_ref[0, 0]], axis=1)
    ones = _ones_column(u)

    def k_block(j):
        r = pl.multiple_of(j * u, u)
        return jnp.concatenate([k_ref[0, pl.ds(r, u), :], kx_ref[0, 0, pl.ds(r, u), :]], axis=1)

    def v_block(j):
        return jnp.concatenate([v_ref[0, pl.ds(pl.multiple_of(j * u, u), u), :], ones], axis=1)

    acc, = _flash_causal(i, t, u, [q], k_block, v_block, scr)
    o_ref[0] = (acc[:, :HEAD_DIM] / acc[:, HEAD_DIM:HEAD_DIM + 1]).astype(o_ref.dtype)


def _fox_attention(p3, qx, kx, t=1024, u=256):
    batch, seq, _ = p3.shape
    return pl.pallas_call(
        functools.partial(_fox_attn_kernel, t=t, u=u),
        grid=(batch, N_HEADS, seq // t),
        in_specs=[
            pl.BlockSpec((1, t, LANE), lambda b, h, i: (b, i, 12 + h)),
            pl.BlockSpec((1, 1, t, LANE), lambda b, h, i: (b, h, i, 0)),
            pl.BlockSpec((1, seq, LANE), lambda b, h, i: (b, 0, 16 + h)),
            pl.BlockSpec((1, 1, seq, LANE), lambda b, h, i: (b, h, 0, 0)),
            pl.BlockSpec((1, seq, LANE), lambda b, h, i: (b, 0, 20 + h)),
        ],
        out_specs=pl.BlockSpec((1, t, LANE), lambda b, h, i: (b, i, h)),
        out_shape=jax.ShapeDtypeStruct((batch, seq, N_HEADS * HEAD_DIM), BF16),
        scratch_shapes=_flash_scratch(1, t, u),
        compiler_params=_cparams(("arbitrary", "arbitrary", "arbitrary")),
        name="fox_attn",
    )(p3, qx, p3, kx, p3)


def _diff_attn_kernel(q_ref, k_ref, v_ref, lq1_ref, lk1_ref, lq2_ref, lk2_ref, gs_ref, o_ref,
                      *scr, t, u, lam_init):
    i = pl.program_id(2)
    q = q_ref[0]
    lane = lax.broadcasted_iota(jnp.int32, q.shape, 1)
    zero = jnp.zeros_like(q)
    q1 = jnp.where(lane < DIFF_HALF, q, zero)
    q2 = jnp.where(lane >= DIFF_HALF, q, zero)
    ones = _ones_column(u)
    k_block = lambda j: k_ref[0, pl.ds(pl.multiple_of(j * u, u), u), :]
    v_block = lambda j: jnp.concatenate(
        [v_ref[0, pl.ds(pl.multiple_of(j * u, u), u), :], ones], axis=1)
    a1, a2 = _flash_causal(i, t, u, [q1, q2], k_block, v_block, scr)
    lam = (jnp.exp(jnp.sum(lq1_ref[...] * lk1_ref[...], axis=-1, keepdims=True))
           - jnp.exp(jnp.sum(lq2_ref[...] * lk2_ref[...], axis=-1, keepdims=True)) + lam_init)
    o = (a1[:, :HEAD_DIM] / a1[:, HEAD_DIM:HEAD_DIM + 1]
         - lam * (a2[:, :HEAD_DIM] / a2[:, HEAD_DIM:HEAD_DIM + 1]))
    o_ref[0] = (_rms(o, gs_ref[...]) * (1.0 - lam_init)).astype(o_ref.dtype)


def _diff_attention(p3, lq1, lk1, lq2, lk2, g_sub, lam_init, t=1024, u=256):
    batch, seq, _ = p3.shape
    vec = lambda a: a.reshape(1, -1).astype(F32)
    return pl.pallas_call(
        functools.partial(_diff_attn_kernel, t=t, u=u, lam_init=lam_init),
        grid=(batch, N_HEADS, seq // t),
        in_specs=[
            pl.BlockSpec((1, t, LANE), lambda b, h, i: (b, i, h)),
            pl.BlockSpec((1, seq, LANE), lambda b, h, i: (b, 0, 4 + h)),
            pl.BlockSpec((1, seq, LANE), lambda b, h, i: (b, 0, 8 + h)),
            _const_spec((1, DIFF_HALF)), _const_spec((1, DIFF_HALF)),
            _const_spec((1, DIFF_HALF)), _const_spec((1, DIFF_HALF)),
            _const_spec((1, HEAD_DIM)),
        ],
        out_specs=pl.BlockSpec((1, t, LANE), lambda b, h, i: (b, i, h)),
        out_shape=jax.ShapeDtypeStruct((batch, seq, N_HEADS * HEAD_DIM), BF16),
        scratch_shapes=_flash_scratch(2, t, u),
        compiler_params=_cparams(("arbitrary", "arbitrary", "arbitrary")),
        name="diff_attn",
    )(p3, p3, p3, vec(lq1), vec(lk1), vec(lq2), vec(lk2), vec(g_sub))


def _gla_kernel(q_ref, k_ref, v_ref, g_ref, gl_ref, wgk_ref, bgk_ref, on_ref, tri_ref,
                o_ref, st_ref, gk_ref, bc_ref, *, tc):
    i = pl.program_id(2)

    @pl.when(i == 0)
    def _():
        st_ref[...] = jnp.zeros_like(st_ref)

    x = _dot(gl_ref[0].astype(BF16), wgk_ref[...]) + bgk_ref[...]
    gk_ref[...] = _log_sigmoid(x) * (1.0 / GLA_GATE_NORM)
    C = GLA_FAST_CHUNK
    tri = tri_ref[...]
    for c in range(tc // C):
        bc_ref[c * C:(c + 1) * C, :] = _cumsum_rows(gk_ref[c * C:(c + 1) * C, :], tri)

    def finish(o, rows):
        y = _rms(o, on_ref[...])
        g = g_ref[0, rows, :].astype(F32)
        o_ref[0, rows, :] = (y * (g * jax.nn.sigmoid(g))).astype(o_ref.dtype)

    def update_state(st, k, vf, b, n):
        b_last = b[n - 1:n, :]
        k_dec = (k * jnp.exp(b_last - b)).astype(BF16)
        st_ref[...] = st * jnp.exp(b_last) + _dot(vf.T.astype(BF16), k_dec)

    safe = jnp.min(bc_ref[...]) >= -GLA_SAFE_DECAY

    @pl.when(safe)
    def _():
        r = lax.broadcasted_iota(jnp.int32, (C, C), 0)
        cc = lax.broadcasted_iota(jnp.int32, (C, C), 1)
        causal = cc <= r
        for c in range(tc // C):
            rows = slice(c * C, (c + 1) * C)
            q = q_ref[0, rows, :].astype(F32)
            k = k_ref[0, rows, :].astype(F32)
            v = v_ref[0, rows, :]
            b = bc_ref[rows, :]
            st = st_ref[...]
            qe = (q * jnp.exp(b)).astype(BF16)
            ke = (k * jnp.exp(-b)).astype(BF16)
            o = _dot_nt(qe, st.astype(BF16))
            a = jnp.where(causal, _dot_nt(qe, ke), 0.0)
            o = o + _dot(a.astype(BF16), v)
            update_state(st, k, v.astype(F32), b, C)
            finish(o, rows)

    @pl.when(jnp.logical_not(safe))
    def _():
        L, SB = GLA_CHUNK, GLA_SUB
        row = lax.broadcasted_iota(jnp.int32, (L, HEAD_DIM), 0)
        sub_row = lax.broadcasted_iota(jnp.int32, (SB, 1), 0)
        tri_l = tri[:L, :L]

        def chunk(c, _):
            rows = pl.ds(pl.multiple_of(c * L, L), L)
            q = q_ref[0, rows, :].astype(F32)
            k = k_ref[0, rows, :].astype(F32)
            v = v_ref[0, rows, :]
            vf = v.astype(F32)
            b = _cumsum_rows(gk_ref[rows, :], tri_l)
            st = st_ref[...]

            o = _dot_nt((q * jnp.exp(b)).astype(BF16), st.astype(BF16))

            qs, ks = [], []
            for s in range(1, L // SB):
                ref = b[s * SB - 1:s * SB, :]
                in_sub = (row >= s * SB) & (row < (s + 1) * SB)
                qs.append(jnp.where(in_sub, q * jnp.exp(jnp.minimum(b - ref, 0.0)), 0.0))
                ks.append(jnp.where(row < s * SB, k * jnp.exp(jnp.minimum(ref - b, 0.0)), 0.0))
            a = _dot_nt(jnp.concatenate(qs, axis=1).astype(BF16),
                        jnp.concatenate(ks, axis=1).astype(BF16))
            o = o + _dot(a.astype(BF16), v)

            diag = []
            for s in range(L // SB):
                sl = slice(s * SB, (s + 1) * SB)
                qd, kd, bd, vd = q[sl], k[sl], b[sl], vf[sl]
                acc = jnp.zeros((SB, GLA_DV), F32)
                for j in range(SB):
                    w = qd * kd[j:j + 1] * jnp.exp(jnp.minimum(bd - bd[j:j + 1], 0.0))
                    d = jnp.sum(w, axis=-1, keepdims=True)
                    acc = acc + jnp.where(sub_row >= j, d, 0.0) * vd[j:j + 1]
                diag.append(acc)
            o = o + jnp.concatenate(diag, axis=0)

            update_state(st, k, vf, b, L)
            finish(o, rows)
            return 0

        lax.fori_loop(0, tc // L, chunk, 0)


def _gla(p3, gl3, w_gk, b_gk, g_onorm, tc=1024):
    batch, seq, _ = p3.shape
    wgk = jnp.pad(w_gk, ((0, LANE - GLA_LOWRANK), (0, 0))).astype(BF16)
    r = np.arange(GLA_FAST_CHUNK)
    tri = jnp.asarray(r[:, None] >= r[None, :], BF16)
    return pl.pallas_call(
        functools.partial(_gla_kernel, tc=tc),
        grid=(batch, N_HEADS, seq // tc),
        in_specs=[
            pl.BlockSpec((1, tc, HEAD_DIM), lambda b, h, i: (b, i, h)),
            pl.BlockSpec((1, tc, HEAD_DIM), lambda b, h, i: (b, i, 4 + h)),
            pl.BlockSpec((1, tc, GLA_DV), lambda b, h, i: (b, i, 4 + h)),
            pl.BlockSpec((1, tc, GLA_DV), lambda b, h, i: (b, i, 8 + h)),
            pl.BlockSpec((1, tc, LANE), lambda b, h, i: (b, i, 0)),
            pl.BlockSpec((LANE, HEAD_DIM), lambda b, h, i: (0, h)),
            pl.BlockSpec((1, HEAD_DIM), lambda b, h, i: (0, h)),
            _const_spec((1, GLA_DV)),
            _const_spec((GLA_FAST_CHUNK, GLA_FAST_CHUNK)),
        ],
        out_specs=pl.BlockSpec((1, tc, GLA_DV), lambda b, h, i: (b, i, h)),
        out_shape=jax.ShapeDtypeStruct((batch, seq, N_HEADS * GLA_DV), BF16),
        scratch_shapes=[pltpu.VMEM((GLA_DV, HEAD_DIM), F32), pltpu.VMEM((tc, HEAD_DIM), F32),
                        pltpu.VMEM((tc, HEAD_DIM), F32)],
        compiler_params=_cparams(("arbitrary", "arbitrary", "arbitrary")),
        name="gla",
    )(p3, p3, p3, p3, gl3, wgk, b_gk.reshape(1, -1).astype(F32),
      g_onorm.reshape(1, -1).astype(F32), tri)


def _tail_kernel(*refs, mix_widths, final_norm):
    n_mix = len(mix_widths)
    h_ref = refs[0]
    o_refs = refs[1:1 + n_mix]
    wo_ref, gm_ref, wu_ref, wd_ref, gp_ref, wg_ref, p_ref, wp_ref = refs[1 + n_mix:9 + n_mix]
    gf_ref = refs[9 + n_mix] if final_norm else None
    out_ref = refs[-1]

    mixed, c0 = None, 0
    for o_ref, w in zip(o_refs, mix_widths):
        term = _dot(o_ref[...], wo_ref[c0:c0 + w, :])
        mixed = term if mixed is None else mixed + term
        c0 += w
    h = h_ref[...] + mixed
    hn = _rms(h, gm_ref[...]).astype(BF16)
    acc = h
    for c0 in range(0, D_FF, FF_CHUNK):
        u = jnp.square(jnp.maximum(_dot(hn, wu_ref[:, c0:c0 + FF_CHUNK]), 0.0))
        acc = acc + _dot(u.astype(BF16), wd_ref[c0:c0 + FF_CHUNK, :])
    h = acc
    gate = jax.nn.sigmoid(_dot(_rms(h, gp_ref[...]).astype(BF16), wg_ref[...]))
    h = h + gate * _dot(p_ref[...].astype(BF16), wp_ref[...])
    if final_norm:
        h = _rms(h, gf_ref[...])
    out_ref[...] = h


def _tail(h, mix, w_o, g_mlp, w_up, w_down, g_ple, w_gate, p_all, layer, w_proj, g_final, tm=512):
    t = h.shape[0]
    row = lambda a: a.reshape(1, -1).astype(F32)
    args = [h, *mix, w_o, row(g_mlp), w_up, w_down, row(g_ple), w_gate, p_all, w_proj]
    in_specs = [pl.BlockSpec((tm, D_MODEL), lambda i: (i, 0))]
    in_specs += [pl.BlockSpec((tm, m.shape[1]), lambda i: (i, 0)) for m in mix]
    in_specs += [_const_spec(w_o.shape), _const_spec((1, D_MODEL)), _const_spec(w_up.shape),
                 _const_spec(w_down.shape), _const_spec((1, D_MODEL)), _const_spec(w_gate.shape),
                 pl.BlockSpec((None, tm, PLE_DIM), lambda i: (layer, i, 0)),
                 _const_spec(w_proj.shape)]
    if g_final is not None:
        args.append(row(g_final))
        in_specs.append(_const_spec((1, D_MODEL)))
    return pl.pallas_call(
        functools.partial(_tail_kernel, mix_widths=tuple(m.shape[1] for m in mix),
                          final_norm=g_final is not None),
        grid=(t // tm,),
        in_specs=in_specs,
        out_specs=pl.BlockSpec((tm, D_MODEL), lambda i: (i, 0)),
        out_shape=jax.ShapeDtypeStruct((t, D_MODEL), F32),
        compiler_params=_cparams(("arbitrary",)),
        name="tail",
    )(*args)


def _split_in_weight(w_in, n_gate):
    w_gate = jnp.pad(w_in[:, MAIN_W:], ((0, 0), (0, LANE - n_gate)))
    return w_in[:, :MAIN_W].astype(BF16), w_gate.astype(BF16)


def _col_scale(scales):
    cs = np.ones((1, MAIN_W), np.float32)
    for (c0, c1), v in scales.items():
        cs[0, c0:c1] = v
    return jnp.asarray(cs)


def kernel(x, p, norm_mix, norm_mlp, norm_ple, norm_final, even_w_in, even_b_f, lam_q1, lam_k1, lam_q2, lam_k2, diff_subln, even_w_o, odd_w_in, odd_w_gk, odd_b_gk, gla_onorm, odd_w_o, w_up, w_down, w_ple_gate, w_ple_proj):
    batch, seq, _ = x.shape
    depth = p.shape[0]
    t = batch * seq
    h = x.reshape(t, D_MODEL)
    p_all = p.reshape(depth, t, PLE_DIM)
    for i in range(depth):
        j = i // 2
        g_mix = norm_mix[i].reshape(1, -1)
        if i % 2 == 0:
            w_main, w_gate = _split_in_weight(even_w_in[j], N_HEADS)
            cs = _col_scale({(0, 512): DIFF_HALF ** -0.5 * LOG2E,
                             (1536, 2048): HEAD_DIM ** -0.5 * LOG2E})
            proj, gate = _inproj(h, g_mix, w_main, w_gate, cs)
            p3 = proj.reshape(batch, seq, MAIN_W)
            lam_init = 0.8 - 0.6 * math.exp(-0.3 * i)
            oa = _diff_attention(p3, lam_q1[j], lam_k1[j], lam_q2[j], lam_k2[j],
                                 diff_subln[j], lam_init)
            qx, kx = _fox_prep(gate.reshape(batch, seq, LANE), even_b_f[j], batch, seq)
            ob = _fox_attention(p3, qx, kx)
            mix = [oa.reshape(t, -1), ob.reshape(t, -1)]
            w_o = even_w_o[j]
        else:
            w_main, w_gate = _split_in_weight(odd_w_in[j], GLA_LOWRANK)
            cs = _col_scale({(0, 512): HEAD_DIM ** -0.5})
            proj, gate = _inproj(h, g_mix, w_main, w_gate, cs)
            o = _gla(proj.reshape(batch, seq, MAIN_W), gate.reshape(batch, seq, LANE),
                     odd_w_gk[j], odd_b_gk[j], gla_onorm[j])
            mix = [o.reshape(t, -1)]
            w_o = odd_w_o[j]
        h = _tail(h, mix, w_o.astype(BF16), norm_mlp[i], w_up[i].astype(BF16),
                  w_down[i].astype(BF16), norm_ple[i], w_ple_gate[i].astype(BF16), p_all, i,
                  w_ple_proj[i].astype(BF16), norm_final if i == depth - 1 else None)
    return h.reshape(batch, seq, D_MODEL)
```

```python
import functools
import math

import jax
import jax.numpy as jnp
import numpy as np
from jax import lax
from jax.experimental import pallas as pl
from jax.experimental.pallas import tpu as pltpu

F32 = jnp.float32
BF16 = jnp.bfloat16

D_MODEL = 1024
PLE_DIM = 256
N_HEADS = 4
HEAD_DIM = 128
DIFF_HALF = 64
GLA_DV = 256
GLA_LOWRANK = 16
GLA_GATE_NORM = 16.0
GLA_CHUNK = 64
GLA_SUB = 16
GLA_FAST_CHUNK = 128
GLA_SAFE_DECAY = 60.0
FOX_EXT_STRIDE = 8
D_FF = 4 * D_MODEL
FF_CHUNK = 1024
EPS = 1e-6
LOG2E = 1.4426950408889634
NEG_BIG = -1e30
LANE = 128
MAIN_W = 3072

VMEM_LIMIT = 56 * 1024 * 1024


def _cparams(sem):
    return pltpu.CompilerParams(dimension_semantics=sem, vmem_limit_bytes=VMEM_LIMIT)


def _const_spec(shape):
    nd = len(shape)
    return pl.BlockSpec(shape, lambda *_: (0,) * nd, pipeline_mode=pl.Buffered(1))


def _layer_spec(shape, layer):
    nd = len(shape)
    return pl.BlockSpec((None,) + tuple(shape), lambda *_: (layer,) + (0,) * nd,
                        pipeline_mode=pl.Buffered(1))


def _rms(x, g):
    return x * lax.rsqrt(jnp.mean(x * x, axis=-1, keepdims=True) + EPS) * g


def _log_sigmoid(x):
    return jnp.minimum(x, 0.0) - jnp.log(1.0 + jnp.exp(-jnp.abs(x)))


def _split3(x):
    hi = x.astype(BF16)
    r1 = x - hi.astype(F32)
    mid = r1.astype(BF16)
    lo = (r1 - mid.astype(F32)).astype(BF16)
    return hi, mid, lo


def _dot(a, b):
    return jnp.dot(a, b, preferred_element_type=F32)


def _cumsum_rows(x, tri):
    w = x.shape[1]
    y = _dot(tri, jnp.concatenate(_split3(x), axis=1))
    return y[:, :w] + y[:, w:2 * w] + y[:, 2 * w:]


def _dot_nt(a, b):
    return lax.dot_general(a, b, (((1,), (1,)), ((), ())), preferred_element_type=F32)


def _project(x_ref, g_ref, w_ref, wg_ref, cs_ref, o_ref, wb_scr, n_chunk, gate_steps):
    @pl.when(pl.program_id(0) == 0)
    def _():
        for c0 in range(0, MAIN_W, n_chunk):
            wb_scr[:, c0:c0 + n_chunk] = w_ref[:, c0:c0 + n_chunk].astype(BF16)

    hn = _rms(x_ref[...], g_ref[...]).astype(BF16)
    value = _dot(hn, wg_ref[...])
    steps = list(gate_steps)
    for c0 in range(0, MAIN_W, n_chunk):
        y = _dot(hn, wb_scr[:, c0:c0 + n_chunk]) * cs_ref[:, c0:c0 + n_chunk]
        o_ref[:, c0:c0 + n_chunk] = y.astype(o_ref.dtype)
        if steps:
            value = steps.pop(0)(value)
    for step in steps:
        value = step(value)


def _inproj_fox_kernel(x_ref, g_ref, w_ref, wg_ref, cs_ref, bf_ref, tri_ref, eq_ref, ek_ref,
                       oq_ref, ok_ref, o_ref, qx_ref, kx_ref, wb_scr, carry_ref,
                       *, n_chunk, steps_per_seq):
    @pl.when(pl.program_id(0) % steps_per_seq == 0)
    def _():
        carry_ref[...] = jnp.zeros_like(carry_ref)

    def cumulate(gate):
        logf = _log_sigmoid(gate + bf_ref[...]) * LOG2E
        c = carry_ref[...] + _cumsum_rows(logf, tri_ref[...])
        tm = c.shape[0]
        carry_ref[...] = c[tm - 1:tm, :]
        return c

    def extend(c):
        terms = jnp.concatenate(_split3(c), axis=1)
        qx_ref[...] = (_dot(terms, eq_ref[...]) + oq_ref[...]).astype(BF16)
        kx_ref[...] = (ok_ref[...] - _dot(terms, ek_ref[...])).astype(BF16)

    _project(x_ref, g_ref, w_ref, wg_ref, cs_ref, o_ref, wb_scr, n_chunk, [cumulate, extend])


def _inproj_gla_kernel(x_ref, g_ref, w_ref, wg_ref, cs_ref, wgk_ref, bgk_ref, tri_ref,
                       o_ref, bc_ref, wb_scr, *, n_chunk):
    C = GLA_FAST_CHUNK

    def decay(gl):
        x = _dot(gl.astype(BF16), wgk_ref[...]) + bgk_ref[...]
        return _log_sigmoid(x) * (1.0 / GLA_GATE_NORM)

    def cumulate(c):
        def step(gk):
            bc_ref[c * C:(c + 1) * C, :] = _cumsum_rows(gk[c * C:(c + 1) * C, :], tri_ref[...])
            return gk
        return step

    n_row_chunks = x_ref.shape[0] // C
    _project(x_ref, g_ref, w_ref, wg_ref, cs_ref, o_ref, wb_scr, n_chunk,
             [decay] + [cumulate(c) for c in range(n_row_chunks)])


def _inproj_common(x, g, w_in, layer, n_gate, colscale, tm):
    w_gate = jnp.pad(w_in[layer, :, MAIN_W:], ((0, 0), (0, LANE - n_gate))).astype(BF16)
    args = [x, g.reshape(1, -1).astype(F32), w_in, w_gate, colscale]
    in_specs = [
        pl.BlockSpec((tm, D_MODEL), lambda i: (i, 0)),
        _const_spec((1, D_MODEL)),
        _layer_spec((D_MODEL, MAIN_W), layer),
        _const_spec((D_MODEL, LANE)),
        _const_spec((1, MAIN_W)),
    ]
    return args, in_specs


def _inproj_fox(x, g, w_in, layer, b_f, colscale, seq, tm=512):
    t = x.shape[0]
    args, in_specs = _inproj_common(x, g, w_in, layer, N_HEADS, colscale, tm)
    bf = jnp.pad(b_f.astype(F32).reshape(1, -1), ((0, 0), (0, LANE - N_HEADS)))
    r = np.arange(tm)
    tri = jnp.asarray(r[:, None] >= r[None, :], BF16)
    eq = np.zeros((3 * LANE, LANE), np.float32)
    ek = np.zeros((3 * LANE, LANE), np.float32)
    oq = np.zeros((1, LANE), np.float32)
    ok = np.zeros((1, LANE), np.float32)
    for h in range(N_HEADS):
        for j in range(3):
            eq[j * LANE + h, FOX_EXT_STRIDE * h + j] = 1.0
            ek[j * LANE + h, FOX_EXT_STRIDE * h + 3 + j] = 1.0
            oq[0, FOX_EXT_STRIDE * h + 3 + j] = 1.0
            ok[0, FOX_EXT_STRIDE * h + j] = 1.0
    args += [bf, tri, jnp.asarray(eq, BF16), jnp.asarray(ek, BF16), jnp.asarray(oq), jnp.asarray(ok)]
    in_specs += [_const_spec((1, LANE)), _const_spec((tm, tm)), _const_spec((3 * LANE, LANE)),
                 _const_spec((3 * LANE, LANE)), _const_spec((1, LANE)), _const_spec((1, LANE))]
    row_spec = lambda w: pl.BlockSpec((tm, w), lambda i: (i, 0))
    return pl.pallas_call(
        functools.partial(_inproj_fox_kernel, n_chunk=512, steps_per_seq=seq // tm),
        grid=(t // tm,),
        in_specs=in_specs,
        out_specs=[row_spec(MAIN_W), row_spec(LANE), row_spec(LANE)],
        out_shape=[jax.ShapeDtypeStruct((t, MAIN_W), BF16), jax.ShapeDtypeStruct((t, LANE), BF16),
                   jax.ShapeDtypeStruct((t, LANE), BF16)],
        scratch_shapes=[pltpu.VMEM((D_MODEL, MAIN_W), BF16), pltpu.VMEM((1, LANE), F32)],
        compiler_params=_cparams(("arbitrary",)),
        name="inproj_fox",
    )(*args)


def _inproj_gla(x, g, w_in, layer, w_gk, b_gk, colscale, tm=512):
    t = x.shape[0]
    args, in_specs = _inproj_common(x, g, w_in, layer, GLA_LOWRANK, colscale, tm)
    wgk = jnp.pad(w_gk, ((0, LANE - GLA_LOWRANK), (0, 0))).astype(BF16)
    r = np.arange(GLA_FAST_CHUNK)
    tri = jnp.asarray(r[:, None] >= r[None, :], BF16)
    args += [wgk, b_gk.reshape(1, -1).astype(F32), tri]
    in_specs += [_const_spec(wgk.shape), _const_spec((1, N_HEADS * HEAD_DIM)),
                 _const_spec((GLA_FAST_CHUNK, GLA_FAST_CHUNK))]
    row_spec = lambda w: pl.BlockSpec((tm, w), lambda i: (i, 0))
    return pl.pallas_call(
        functools.partial(_inproj_gla_kernel, n_chunk=512),
        grid=(t // tm,),
        in_specs=in_specs,
        out_specs=[row_spec(MAIN_W), row_spec(N_HEADS * HEAD_DIM)],
        out_shape=[jax.ShapeDtypeStruct((t, MAIN_W), BF16),
                   jax.ShapeDtypeStruct((t, N_HEADS * HEAD_DIM), F32)],
        scratch_shapes=[pltpu.VMEM((D_MODEL, MAIN_W), BF16)],
        compiler_params=_cparams(("arbitrary",)),
        name="inproj_gla",
    )(*args)


def _flash_causal(i, t, u, qs, k_block, v_block, scr):
    n = len(qs)
    g = t // u
    per = g * n
    assert per % 2 == 0
    s_slot, p_slot, (acc, m_ref, alpha_ref) = scr[0:2], scr[2:4], scr[4:]

    def lanes(x, width):
        return jnp.concatenate([x] * (width // LANE), axis=1)

    def logits(a, j, slot, r0=0):
        s_slot[slot][r0:, :] = _dot_nt(qs[a][r0:], k_block(j))

    def pv(a, j, slot, r0=0):
        acc[a, r0:, :] = (lanes(alpha_ref[slot, r0:, :], 2 * HEAD_DIM) * acc[a, r0:, :]
                          + _dot(p_slot[slot][r0:, :], v_block(jnp.maximum(j, 0))))

    def softmax(a, slot, mask, r0=0):
        s_ref = s_slot[slot]
        if mask is None:
            load = lambda: s_ref[r0:, :]
        else:
            load = lambda: jnp.where(mask, s_ref[r0:, :], NEG_BIG)
        m_old = m_ref[a, r0:, :]
        m_new = jnp.maximum(m_old, jnp.max(load(), axis=-1, keepdims=True))
        m_ref[a, r0:, :] = m_new
        alpha_ref[slot, r0:, :] = jnp.exp2(m_old - m_new)
        p_slot[slot][r0:, :] = jnp.exp2(load() - lanes(m_new, u)).astype(BF16)

    def group(j0, tri, last):
        row0 = lambda e: 0 if tri is None or e < 0 else (e // n) * u
        for e in range(per):
            slot = e & 1
            if e + 1 < per:
                logits((e + 1) % n, j0 + (e + 1) // n, 1 - slot, row0(e + 1))
            elif not last:
                logits(0, j0 + g, 1 - slot)
            if e > 0:
                pv((e - 1) % n, j0 + (e - 1) // n, 1 - slot, row0(e - 1))
            else:
                pv(n - 1, j0 - 1, 1 - slot)
            softmax(e % n, slot, None if tri is None else tri[:t - row0(e)], row0(e))

    m_ref[...] = jnp.full(m_ref.shape, NEG_BIG, F32)
    alpha_ref[...] = jnp.ones(alpha_ref.shape, F32)
    acc[...] = jnp.zeros(acc.shape, F32)
    p_slot[1][...] = jnp.zeros(p_slot[1].shape, BF16)
    logits(0, 0, 0)

    def body(k, _):
        group(k * g, None, False)
        return 0

    lax.fori_loop(0, i, body, 0)
    r = lax.broadcasted_iota(jnp.int32, (t, u), 0)
    c = lax.broadcasted_iota(jnp.int32, (t, u), 1)
    group(i * g, c <= r, True)
    pv(n - 1, i * g + g - 1, (per - 1) & 1, (g - 1) * u)
    return [acc[a] for a in range(n)]


def _flash_scratch(n, t, u):
    return [pltpu.VMEM((t, u), F32), pltpu.VMEM((t, u), F32),
            pltpu.VMEM((t, u), BF16), pltpu.VMEM((t, u), BF16),
            pltpu.VMEM((n, t, 2 * HEAD_DIM), F32),
            pltpu.VMEM((n, t, LANE), F32), pltpu.VMEM((2, t, LANE), F32)]


def _ones_column(u):
    lane = lax.broadcasted_iota(jnp.int32, (u, LANE), 1)
    return jnp.where(lane == 0, 1.0, 0.0).astype(BF16)


def _fox_attn_kernel(q_ref, qx_ref, k_ref, kx_ref, v_ref, o_ref, *scr, t, u):
    h, i = pl.program_id(1), pl.program_id(2)
    qx = qx_ref[0]
    lane = lax.broadcasted_iota(jnp.int32, qx.shape, 1)
    qx = jnp.where(lane // FOX_EXT_STRIDE == h, qx, jnp.zeros_like(qx))
    q = jnp.concatenate([q_ref[0], qx], axis=1)
    ones = _ones_column(u)

    def k_block(j):
        r = pl.multiple_of(j * u, u)
        return jnp.concatenate([k_ref[0, pl.ds(r, u), :], kx_ref[0, pl.ds(r, u), :]], axis=1)

    def v_block(j):
        return jnp.concatenate([v_ref[0, pl.ds(pl.multiple_of(j * u, u), u), :], ones], axis=1)

    acc, = _flash_causal(i, t, u, [q], k_block, v_block, scr)
    o_ref[0] = (acc[:, :HEAD_DIM] / acc[:, HEAD_DIM:HEAD_DIM + 1]).astype(o_ref.dtype)


def _fox_attention(p3, qx, kx, t=1024, u=256):
    batch, seq, _ = p3.shape
    return pl.pallas_call(
        functools.partial(_fox_attn_kernel, t=t, u=u),
        grid=(batch, N_HEADS, seq // t),
        in_specs=[
            pl.BlockSpec((1, t, LANE), lambda b, h, i: (b, i, 12 + h)),
            pl.BlockSpec((1, t, LANE), lambda b, h, i: (b, i, 0)),
            pl.BlockSpec((1, seq, LANE), lambda b, h, i: (b, 0, 16 + h)),
            pl.BlockSpec((1, seq, LANE), lambda b, h, i: (b, 0, 0)),
            pl.BlockSpec((1, seq, LANE), lambda b, h, i: (b, 0, 20 + h)),
        ],
        out_specs=pl.BlockSpec((1, t, LANE), lambda b, h, i: (b, i, h)),
        out_shape=jax.ShapeDtypeStruct((batch, seq, N_HEADS * HEAD_DIM), BF16),
        scratch_shapes=_flash_scratch(1, t, u),
        compiler_params=_cparams(("arbitrary", "arbitrary", "arbitrary")),
        name="fox_attn",
    )(p3, qx, p3, kx, p3)


def _diff_attn_kernel(q_ref, k_ref, v_ref, lq1_ref, lk1_ref, lq2_ref, lk2_ref, gs_ref, o_ref,
                      *scr, t, u, lam_init):
    i = pl.program_id(2)
    q = q_ref[0]
    lane = lax.broadcasted_iota(jnp.int32, q.shape, 1)
    zero = jnp.zeros_like(q)
    q1 = jnp.where(lane < DIFF_HALF, q, zero)
    q2 = jnp.where(lane >= DIFF_HALF, q, zero)
    ones = _ones_column(u)
    k_block = lambda j: k_ref[0, pl.ds(pl.multiple_of(j * u, u), u), :]
    v_block = lambda j: jnp.concatenate(
        [v_ref[0, pl.ds(pl.multiple_of(j * u, u), u), :], ones], axis=1)
    a1, a2 = _flash_causal(i, t, u, [q1, q2], k_block, v_block, scr)
    lam = (jnp.exp(jnp.sum(lq1_ref[...] * lk1_ref[...], axis=-1, keepdims=True))
           - jnp.exp(jnp.sum(lq2_ref[...] * lk2_ref[...], axis=-1, keepdims=True)) + lam_init)
    o = (a1[:, :HEAD_DIM] / a1[:, HEAD_DIM:HEAD_DIM + 1]
         - lam * (a2[:, :HEAD_DIM] / a2[:, HEAD_DIM:HEAD_DIM + 1]))
    o_ref[0] = (_rms(o, gs_ref[...]) * (1.0 - lam_init)).astype(o_ref.dtype)


def _diff_attention(p3, lq1, lk1, lq2, lk2, g_sub, lam_init, t=1024, u=256):
    batch, seq, _ = p3.shape
    vec = lambda a: a.reshape(1, -1).astype(F32)
    return pl.pallas_call(
        functools.partial(_diff_attn_kernel, t=t, u=u, lam_init=lam_init),
        grid=(batch, N_HEADS, seq // t),
        in_specs=[
            pl.BlockSpec((1, t, LANE), lambda b, h, i: (b, i, h)),
            pl.BlockSpec((1, seq, LANE), lambda b, h, i: (b, 0, 4 + h)),
            pl.BlockSpec((1, seq, LANE), lambda b, h, i: (b, 0, 8 + h)),
            _const_spec((1, DIFF_HALF)), _const_spec((1, DIFF_HALF)),
            _const_spec((1, DIFF_HALF)), _const_spec((1, DIFF_HALF)),
            _const_spec((1, HEAD_DIM)),
        ],
        out_specs=pl.BlockSpec((1, t, LANE), lambda b, h, i: (b, i, h)),
        out_shape=jax.ShapeDtypeStruct((batch, seq, N_HEADS * HEAD_DIM), BF16),
        scratch_shapes=_flash_scratch(2, t, u),
        compiler_params=_cparams(("arbitrary", "arbitrary", "arbitrary")),
        name="diff_attn",
    )(p3, p3, p3, vec(lq1), vec(lk1), vec(lq2), vec(lk2), vec(g_sub))


def _gla_kernel(q_ref, k_ref, v_ref, g_ref, bc_ref, on_ref, o_ref, st_ref, *, tc):
    i = pl.program_id(2)

    @pl.when(i == 0)
    def _():
        st_ref[...] = jnp.zeros_like(st_ref)

    C = GLA_FAST_CHUNK

    def finish(o, rows):
        y = _rms(o, on_ref[...])
        g = g_ref[0, rows, :].astype(F32)
        o_ref[0, rows, :] = (y * (g * jax.nn.sigmoid(g))).astype(o_ref.dtype)

    def update_state(st, k, vf, b, n):
        b_last = b[n - 1:n, :]
        k_dec = (k * jnp.exp(b_last - b)).astype(BF16)
        st_ref[...] = st * jnp.exp(b_last) + _dot(vf.T.astype(BF16), k_dec)

    safe = jnp.min(bc_ref[0]) >= -GLA_SAFE_DECAY

    @pl.when(safe)
    def _():
        r = lax.broadcasted_iota(jnp.int32, (C, C), 0)
        cc = lax.broadcasted_iota(jnp.int32, (C, C), 1)
        causal = cc <= r
        for c in range(tc // C):
            rows = slice(c * C, (c + 1) * C)
            q = q_ref[0, rows, :].astype(F32)
            k = k_ref[0, rows, :].astype(F32)
            v = v_ref[0, rows, :]
            b = bc_ref[0, rows, :]
            st = st_ref[...]
            qe = (q * jnp.exp(b)).astype(BF16)
            ke = (k * jnp.exp(-b)).astype(BF16)
            o = _dot_nt(qe, st.astype(BF16))
            a = jnp.where(causal, _dot_nt(qe, ke), 0.0)
            o = o + _dot(a.astype(BF16), v)
            update_state(st, k, v.astype(F32), b, C)
            finish(o, rows)

    @pl.when(jnp.logical_not(safe))
    def _():
        L, SB = GLA_CHUNK, GLA_SUB
        row = lax.broadcasted_iota(jnp.int32, (L, HEAD_DIM), 0)
        sub_row = lax.broadcasted_iota(jnp.int32, (SB, 1), 0)

        def chunk(c, _):
            r0 = pl.multiple_of(c * L, L)
            rows = pl.ds(r0, L)
            q = q_ref[0, rows, :].astype(F32)
            k = k_ref[0, rows, :].astype(F32)
            v = v_ref[0, rows, :]
            vf = v.astype(F32)
            before = bc_ref[0, pl.ds(jnp.maximum(r0 - 1, 0), 1), :]
            b = bc_ref[0, rows, :] - jnp.where(r0 % C == 0, 0.0, before)
            st = st_ref[...]

            o = _dot_nt((q * jnp.exp(b)).astype(BF16), st.astype(BF16))

            qs, ks = [], []
            for s in range(1, L // SB):
                ref = b[s * SB - 1:s * SB, :]
                in_sub = (row >= s * SB) & (row < (s + 1) * SB)
                qs.append(jnp.where(in_sub, q * jnp.exp(jnp.minimum(b - ref, 0.0)), 0.0))
                ks.append(jnp.where(row < s * SB, k * jnp.exp(jnp.minimum(ref - b, 0.0)), 0.0))
            a = _dot_nt(jnp.concatenate(qs, axis=1).astype(BF16),
                        jnp.concatenate(ks, axis=1).astype(BF16))
            o = o + _dot(a.astype(BF16), v)

            diag = []
            for s in range(L // SB):
                sl = slice(s * SB, (s + 1) * SB)
                qd, kd, bd, vd = q[sl], k[sl], b[sl], vf[sl]
                acc = jnp.zeros((SB, GLA_DV), F32)
                for j in range(SB):
                    w = qd * kd[j:j + 1] * jnp.exp(jnp.minimum(bd - bd[j:j + 1], 0.0))
                    d = jnp.sum(w, axis=-1, keepdims=True)
                    acc = acc + jnp.where(sub_row >= j, d, 0.0) * vd[j:j + 1]
                diag.append(acc)
            o = o + jnp.concatenate(diag, axis=0)

            update_state(st, k, vf, b, L)
            finish(o, rows)
            return 0

        lax.fori_loop(0, tc // L, chunk, 0)


def _gla(p3, bc3, g_onorm, tc=1024):
    batch, seq, _ = p3.shape
    return pl.pallas_call(
        functools.partial(_gla_kernel, tc=tc),
        grid=(batch, N_HEADS, seq // tc),
        in_specs=[
            pl.BlockSpec((1, tc, HEAD_DIM), lambda b, h, i: (b, i, h)),
            pl.BlockSpec((1, tc, HEAD_DIM), lambda b, h, i: (b, i, 4 + h)),
            pl.BlockSpec((1, tc, GLA_DV), lambda b, h, i: (b, i, 4 + h)),
            pl.BlockSpec((1, tc, GLA_DV), lambda b, h, i: (b, i, 8 + h)),
            pl.BlockSpec((1, tc, HEAD_DIM), lambda b, h, i: (b, i, h)),
            _const_spec((1, GLA_DV)),
        ],
        out_specs=pl.BlockSpec((1, tc, GLA_DV), lambda b, h, i: (b, i, h)),
        out_shape=jax.ShapeDtypeStruct((batch, seq, N_HEADS * GLA_DV), BF16),
        scratch_shapes=[pltpu.VMEM((GLA_DV, HEAD_DIM), F32)],
        compiler_params=_cparams(("arbitrary", "arbitrary", "arbitrary")),
        name="gla",
    )(p3, p3, p3, p3, bc3, g_onorm.reshape(1, -1).astype(F32))


def _tail_kernel(*refs, mix_widths, final_norm):
    n_mix = len(mix_widths)
    h_ref = refs[0]
    o_refs = refs[1:1 + n_mix]
    wo_ref, gm_ref, wu_ref, wd_ref, gp_ref, wg_ref, p_ref, wp_ref = refs[1 + n_mix:9 + n_mix]
    gf_ref = refs[9 + n_mix] if final_norm else None
    out_ref = refs[-1]

    mixed, c0 = None, 0
    for o_ref, w in zip(o_refs, mix_widths):
        term = _dot(o_ref[...], wo_ref[c0:c0 + w, :])
        mixed = term if mixed is None else mixed + term
        c0 += w
    h = h_ref[...] + mixed
    hn = _rms(h, gm_ref[...]).astype(BF16)
    acc = h
    for c0 in range(0, D_FF, FF_CHUNK):
        u = jnp.square(jnp.maximum(_dot(hn, wu_ref[:, c0:c0 + FF_CHUNK]), 0.0))
        acc = acc + _dot(u.astype(BF16), wd_ref[c0:c0 + FF_CHUNK, :])
    h = acc
    gate = jax.nn.sigmoid(_dot(_rms(h, gp_ref[...]).astype(BF16), wg_ref[...]))
    h = h + gate * _dot(p_ref[...].astype(BF16), wp_ref[...])
    if final_norm:
        h = _rms(h, gf_ref[...])
    out_ref[...] = h


def _tail(h, mix, w_o, w_o_layer, g_mlp, w_up, w_down, g_ple, w_gate, p_all, w_proj, layer, g_final,
          tm=512):
    t = h.shape[0]
    row = lambda a: a.reshape(1, -1).astype(F32)
    args = [h, *mix, w_o, row(g_mlp), w_up, w_down, row(g_ple), w_gate, p_all, w_proj]
    in_specs = [pl.BlockSpec((tm, D_MODEL), lambda i: (i, 0))]
    in_specs += [pl.BlockSpec((tm, m.shape[1]), lambda i: (i, 0)) for m in mix]
    in_specs += [_layer_spec(w_o.shape[1:], w_o_layer), _const_spec((1, D_MODEL)),
                 _layer_spec(w_up.shape[1:], layer), _layer_spec(w_down.shape[1:], layer),
                 _const_spec((1, D_MODEL)), _layer_spec(w_gate.shape[1:], layer),
                 pl.BlockSpec((None, tm, PLE_DIM), lambda i: (layer, i, 0)),
                 _layer_spec(w_proj.shape[1:], layer)]
    if g_final is not None:
        args.append(row(g_final))
        in_specs.append(_const_spec((1, D_MODEL)))
    return pl.pallas_call(
        functools.partial(_tail_kernel, mix_widths=tuple(m.shape[1] for m in mix),
                          final_norm=g_final is not None),
        grid=(t // tm,),
        in_specs=in_specs,
        out_specs=pl.BlockSpec((tm, D_MODEL), lambda i: (i, 0)),
        out_shape=jax.ShapeDtypeStruct((t, D_MODEL), F32),
        compiler_params=_cparams(("arbitrary",)),
        name="tail",
    )(*args)


def _col_scale(scales):
    cs = np.ones((1, MAIN_W), np.float32)
    for (c0, c1), v in scales.items():
        cs[0, c0:c1] = v
    return jnp.asarray(cs)


def kernel(x, p, norm_mix, norm_mlp, norm_ple, norm_final, even_w_in, even_b_f, lam_q1, lam_k1, lam_q2, lam_k2, diff_subln, even_w_o, odd_w_in, odd_w_gk, odd_b_gk, gla_onorm, odd_w_o, w_up, w_down, w_ple_gate, w_ple_proj):
    batch, seq, _ = x.shape
    depth = p.shape[0]
    t = batch * seq
    h = x.reshape(t, D_MODEL)
    p_all = p.reshape(depth, t, PLE_DIM)
    even_w_o, odd_w_o, w_up, w_down, w_ple_gate, w_ple_proj = (
        w.astype(BF16) for w in (even_w_o, odd_w_o, w_up, w_down, w_ple_gate, w_ple_proj))
    for i in range(depth):
        j = i // 2
        if i % 2 == 0:
            cs = _col_scale({(0, 512): DIFF_HALF ** -0.5 * LOG2E,
                             (1536, 2048): HEAD_DIM ** -0.5 * LOG2E})
            proj, qx, kx = _inproj_fox(h, norm_mix[i], even_w_in, j, even_b_f[j], cs, seq)
            p3 = proj.reshape(batch, seq, MAIN_W)
            lam_init = 0.8 - 0.6 * math.exp(-0.3 * i)
            oa = _diff_attention(p3, lam_q1[j], lam_k1[j], lam_q2[j], lam_k2[j],
                                 diff_subln[j], lam_init)
            ob = _fox_attention(p3, qx.reshape(batch, seq, LANE), kx.reshape(batch, seq, LANE))
            mix = [oa.reshape(t, -1), ob.reshape(t, -1)]
            w_o = even_w_o
        else:
            cs = _col_scale({(0, 512): HEAD_DIM ** -0.5})
            proj, bc = _inproj_gla(h, norm_mix[i], odd_w_in, j, odd_w_gk[j], odd_b_gk[j], cs)
            o = _gla(proj.reshape(batch, seq, MAIN_W), bc.reshape(batch, seq, -1), gla_onorm[j])
            mix = [o.reshape(t, -1)]
            w_o = odd_w_o
        h = _tail(h, mix, w_o, j, norm_mlp[i], w_up, w_down, norm_ple[i], w_ple_gate, p_all,
                  w_ple_proj, i, norm_final if i == depth - 1 else None)
    return h.reshape(batch, seq, D_MODEL)
```

```python
import functools
import math

import jax
import jax.numpy as jnp
import numpy as np
from jax import lax
from jax.experimental import pallas as pl
from jax.experimental.pallas import tpu as pltpu

F32 = jnp.float32
BF16 = jnp.bfloat16

D_MODEL = 1024
PLE_DIM = 256
N_HEADS = 4
HEAD_DIM = 128
DIFF_HALF = 64
GLA_DV = 256
GLA_LOWRANK = 16
GLA_GATE_NORM = 16.0
GLA_CHUNK = 64
GLA_SUB = 16
GLA_FAST_CHUNK = 128
GLA_SAFE_DECAY = 60.0
FOX_EXT_STRIDE = 8
D_FF = 4 * D_MODEL
FF_CHUNK = 1024
EPS = 1e-6
LOG2E = 1.4426950408889634
NEG_BIG = -1e30
LANE = 128
MAIN_W = 3072

VMEM_LIMIT = 56 * 1024 * 1024


def _cparams(sem):
    return pltpu.CompilerParams(dimension_semantics=sem, vmem_limit_bytes=VMEM_LIMIT)


def _const_spec(shape):
    nd = len(shape)
    return pl.BlockSpec(shape, lambda *_: (0,) * nd, pipeline_mode=pl.Buffered(1))


def _layer_spec(shape, layer):
    nd = len(shape)
    return pl.BlockSpec((None,) + tuple(shape), lambda *_: (layer,) + (0,) * nd,
                        pipeline_mode=pl.Buffered(1))


def _rms(x, g):
    return x * lax.rsqrt(jnp.mean(x * x, axis=-1, keepdims=True) + EPS) * g


def _log_sigmoid(x):
    return jnp.minimum(x, 0.0) - jnp.log(1.0 + jnp.exp(-jnp.abs(x)))


def _split3(x):
    hi = x.astype(BF16)
    r1 = x - hi.astype(F32)
    mid = r1.astype(BF16)
    lo = (r1 - mid.astype(F32)).astype(BF16)
    return hi, mid, lo


def _dot(a, b):
    return jnp.dot(a, b, preferred_element_type=F32)


def _cumsum_rows(x, tri):
    w = x.shape[1]
    y = _dot(tri, jnp.concatenate(_split3(x), axis=1))
    return y[:, :w] + y[:, w:2 * w] + y[:, 2 * w:]


def _dot_nt(a, b):
    return lax.dot_general(a, b, (((1,), (1,)), ((), ())), preferred_element_type=F32)


def _project(x_ref, g_ref, w_ref, wg_ref, cs_ref, o_ref, wb_scr, n_chunk, gate_steps):
    @pl.when(pl.program_id(0) == 0)
    def _():
        for c0 in range(0, MAIN_W, n_chunk):
            wb_scr[:, c0:c0 + n_chunk] = w_ref[c0:c0 + n_chunk, :].T.astype(BF16)

    hn = _rms(x_ref[...], g_ref[...]).astype(BF16)
    value = _dot(hn, wg_ref[...])
    steps = list(gate_steps)
    for c0 in range(0, MAIN_W, n_chunk):
        y = _dot(hn, wb_scr[:, c0:c0 + n_chunk]) * cs_ref[:, c0:c0 + n_chunk]
        o_ref[:, c0:c0 + n_chunk] = y.astype(o_ref.dtype)
        if steps:
            value = steps.pop(0)(value)
    for step in steps:
        value = step(value)


def _inproj_fox_kernel(x_ref, g_ref, w_ref, wg_ref, cs_ref, bf_ref, tri_ref, eq_ref, ek_ref,
                       oq_ref, ok_ref, o_ref, qx_ref, kx_ref, wb_scr, carry_ref,
                       *, n_chunk, steps_per_seq):
    @pl.when(pl.program_id(0) % steps_per_seq == 0)
    def _():
        carry_ref[...] = jnp.zeros_like(carry_ref)

    def cumulate(gate):
        logf = _log_sigmoid(gate + bf_ref[...]) * LOG2E
        c = carry_ref[...] + _cumsum_rows(logf, tri_ref[...])
        tm = c.shape[0]
        carry_ref[...] = c[tm - 1:tm, :]
        return c

    def extend(c):
        terms = jnp.concatenate(_split3(c), axis=1)
        qx_ref[...] = (_dot(terms, eq_ref[...]) + oq_ref[...]).astype(BF16)
        kx_ref[...] = (ok_ref[...] - _dot(terms, ek_ref[...])).astype(BF16)

    _project(x_ref, g_ref, w_ref, wg_ref, cs_ref, o_ref, wb_scr, n_chunk, [cumulate, extend])


def _inproj_gla_kernel(x_ref, g_ref, w_ref, wg_ref, cs_ref, wgk_ref, bgk_ref, tri_ref,
                       o_ref, bc_ref, wb_scr, *, n_chunk):
    C = GLA_FAST_CHUNK

    def decay(gl):
        x = _dot(gl.astype(BF16), wgk_ref[...]) + bgk_ref[...]
        return _log_sigmoid(x) * (1.0 / GLA_GATE_NORM)

    def cumulate(c):
        def step(gk):
            bc_ref[c * C:(c + 1) * C, :] = _cumsum_rows(gk[c * C:(c + 1) * C, :], tri_ref[...])
            return gk
        return step

    n_row_chunks = x_ref.shape[0] // C
    _project(x_ref, g_ref, w_ref, wg_ref, cs_ref, o_ref, wb_scr, n_chunk,
             [decay] + [cumulate(c) for c in range(n_row_chunks)])


def _inproj_common(x, g, w_in, layer, n_gate, colscale, tm):
    w_gate = jnp.pad(w_in[layer, :, MAIN_W:], ((0, 0), (0, LANE - n_gate))).astype(BF16)
    args = [x, g.reshape(1, -1).astype(F32), jnp.swapaxes(w_in, 1, 2), w_gate, colscale]
    in_specs = [
        pl.BlockSpec((tm, D_MODEL), lambda i: (i, 0)),
        _const_spec((1, D_MODEL)),
        _layer_spec((MAIN_W, D_MODEL), layer),
        _const_spec((D_MODEL, LANE)),
        _const_spec((1, MAIN_W)),
    ]
    return args, in_specs


def _inproj_fox(x, g, w_in, layer, b_f, colscale, seq, tm=512):
    t = x.shape[0]
    args, in_specs = _inproj_common(x, g, w_in, layer, N_HEADS, colscale, tm)
    bf = jnp.pad(b_f.astype(F32).reshape(1, -1), ((0, 0), (0, LANE - N_HEADS)))
    r = np.arange(tm)
    tri = jnp.asarray(r[:, None] >= r[None, :], BF16)
    eq = np.zeros((3 * LANE, LANE), np.float32)
    ek = np.zeros((3 * LANE, LANE), np.float32)
    oq = np.zeros((1, LANE), np.float32)
    ok = np.zeros((1, LANE), np.float32)
    for h in range(N_HEADS):
        for j in range(3):
            eq[j * LANE + h, FOX_EXT_STRIDE * h + j] = 1.0
            ek[j * LANE + h, FOX_EXT_STRIDE * h + 3 + j] = 1.0
            oq[0, FOX_EXT_STRIDE * h + 3 + j] = 1.0
            ok[0, FOX_EXT_STRIDE * h + j] = 1.0
    args += [bf, tri, jnp.asarray(eq, BF16), jnp.asarray(ek, BF16), jnp.asarray(oq), jnp.asarray(ok)]
    in_specs += [_const_spec((1, LANE)), _const_spec((tm, tm)), _const_spec((3 * LANE, LANE)),
                 _const_spec((3 * LANE, LANE)), _const_spec((1, LANE)), _const_spec((1, LANE))]
    row_spec = lambda w: pl.BlockSpec((tm, w), lambda i: (i, 0))
    return pl.pallas_call(
        functools.partial(_inproj_fox_kernel, n_chunk=512, steps_per_seq=seq // tm),
        grid=(t // tm,),
        in_specs=in_specs,
        out_specs=[row_spec(MAIN_W), row_spec(LANE), row_spec(LANE)],
        out_shape=[jax.ShapeDtypeStruct((t, MAIN_W), BF16), jax.ShapeDtypeStruct((t, LANE), BF16),
                   jax.ShapeDtypeStruct((t, LANE), BF16)],
        scratch_shapes=[pltpu.VMEM((D_MODEL, MAIN_W), BF16), pltpu.VMEM((1, LANE), F32)],
        compiler_params=_cparams(("arbitrary",)),
        name="inproj_fox",
    )(*args)


def _inproj_gla(x, g, w_in, layer, w_gk, b_gk, colscale, tm=512):
    t = x.shape[0]
    args, in_specs = _inproj_common(x, g, w_in, layer, GLA_LOWRANK, colscale, tm)
    wgk = jnp.pad(w_gk, ((0, LANE - GLA_LOWRANK), (0, 0))).astype(BF16)
    r = np.arange(GLA_FAST_CHUNK)
    tri = jnp.asarray(r[:, None] >= r[None, :], BF16)
    args += [wgk, b_gk.reshape(1, -1).astype(F32), tri]
    in_specs += [_const_spec(wgk.shape), _const_spec((1, N_HEADS * HEAD_DIM)),
                 _const_spec((GLA_FAST_CHUNK, GLA_FAST_CHUNK))]
    row_spec = lambda w: pl.BlockSpec((tm, w), lambda i: (i, 0))
    return pl.pallas_call(
        functools.partial(_inproj_gla_kernel, n_chunk=512),
        grid=(t // tm,),
        in_specs=in_specs,
        out_specs=[row_spec(MAIN_W), row_spec(N_HEADS * HEAD_DIM)],
        out_shape=[jax.ShapeDtypeStruct((t, MAIN_W), BF16),
                   jax.ShapeDtypeStruct((t, N_HEADS * HEAD_DIM), F32)],
        scratch_shapes=[pltpu.VMEM((D_MODEL, MAIN_W), BF16)],
        compiler_params=_cparams(("arbitrary",)),
        name="inproj_gla",
    )(*args)


def _flash_causal(i, t, u, qs, k_block, v_block, scr):
    n = len(qs)
    g = t // u
    per = g * n
    assert per % 2 == 0
    s_slot, p_slot, (acc, m_ref, alpha_ref) = scr[0:2], scr[2:4], scr[4:]

    def lanes(x, width):
        return jnp.concatenate([x] * (width // LANE), axis=1)

    def logits(a, j, slot, r0=0):
        s_slot[slot][r0:, :] = _dot_nt(qs[a][r0:], k_block(j))

    def pv(a, j, slot, r0=0):
        acc[a, r0:, :] = (lanes(alpha_ref[slot, r0:, :], 2 * HEAD_DIM) * acc[a, r0:, :]
                          + _dot(p_slot[slot][r0:, :], v_block(jnp.maximum(j, 0))))

    def softmax(a, slot, mask, r0=0):
        s_ref = s_slot[slot]
        if mask is None:
            load = lambda: s_ref[r0:, :]
        else:
            load = lambda: jnp.where(mask, s_ref[r0:, :], NEG_BIG)
        m_old = m_ref[a, r0:, :]
        m_new = jnp.maximum(m_old, jnp.max(load(), axis=-1, keepdims=True))
        m_ref[a, r0:, :] = m_new
        alpha_ref[slot, r0:, :] = jnp.exp2(m_old - m_new)
        p_slot[slot][r0:, :] = jnp.exp2(load() - lanes(m_new, u)).astype(BF16)

    def group(j0, tri, last):
        row0 = lambda e: 0 if tri is None or e < 0 else (e // n) * u
        for e in range(per):
            slot = e & 1
            if e + 1 < per:
                logits((e + 1) % n, j0 + (e + 1) // n, 1 - slot, row0(e + 1))
            elif not last:
                logits(0, j0 + g, 1 - slot)
            if e > 0:
                pv((e - 1) % n, j0 + (e - 1) // n, 1 - slot, row0(e - 1))
            else:
                pv(n - 1, j0 - 1, 1 - slot)
            softmax(e % n, slot, None if tri is None else tri[:t - row0(e)], row0(e))

    m_ref[...] = jnp.full(m_ref.shape, NEG_BIG, F32)
    alpha_ref[1] = jnp.ones(alpha_ref.shape[1:], F32)
    acc[...] = jnp.zeros(acc.shape, F32)
    p_slot[1][...] = jnp.zeros(p_slot[1].shape, BF16)
    logits(0, 0, 0)

    def body(k, _):
        group(k * g, None, False)
        return 0

    lax.fori_loop(0, i, body, 0)
    r = lax.broadcasted_iota(jnp.int32, (t, u), 0)
    c = lax.broadcasted_iota(jnp.int32, (t, u), 1)
    group(i * g, c <= r, True)
    pv(n - 1, i * g + g - 1, (per - 1) & 1, (g - 1) * u)
    return [acc[a] for a in range(n)]


def _flash_scratch(n, t, u):
    return [pltpu.VMEM((t, u), F32), pltpu.VMEM((t, u), F32),
            pltpu.VMEM((t, u), BF16), pltpu.VMEM((t, u), BF16),
            pltpu.VMEM((n, t, 2 * HEAD_DIM), F32),
            pltpu.VMEM((n, t, LANE), F32), pltpu.VMEM((2, t, LANE), F32)]


def _ones_block(u):
    return jnp.ones((u, LANE), BF16)


def _fox_attn_kernel(q_ref, qx_ref, k_ref, kx_ref, v_ref, o_ref, *scr, t, u):
    h, i = pl.program_id(1), pl.program_id(2)
    qx = qx_ref[0]
    lane = lax.broadcasted_iota(jnp.int32, qx.shape, 1)
    qx = jnp.where(lane // FOX_EXT_STRIDE == h, qx, jnp.zeros_like(qx))
    q = jnp.concatenate([q_ref[0], qx], axis=1)
    ones = _ones_block(u)

    def k_block(j):
        r = pl.multiple_of(j * u, u)
        return jnp.concatenate([k_ref[0, pl.ds(r, u), :], kx_ref[0, pl.ds(r, u), :]], axis=1)

    def v_block(j):
        return jnp.concatenate([v_ref[0, pl.ds(pl.multiple_of(j * u, u), u), :], ones], axis=1)

    acc, = _flash_causal(i, t, u, [q], k_block, v_block, scr)
    o_ref[0] = (acc[:, :HEAD_DIM] / acc[:, HEAD_DIM:]).astype(o_ref.dtype)


def _fox_attention(p3, qx, kx, t=1024, u=256):
    batch, seq, _ = p3.shape
    return pl.pallas_call(
        functools.partial(_fox_attn_kernel, t=t, u=u),
        grid=(batch, N_HEADS, seq // t),
        in_specs=[
            pl.BlockSpec((1, t, LANE), lambda b, h, i: (b, i, 12 + h)),
            pl.BlockSpec((1, t, LANE), lambda b, h, i: (b, i, 0)),
            pl.BlockSpec((1, seq, LANE), lambda b, h, i: (b, 0, 16 + h)),
            pl.BlockSpec((1, seq, LANE), lambda b, h, i: (b, 0, 0)),
            pl.BlockSpec((1, seq, LANE), lambda b, h, i: (b, 0, 20 + h)),
        ],
        out_specs=pl.BlockSpec((1, t, LANE), lambda b, h, i: (b, i, h)),
        out_shape=jax.ShapeDtypeStruct((batch, seq, N_HEADS * HEAD_DIM), BF16),
        scratch_shapes=_flash_scratch(1, t, u),
        compiler_params=_cparams(("arbitrary", "arbitrary", "arbitrary")),
        name="fox_attn",
    )(p3, qx, p3, kx, p3)


def _diff_attn_kernel(q_ref, k_ref, v_ref, lq1_ref, lk1_ref, lq2_ref, lk2_ref, gs_ref, o_ref,
                      *scr, t, u, lam_init):
    i = pl.program_id(2)
    q = q_ref[0]
    lane = lax.broadcasted_iota(jnp.int32, q.shape, 1)
    zero = jnp.zeros_like(q)
    q1 = jnp.where(lane < DIFF_HALF, q, zero)
    q2 = jnp.where(lane >= DIFF_HALF, q, zero)
    ones = _ones_block(u)
    k_block = lambda j: k_ref[0, pl.ds(pl.multiple_of(j * u, u), u), :]
    v_block = lambda j: jnp.concatenate(
        [v_ref[0, pl.ds(pl.multiple_of(j * u, u), u), :], ones], axis=1)
    a1, a2 = _flash_causal(i, t, u, [q1, q2], k_block, v_block, scr)
    lam = (jnp.exp(jnp.sum(lq1_ref[...] * lk1_ref[...], axis=-1, keepdims=True))
           - jnp.exp(jnp.sum(lq2_ref[...] * lk2_ref[...], axis=-1, keepdims=True)) + lam_init)
    o = a1[:, :HEAD_DIM] / a1[:, HEAD_DIM:] - lam * (a2[:, :HEAD_DIM] / a2[:, HEAD_DIM:])
    o_ref[0] = (_rms(o, gs_ref[...]) * (1.0 - lam_init)).astype(o_ref.dtype)


def _diff_attention(p3, lq1, lk1, lq2, lk2, g_sub, lam_init, t=1024, u=256):
    batch, seq, _ = p3.shape
    vec = lambda a: a.reshape(1, -1).astype(F32)
    return pl.pallas_call(
        functools.partial(_diff_attn_kernel, t=t, u=u, lam_init=lam_init),
        grid=(batch, N_HEADS, seq // t),
        in_specs=[
            pl.BlockSpec((1, t, LANE), lambda b, h, i: (b, i, h)),
            pl.BlockSpec((1, seq, LANE), lambda b, h, i: (b, 0, 4 + h)),
            pl.BlockSpec((1, seq, LANE), lambda b, h, i: (b, 0, 8 + h)),
            _const_spec((1, DIFF_HALF)), _const_spec((1, DIFF_HALF)),
            _const_spec((1, DIFF_HALF)), _const_spec((1, DIFF_HALF)),
            _const_spec((1, HEAD_DIM)),
        ],
        out_specs=pl.BlockSpec((1, t, LANE), lambda b, h, i: (b, i, h)),
        out_shape=jax.ShapeDtypeStruct((batch, seq, N_HEADS * HEAD_DIM), BF16),
        scratch_shapes=_flash_scratch(2, t, u),
        compiler_params=_cparams(("arbitrary", "arbitrary", "arbitrary")),
        name="diff_attn",
    )(p3, p3, p3, vec(lq1), vec(lk1), vec(lq2), vec(lk2), vec(g_sub))


def _gla_kernel(q_ref, k_ref, v_ref, g_ref, bc_ref, on_ref, o_ref, st_ref, *, tc):
    i = pl.program_id(2)

    @pl.when(i == 0)
    def _():
        st_ref[...] = jnp.zeros_like(st_ref)

    C = GLA_FAST_CHUNK

    def finish(o, rows):
        y = _rms(o, on_ref[...])
        g = g_ref[0, rows, :].astype(F32)
        o_ref[0, rows, :] = (y * (g * jax.nn.sigmoid(g))).astype(o_ref.dtype)

    def update_state(st, k, vf, b, n):
        b_last = b[n - 1:n, :]
        k_dec = (k * jnp.exp(b_last - b)).astype(BF16)
        st_ref[...] = st * jnp.exp(b_last) + _dot(vf.T.astype(BF16), k_dec)

    safe = jnp.min(bc_ref[0]) >= -GLA_SAFE_DECAY

    @pl.when(safe)
    def _():
        r = lax.broadcasted_iota(jnp.int32, (C, C), 0)
        cc = lax.broadcasted_iota(jnp.int32, (C, C), 1)
        causal = cc <= r
        for c in range(tc // C):
            rows = slice(c * C, (c + 1) * C)
            q = q_ref[0, rows, :].astype(F32)
            k = k_ref[0, rows, :].astype(F32)
            v = v_ref[0, rows, :]
            b = bc_ref[0, rows, :]
            st = st_ref[...]
            qe = (q * jnp.exp(b)).astype(BF16)
            ke = (k * jnp.exp(-b)).astype(BF16)
            o = _dot_nt(qe, st.astype(BF16))
            a = jnp.where(causal, _dot_nt(qe, ke), 0.0)
            o = o + _dot(a.astype(BF16), v)
            update_state(st, k, v.astype(F32), b, C)
            finish(o, rows)

    @pl.when(jnp.logical_not(safe))
    def _():
        L, SB = GLA_CHUNK, GLA_SUB
        row = lax.broadcasted_iota(jnp.int32, (L, HEAD_DIM), 0)
        sub_row = lax.broadcasted_iota(jnp.int32, (SB, 1), 0)

        def chunk(c, _):
            r0 = pl.multiple_of(c * L, L)
            rows = pl.ds(r0, L)
            q = q_ref[0, rows, :].astype(F32)
            k = k_ref[0, rows, :].astype(F32)
            v = v_ref[0, rows, :]
            vf = v.astype(F32)
            before = bc_ref[0, pl.ds(jnp.maximum(r0 - 1, 0), 1), :]
            b = bc_ref[0, rows, :] - jnp.where(r0 % C == 0, 0.0, before)
            st = st_ref[...]

            o = _dot_nt((q * jnp.exp(b)).astype(BF16), st.astype(BF16))

            qs, ks = [], []
            for s in range(1, L // SB):
                ref = b[s * SB - 1:s * SB, :]
                in_sub = (row >= s * SB) & (row < (s + 1) * SB)
                qs.append(jnp.where(in_sub, q * jnp.exp(jnp.minimum(b - ref, 0.0)), 0.0))
                ks.append(jnp.where(row < s * SB, k * jnp.exp(jnp.minimum(ref - b, 0.0)), 0.0))
            a = _dot_nt(jnp.concatenate(qs, axis=1).astype(BF16),
                        jnp.concatenate(ks, axis=1).astype(BF16))
            o = o + _dot(a.astype(BF16), v)

            diag = []
            for s in range(L // SB):
                sl = slice(s * SB, (s + 1) * SB)
                qd, kd, bd, vd = q[sl], k[sl], b[sl], vf[sl]
                acc = jnp.zeros((SB, GLA_DV), F32)
                for j in range(SB):
                    w = qd * kd[j:j + 1] * jnp.exp(jnp.minimum(bd - bd[j:j + 1], 0.0))
                    d = jnp.sum(w, axis=-1, keepdims=True)
                    acc = acc + jnp.where(sub_row >= j, d, 0.0) * vd[j:j + 1]
                diag.append(acc)
            o = o + jnp.concatenate(diag, axis=0)

            update_state(st, k, vf, b, L)
            finish(o, rows)
            return 0

        lax.fori_loop(0, tc // L, chunk, 0)


def _gla(p3, bc3, g_onorm, tc=1024):
    batch, seq, _ = p3.shape
    return pl.pallas_call(
        functools.partial(_gla_kernel, tc=tc),
        grid=(batch, N_HEADS, seq // tc),
        in_specs=[
            pl.BlockSpec((1, tc, HEAD_DIM), lambda b, h, i: (b, i, h)),
            pl.BlockSpec((1, tc, HEAD_DIM), lambda b, h, i: (b, i, 4 + h)),
            pl.BlockSpec((1, tc, GLA_DV), lambda b, h, i: (b, i, 4 + h)),
            pl.BlockSpec((1, tc, GLA_DV), lambda b, h, i: (b, i, 8 + h)),
            pl.BlockSpec((1, tc, HEAD_DIM), lambda b, h, i: (b, i, h)),
            _const_spec((1, GLA_DV)),
        ],
        out_specs=pl.BlockSpec((1, tc, GLA_DV), lambda b, h, i: (b, i, h)),
        out_shape=jax.ShapeDtypeStruct((batch, seq, N_HEADS * GLA_DV), BF16),
        scratch_shapes=[pltpu.VMEM((GLA_DV, HEAD_DIM), F32)],
        compiler_params=_cparams(("arbitrary", "arbitrary", "arbitrary")),
        name="gla",
    )(p3, p3, p3, p3, bc3, g_onorm.reshape(1, -1).astype(F32))


def _tail_kernel(*refs, mix_widths, final_norm):
    n_mix = len(mix_widths)
    h_ref = refs[0]
    o_refs = refs[1:1 + n_mix]
    wo_ref, gm_ref, wu_ref, wd_ref, gp_ref, wg_ref, p_ref, wp_ref = refs[1 + n_mix:9 + n_mix]
    gf_ref = refs[9 + n_mix] if final_norm else None
    out_ref = refs[-1]

    mixed, c0 = None, 0
    for o_ref, w in zip(o_refs, mix_widths):
        term = _dot(o_ref[...], wo_ref[c0:c0 + w, :])
        mixed = term if mixed is None else mixed + term
        c0 += w
    h = h_ref[...] + mixed
    hn = _rms(h, gm_ref[...]).astype(BF16)
    acc = h
    for c0 in range(0, D_FF, FF_CHUNK):
        u = jnp.square(jnp.maximum(_dot(hn, wu_ref[:, c0:c0 + FF_CHUNK]), 0.0))
        acc = acc + _dot(u.astype(BF16), wd_ref[c0:c0 + FF_CHUNK, :])
    h = acc
    gate = jax.nn.sigmoid(_dot(_rms(h, gp_ref[...]).astype(BF16), wg_ref[...]))
    h = h + gate * _dot(p_ref[...].astype(BF16), wp_ref[...])
    if final_norm:
        h = _rms(h, gf_ref[...])
    out_ref[...] = h


def _tail(h, mix, w_o, w_o_layer, g_mlp, w_up, w_down, g_ple, w_gate, p_all, w_proj, layer, g_final,
          tm=512):
    t = h.shape[0]
    row = lambda a: a.reshape(1, -1).astype(F32)
    args = [h, *mix, w_o, row(g_mlp), w_up, w_down, row(g_ple), w_gate, p_all, w_proj]
    in_specs = [pl.BlockSpec((tm, D_MODEL), lambda i: (i, 0))]
    in_specs += [pl.BlockSpec((tm, m.shape[1]), lambda i: (i, 0)) for m in mix]
    in_specs += [_layer_spec(w_o.shape[1:], w_o_layer), _const_spec((1, D_MODEL)),
                 _layer_spec(w_up.shape[1:], layer), _layer_spec(w_down.shape[1:], layer),
                 _const_spec((1, D_MODEL)), _layer_spec(w_gate.shape[1:], layer),
                 pl.BlockSpec((None, tm, PLE_DIM), lambda i: (layer, i, 0)),
                 _layer_spec(w_proj.shape[1:], layer)]
    if g_final is not None:
        args.append(row(g_final))
        in_specs.append(_const_spec((1, D_MODEL)))
    return pl.pallas_call(
        functools.partial(_tail_kernel, mix_widths=tuple(m.shape[1] for m in mix),
                          final_norm=g_final is not None),
        grid=(t // tm,),
        in_specs=in_specs,
        out_specs=pl.BlockSpec((tm, D_MODEL), lambda i: (i, 0)),
        out_shape=jax.ShapeDtypeStruct((t, D_MODEL), F32),
        compiler_params=_cparams(("arbitrary",)),
        name="tail",
    )(*args)


def _col_scale(scales):
    cs = np.ones((1, MAIN_W), np.float32)
    for (c0, c1), v in scales.items():
        cs[0, c0:c1] = v
    return jnp.asarray(cs)


def kernel(x, p, norm_mix, norm_mlp, norm_ple, norm_final, even_w_in, even_b_f, lam_q1, lam_k1, lam_q2, lam_k2, diff_subln, even_w_o, odd_w_in, odd_w_gk, odd_b_gk, gla_onorm, odd_w_o, w_up, w_down, w_ple_gate, w_ple_proj):
    batch, seq, _ = x.shape
    depth = p.shape[0]
    t = batch * seq
    h = x.reshape(t, D_MODEL)
    p_all = p.reshape(depth, t, PLE_DIM)
    even_w_o, odd_w_o, w_up, w_down, w_ple_gate, w_ple_proj = (
        w.astype(BF16) for w in (even_w_o, odd_w_o, w_up, w_down, w_ple_gate, w_ple_proj))
    for i in range(depth):
        j = i // 2
        if i % 2 == 0:
            cs = _col_scale({(0, 512): DIFF_HALF ** -0.5 * LOG2E,
                             (1536, 2048): HEAD_DIM ** -0.5 * LOG2E})
            proj, qx, kx = _inproj_fox(h, norm_mix[i], even_w_in, j, even_b_f[j], cs, seq)
            p3 = proj.reshape(batch, seq, MAIN_W)
            lam_init = 0.8 - 0.6 * math.exp(-0.3 * i)
            oa = _diff_attention(p3, lam_q1[j], lam_k1[j], lam_q2[j], lam_k2[j],
                                 diff_subln[j], lam_init)
            ob = _fox_attention(p3, qx.reshape(batch, seq, LANE), kx.reshape(batch, seq, LANE))
            mix = [oa.reshape(t, -1), ob.reshape(t, -1)]
            w_o = even_w_o
        else:
            cs = _col_scale({(0, 512): HEAD_DIM ** -0.5})
            proj, bc = _inproj_gla(h, norm_mix[i], odd_w_in, j, odd_w_gk[j], odd_b_gk[j], cs)
            o = _gla(proj.reshape(batch, seq, MAIN_W), bc.reshape(batch, seq, -1), gla_onorm[j])
            mix = [o.reshape(t, -1)]
            w_o = odd_w_o
        h = _tail(h, mix, w_o, j, norm_mlp[i], w_up, w_down, norm_ple[i], w_ple_gate, p_all,
                  w_ple_proj, i, norm_final if i == depth - 1 else None)
    return h.reshape(batch, seq, D_MODEL)
```

```python
import functools
import math

import jax
import jax.numpy as jnp
import numpy as np
from jax import lax
from jax.experimental import pallas as pl
from jax.experimental.pallas import tpu as pltpu

F32 = jnp.float32
BF16 = jnp.bfloat16

D_MODEL = 1024
PLE_DIM = 256
N_HEADS = 4
HEAD_DIM = 128
DIFF_HALF = 64
GLA_DV = 256
GLA_LOWRANK = 16
GLA_GATE_NORM = 16.0
GLA_CHUNK = 64
GLA_SUB = 16
GLA_FAST_CHUNK = 128
GLA_SAFE_DECAY = 60.0
FOX_EXT_STRIDE = 8
D_FF = 4 * D_MODEL
FF_CHUNK = 1024
EPS = 1e-6
LOG2E = 1.4426950408889634
NEG_BIG = -1e30
LANE = 128
MAIN_W = 3072

VMEM_LIMIT = 56 * 1024 * 1024


def _cparams(sem):
    return pltpu.CompilerParams(dimension_semantics=sem, vmem_limit_bytes=VMEM_LIMIT)


def _const_spec(shape):
    nd = len(shape)
    return pl.BlockSpec(shape, lambda *_: (0,) * nd, pipeline_mode=pl.Buffered(1))


def _layer_spec(shape, layer):
    nd = len(shape)
    return pl.BlockSpec((None,) + tuple(shape), lambda *_: (layer,) + (0,) * nd,
                        pipeline_mode=pl.Buffered(1))


def _rms(x, g):
    return x * lax.rsqrt(jnp.mean(x * x, axis=-1, keepdims=True) + EPS) * g


def _log_sigmoid(x):
    return jnp.minimum(x, 0.0) - jnp.log(1.0 + jnp.exp(-jnp.abs(x)))


def _split3(x):
    hi = x.astype(BF16)
    r1 = x - hi.astype(F32)
    mid = r1.astype(BF16)
    lo = (r1 - mid.astype(F32)).astype(BF16)
    return hi, mid, lo


def _dot(a, b):
    return jnp.dot(a, b, preferred_element_type=F32)


def _cumsum_rows(x, tri):
    w = x.shape[1]
    y = _dot(tri, jnp.concatenate(_split3(x), axis=1))
    return y[:, :w] + y[:, w:2 * w] + y[:, 2 * w:]


def _dot_nt(a, b):
    return lax.dot_general(a, b, (((1,), (1,)), ((), ())), preferred_element_type=F32)


def _project(x_ref, g_ref, w_ref, wg_ref, cs_ref, o_ref, wb_scr, n_chunk, gate_steps):
    @pl.when(pl.program_id(0) == 0)
    def _():
        for c0 in range(0, MAIN_W, n_chunk):
            wb_scr[:, c0:c0 + n_chunk] = w_ref[c0:c0 + n_chunk, :].T.astype(BF16)

    hn = _rms(x_ref[...], g_ref[...]).astype(BF16)
    value = _dot(hn, wg_ref[...])
    steps = list(gate_steps)
    for c0 in range(0, MAIN_W, n_chunk):
        y = _dot(hn, wb_scr[:, c0:c0 + n_chunk]) * cs_ref[:, c0:c0 + n_chunk]
        o_ref[:, c0:c0 + n_chunk] = y.astype(o_ref.dtype)
        if steps:
            value = steps.pop(0)(value)
    for step in steps:
        value = step(value)


def _inproj_fox_kernel(x_ref, g_ref, w_ref, wg_ref, cs_ref, bf_ref, tri_ref, eq_ref, ek_ref,
                       oq_ref, ok_ref, o_ref, qx_ref, kx_ref, wb_scr, carry_ref,
                       *, n_chunk, steps_per_seq):
    @pl.when(pl.program_id(0) % steps_per_seq == 0)
    def _():
        carry_ref[...] = jnp.zeros_like(carry_ref)

    def cumulate(gate):
        logf = _log_sigmoid(gate + bf_ref[...]) * LOG2E
        c = carry_ref[...] + _cumsum_rows(logf, tri_ref[...])
        tm = c.shape[0]
        carry_ref[...] = c[tm - 1:tm, :]
        return c

    def extend(c):
        terms = jnp.concatenate(_split3(c), axis=1)
        qx_ref[...] = (_dot(terms, eq_ref[...]) + oq_ref[...]).astype(BF16)
        kx_ref[...] = (ok_ref[...] - _dot(terms, ek_ref[...])).astype(BF16)

    _project(x_ref, g_ref, w_ref, wg_ref, cs_ref, o_ref, wb_scr, n_chunk, [cumulate, extend])


def _inproj_gla_kernel(x_ref, g_ref, w_ref, wg_ref, cs_ref, wgk_ref, bgk_ref, tri_ref,
                       o_ref, bc_ref, wb_scr, *, n_chunk):
    C = GLA_FAST_CHUNK

    def decay(gl):
        x = _dot(gl.astype(BF16), wgk_ref[...]) + bgk_ref[...]
        return _log_sigmoid(x) * (1.0 / GLA_GATE_NORM)

    def cumulate(c):
        def step(gk):
            bc_ref[c * C:(c + 1) * C, :] = _cumsum_rows(gk[c * C:(c + 1) * C, :], tri_ref[...])
            return gk
        return step

    n_row_chunks = x_ref.shape[0] // C
    _project(x_ref, g_ref, w_ref, wg_ref, cs_ref, o_ref, wb_scr, n_chunk,
             [decay] + [cumulate(c) for c in range(n_row_chunks)])


def _inproj_common(x, g, w_in, layer, n_gate, colscale, tm):
    w_gate = jnp.pad(w_in[layer, :, MAIN_W:], ((0, 0), (0, LANE - n_gate))).astype(BF16)
    args = [x, g.reshape(1, -1).astype(F32), jnp.swapaxes(w_in, 1, 2), w_gate, colscale]
    in_specs = [
        pl.BlockSpec((tm, D_MODEL), lambda i: (i, 0)),
        _const_spec((1, D_MODEL)),
        _layer_spec((MAIN_W, D_MODEL), layer),
        _const_spec((D_MODEL, LANE)),
        _const_spec((1, MAIN_W)),
    ]
    return args, in_specs


def _inproj_fox(x, g, w_in, layer, b_f, colscale, seq, tm=512):
    t = x.shape[0]
    args, in_specs = _inproj_common(x, g, w_in, layer, N_HEADS, colscale, tm)
    bf = jnp.pad(b_f.astype(F32).reshape(1, -1), ((0, 0), (0, LANE - N_HEADS)))
    r = np.arange(tm)
    tri = jnp.asarray(r[:, None] >= r[None, :], BF16)
    eq = np.zeros((3 * LANE, LANE), np.float32)
    ek = np.zeros((3 * LANE, LANE), np.float32)
    oq = np.zeros((1, LANE), np.float32)
    ok = np.zeros((1, LANE), np.float32)
    for h in range(N_HEADS):
        for j in range(3):
            eq[j * LANE + h, FOX_EXT_STRIDE * h + j] = 1.0
            ek[j * LANE + h, FOX_EXT_STRIDE * h + 3 + j] = 1.0
            oq[0, FOX_EXT_STRIDE * h + 3 + j] = 1.0
            ok[0, FOX_EXT_STRIDE * h + j] = 1.0
    args += [bf, tri, jnp.asarray(eq, BF16), jnp.asarray(ek, BF16), jnp.asarray(oq), jnp.asarray(ok)]
    in_specs += [_const_spec((1, LANE)), _const_spec((tm, tm)), _const_spec((3 * LANE, LANE)),
                 _const_spec((3 * LANE, LANE)), _const_spec((1, LANE)), _const_spec((1, LANE))]
    row_spec = lambda w: pl.BlockSpec((tm, w), lambda i: (i, 0))
    return pl.pallas_call(
        functools.partial(_inproj_fox_kernel, n_chunk=512, steps_per_seq=seq // tm),
        grid=(t // tm,),
        in_specs=in_specs,
        out_specs=[row_spec(MAIN_W), row_spec(LANE), row_spec(LANE)],
        out_shape=[jax.ShapeDtypeStruct((t, MAIN_W), BF16), jax.ShapeDtypeStruct((t, LANE), BF16),
                   jax.ShapeDtypeStruct((t, LANE), BF16)],
        scratch_shapes=[pltpu.VMEM((D_MODEL, MAIN_W), BF16), pltpu.VMEM((1, LANE), F32)],
        compiler_params=_cparams(("arbitrary",)),
        name="inproj_fox",
    )(*args)


def _inproj_gla(x, g, w_in, layer, w_gk, b_gk, colscale, tm=512):
    t = x.shape[0]
    args, in_specs = _inproj_common(x, g, w_in, layer, GLA_LOWRANK, colscale, tm)
    wgk = jnp.pad(w_gk, ((0, LANE - GLA_LOWRANK), (0, 0))).astype(BF16)
    r = np.arange(GLA_FAST_CHUNK)
    tri = jnp.asarray(r[:, None] >= r[None, :], BF16)
    args += [wgk, b_gk.reshape(1, -1).astype(F32), tri]
    in_specs += [_const_spec(wgk.shape), _const_spec((1, N_HEADS * HEAD_DIM)),
                 _const_spec((GLA_FAST_CHUNK, GLA_FAST_CHUNK))]
    row_spec = lambda w: pl.BlockSpec((tm, w), lambda i: (i, 0))
    return pl.pallas_call(
        functools.partial(_inproj_gla_kernel, n_chunk=512),
        grid=(t // tm,),
        in_specs=in_specs,
        out_specs=[row_spec(MAIN_W), row_spec(N_HEADS * HEAD_DIM)],
        out_shape=[jax.ShapeDtypeStruct((t, MAIN_W), BF16),
                   jax.ShapeDtypeStruct((t, N_HEADS * HEAD_DIM), F32)],
        scratch_shapes=[pltpu.VMEM((D_MODEL, MAIN_W), BF16)],
        compiler_params=_cparams(("arbitrary",)),
        name="inproj_gla",
    )(*args)


def _flash_causal(i, t, u, qs, k_block, v_block, scr):
    n = len(qs)
    g = t // u
    per = g * n
    assert per % 2 == 0
    s_slot, p_slot, (acc, m_ref, alpha_ref) = scr[0:2], scr[2:4], scr[4:]

    def lanes(x, width):
        return jnp.concatenate([x] * (width // LANE), axis=1)

    def logits(a, j, slot, r0=0):
        s_slot[slot][r0:, :] = _dot_nt(qs[a][r0:], k_block(j))

    def pv(a, j, slot, r0=0):
        acc[a, r0:, :] = (lanes(alpha_ref[slot, r0:, :], 2 * HEAD_DIM) * acc[a, r0:, :]
                          + _dot(p_slot[slot][r0:, :], v_block(jnp.maximum(j, 0))))

    def softmax(a, slot, mask, r0=0):
        s_ref = s_slot[slot]
        if mask is None:
            load = lambda: s_ref[r0:, :]
        else:
            load = lambda: jnp.where(mask, s_ref[r0:, :], NEG_BIG)
        m_old = m_ref[a, r0:, :]
        m_new = jnp.maximum(m_old, jnp.max(load(), axis=-1, keepdims=True))
        m_ref[a, r0:, :] = m_new
        alpha_ref[slot, r0:, :] = jnp.exp2(m_old - m_new)
        p_slot[slot][r0:, :] = jnp.exp2(load() - lanes(m_new, u)).astype(BF16)

    def group(j0, tri, last):
        row0 = lambda e: 0 if tri is None or e < 0 else (e // n) * u
        for e in range(per):
            slot = e & 1
            if e + 1 < per:
                logits((e + 1) % n, j0 + (e + 1) // n, 1 - slot, row0(e + 1))
            elif not last:
                logits(0, j0 + g, 1 - slot)
            if e > 0:
                pv((e - 1) % n, j0 + (e - 1) // n, 1 - slot, row0(e - 1))
            else:
                pv(n - 1, j0 - 1, 1 - slot)
            softmax(e % n, slot, None if tri is None else tri[:t - row0(e)], row0(e))

    m_ref[...] = jnp.full(m_ref.shape, NEG_BIG, F32)
    alpha_ref[1] = jnp.ones(alpha_ref.shape[1:], F32)
    acc[...] = jnp.zeros(acc.shape, F32)
    p_slot[1][...] = jnp.zeros(p_slot[1].shape, BF16)
    logits(0, 0, 0)

    def body(k, _):
        group(k * g, None, False)
        return 0

    lax.fori_loop(0, i, body, 0)
    r = lax.broadcasted_iota(jnp.int32, (t, u), 0)
    c = lax.broadcasted_iota(jnp.int32, (t, u), 1)
    group(i * g, c <= r, True)
    pv(n - 1, i * g + g - 1, (per - 1) & 1, (g - 1) * u)
    return [acc[a] for a in range(n)]


def _flash_scratch(n, t, u):
    return [pltpu.VMEM((t, u), F32), pltpu.VMEM((t, u), F32),
            pltpu.VMEM((t, u), BF16), pltpu.VMEM((t, u), BF16),
            pltpu.VMEM((n, t, 2 * HEAD_DIM), F32),
            pltpu.VMEM((n, t, LANE), F32), pltpu.VMEM((2, t, LANE), F32)]


def _ones_block(u):
    return jnp.ones((u, LANE), BF16)


def _fox_attn_kernel(q_ref, qx_ref, k_ref, kx_ref, v_ref, o_ref, *scr, t, u):
    h, i = pl.program_id(1), pl.program_id(2)
    qx = qx_ref[0]
    lane = lax.broadcasted_iota(jnp.int32, qx.shape, 1)
    qx = jnp.where(lane // FOX_EXT_STRIDE == h, qx, jnp.zeros_like(qx))
    q = jnp.concatenate([q_ref[0], qx], axis=1)
    ones = _ones_block(u)

    def k_block(j):
        r = pl.multiple_of(j * u, u)
        return jnp.concatenate([k_ref[0, pl.ds(r, u), :], kx_ref[0, pl.ds(r, u), :]], axis=1)

    def v_block(j):
        return jnp.concatenate([v_ref[0, pl.ds(pl.multiple_of(j * u, u), u), :], ones], axis=1)

    acc, = _flash_causal(i, t, u, [q], k_block, v_block, scr)
    o_ref[0] = (acc[:, :HEAD_DIM] / acc[:, HEAD_DIM:]).astype(o_ref.dtype)


def _fox_attention(p3, qx, kx, t=2048, u=256):
    batch, seq, _ = p3.shape
    return pl.pallas_call(
        functools.partial(_fox_attn_kernel, t=t, u=u),
        grid=(batch, N_HEADS, seq // t),
        in_specs=[
            pl.BlockSpec((1, t, LANE), lambda b, h, i: (b, i, 12 + h)),
            pl.BlockSpec((1, t, LANE), lambda b, h, i: (b, i, 0)),
            pl.BlockSpec((1, seq, LANE), lambda b, h, i: (b, 0, 16 + h)),
            pl.BlockSpec((1, seq, LANE), lambda b, h, i: (b, 0, 0)),
            pl.BlockSpec((1, seq, LANE), lambda b, h, i: (b, 0, 20 + h)),
        ],
        out_specs=pl.BlockSpec((1, t, LANE), lambda b, h, i: (b, i, h)),
        out_shape=jax.ShapeDtypeStruct((batch, seq, N_HEADS * HEAD_DIM), BF16),
        scratch_shapes=_flash_scratch(1, t, u),
        compiler_params=_cparams(("arbitrary", "arbitrary", "arbitrary")),
        name="fox_attn",
    )(p3, qx, p3, kx, p3)


def _diff_attn_kernel(q_ref, k_ref, v_ref, lq1_ref, lk1_ref, lq2_ref, lk2_ref, gs_ref, o_ref,
                      *scr, t, u, lam_init):
    i = pl.program_id(2)
    q = q_ref[0]
    lane = lax.broadcasted_iota(jnp.int32, q.shape, 1)
    zero = jnp.zeros_like(q)
    q1 = jnp.where(lane < DIFF_HALF, q, zero)
    q2 = jnp.where(lane >= DIFF_HALF, q, zero)
    ones = _ones_block(u)
    k_block = lambda j: k_ref[0, pl.ds(pl.multiple_of(j * u, u), u), :]
    v_block = lambda j: jnp.concatenate(
        [v_ref[0, pl.ds(pl.multiple_of(j * u, u), u), :], ones], axis=1)
    a1, a2 = _flash_causal(i, t, u, [q1, q2], k_block, v_block, scr)
    lam = (jnp.exp(jnp.sum(lq1_ref[...] * lk1_ref[...], axis=-1, keepdims=True))
           - jnp.exp(jnp.sum(lq2_ref[...] * lk2_ref[...], axis=-1, keepdims=True)) + lam_init)
    o = a1[:, :HEAD_DIM] / a1[:, HEAD_DIM:] - lam * (a2[:, :HEAD_DIM] / a2[:, HEAD_DIM:])
    o_ref[0] = (_rms(o, gs_ref[...]) * (1.0 - lam_init)).astype(o_ref.dtype)


def _diff_attention(p3, lq1, lk1, lq2, lk2, g_sub, lam_init, t=2048, u=256):
    batch, seq, _ = p3.shape
    vec = lambda a: a.reshape(1, -1).astype(F32)
    return pl.pallas_call(
        functools.partial(_diff_attn_kernel, t=t, u=u, lam_init=lam_init),
        grid=(batch, N_HEADS, seq // t),
        in_specs=[
            pl.BlockSpec((1, t, LANE), lambda b, h, i: (b, i, h)),
            pl.BlockSpec((1, seq, LANE), lambda b, h, i: (b, 0, 4 + h)),
            pl.BlockSpec((1, seq, LANE), lambda b, h, i: (b, 0, 8 + h)),
            _const_spec((1, DIFF_HALF)), _const_spec((1, DIFF_HALF)),
            _const_spec((1, DIFF_HALF)), _const_spec((1, DIFF_HALF)),
            _const_spec((1, HEAD_DIM)),
        ],
        out_specs=pl.BlockSpec((1, t, LANE), lambda b, h, i: (b, i, h)),
        out_shape=jax.ShapeDtypeStruct((batch, seq, N_HEADS * HEAD_DIM), BF16),
        scratch_shapes=_flash_scratch(2, t, u),
        compiler_params=_cparams(("arbitrary", "arbitrary", "arbitrary")),
        name="diff_attn",
    )(p3, p3, p3, vec(lq1), vec(lk1), vec(lq2), vec(lk2), vec(g_sub))


def _gla_kernel(q_ref, k_ref, v_ref, g_ref, bc_ref, on_ref, o_ref, st_ref, *, tc):
    i = pl.program_id(2)

    @pl.when(i == 0)
    def _():
        st_ref[...] = jnp.zeros_like(st_ref)

    C = GLA_FAST_CHUNK

    def finish(o, rows):
        y = _rms(o, on_ref[...])
        g = g_ref[0, rows, :].astype(F32)
        o_ref[0, rows, :] = (y * (g * jax.nn.sigmoid(g))).astype(o_ref.dtype)

    def update_state(st, k, v, b, n):
        b_last = b[n - 1:n, :]
        k_dec = (k * jnp.exp(b_last - b)).astype(BF16)
        vt_k = lax.dot_general(v, k_dec, (((0,), (0,)), ((), ())), preferred_element_type=F32)
        st_ref[...] = st * jnp.exp(b_last) + vt_k

    safe = jnp.min(bc_ref[0]) >= -GLA_SAFE_DECAY

    @pl.when(safe)
    def _():
        r = lax.broadcasted_iota(jnp.int32, (C, C), 0)
        cc = lax.broadcasted_iota(jnp.int32, (C, C), 1)
        causal = cc <= r
        for c in range(tc // C):
            rows = slice(c * C, (c + 1) * C)
            q = q_ref[0, rows, :].astype(F32)
            k = k_ref[0, rows, :].astype(F32)
            v = v_ref[0, rows, :]
            b = bc_ref[0, rows, :]
            st = st_ref[...]
            qe = (q * jnp.exp(b)).astype(BF16)
            ke = (k * jnp.exp(-b)).astype(BF16)
            o = _dot_nt(qe, st.astype(BF16))
            a = jnp.where(causal, _dot_nt(qe, ke), 0.0)
            o = o + _dot(a.astype(BF16), v)
            update_state(st, k, v, b, C)
            finish(o, rows)

    @pl.when(jnp.logical_not(safe))
    def _():
        L, SB = GLA_CHUNK, GLA_SUB
        row = lax.broadcasted_iota(jnp.int32, (L, HEAD_DIM), 0)
        sub_row = lax.broadcasted_iota(jnp.int32, (SB, 1), 0)

        def chunk(c, _):
            r0 = pl.multiple_of(c * L, L)
            rows = pl.ds(r0, L)
            q = q_ref[0, rows, :].astype(F32)
            k = k_ref[0, rows, :].astype(F32)
            v = v_ref[0, rows, :]
            vf = v.astype(F32)
            before = bc_ref[0, pl.ds(jnp.maximum(r0 - 1, 0), 1), :]
            b = bc_ref[0, rows, :] - jnp.where(r0 % C == 0, 0.0, before)
            st = st_ref[...]

            o = _dot_nt((q * jnp.exp(b)).astype(BF16), st.astype(BF16))

            qs, ks = [], []
            for s in range(1, L // SB):
                ref = b[s * SB - 1:s * SB, :]
                in_sub = (row >= s * SB) & (row < (s + 1) * SB)
                qs.append(jnp.where(in_sub, q * jnp.exp(jnp.minimum(b - ref, 0.0)), 0.0))
                ks.append(jnp.where(row < s * SB, k * jnp.exp(jnp.minimum(ref - b, 0.0)), 0.0))
            a = _dot_nt(jnp.concatenate(qs, axis=1).astype(BF16),
                        jnp.concatenate(ks, axis=1).astype(BF16))
            o = o + _dot(a.astype(BF16), v)

            diag = []
            for s in range(L // SB):
                sl = slice(s * SB, (s + 1) * SB)
                qd, kd, bd, vd = q[sl], k[sl], b[sl], vf[sl]
                acc = jnp.zeros((SB, GLA_DV), F32)
                for j in range(SB):
                    w = qd * kd[j:j + 1] * jnp.exp(jnp.minimum(bd - bd[j:j + 1], 0.0))
                    d = jnp.sum(w, axis=-1, keepdims=True)
                    acc = acc + jnp.where(sub_row >= j, d, 0.0) * vd[j:j + 1]
                diag.append(acc)
            o = o + jnp.concatenate(diag, axis=0)

            update_state(st, k, v, b, L)
            finish(o, rows)
            return 0

        lax.fori_loop(0, tc // L, chunk, 0)


def _gla(p3, bc3, g_onorm, tc=1024):
    batch, seq, _ = p3.shape
    return pl.pallas_call(
        functools.partial(_gla_kernel, tc=tc),
        grid=(batch, N_HEADS, seq // tc),
        in_specs=[
            pl.BlockSpec((1, tc, HEAD_DIM), lambda b, h, i: (b, i, h)),
            pl.BlockSpec((1, tc, HEAD_DIM), lambda b, h, i: (b, i, 4 + h)),
            pl.BlockSpec((1, tc, GLA_DV), lambda b, h, i: (b, i, 4 + h)),
            pl.BlockSpec((1, tc, GLA_DV), lambda b, h, i: (b, i, 8 + h)),
            pl.BlockSpec((1, tc, HEAD_DIM), lambda b, h, i: (b, i, h)),
            _const_spec((1, GLA_DV)),
        ],
        out_specs=pl.BlockSpec((1, tc, GLA_DV), lambda b, h, i: (b, i, h)),
        out_shape=jax.ShapeDtypeStruct((batch, seq, N_HEADS * GLA_DV), BF16),
        scratch_shapes=[pltpu.VMEM((GLA_DV, HEAD_DIM), F32)],
        compiler_params=_cparams(("arbitrary", "arbitrary", "arbitrary")),
        name="gla",
    )(p3, p3, p3, p3, bc3, g_onorm.reshape(1, -1).astype(F32))


def _tail_kernel(*refs, mix_widths, final_norm):
    n_mix = len(mix_widths)
    h_ref = refs[0]
    o_refs = refs[1:1 + n_mix]
    wo_ref, gm_ref, wu_ref, wd_ref, gp_ref, wg_ref, p_ref, wp_ref = refs[1 + n_mix:9 + n_mix]
    gf_ref = refs[9 + n_mix] if final_norm else None
    out_ref = refs[-1]

    mixed, c0 = None, 0
    for o_ref, w in zip(o_refs, mix_widths):
        term = _dot(o_ref[...], wo_ref[c0:c0 + w, :])
        mixed = term if mixed is None else mixed + term
        c0 += w
    h = h_ref[...] + mixed
    hn = _rms(h, gm_ref[...]).astype(BF16)
    acc = h
    for c0 in range(0, D_FF, FF_CHUNK):
        u = jnp.square(jnp.maximum(_dot(hn, wu_ref[:, c0:c0 + FF_CHUNK]), 0.0))
        acc = acc + _dot(u.astype(BF16), wd_ref[c0:c0 + FF_CHUNK, :])
    h = acc
    gate = jax.nn.sigmoid(_dot(_rms(h, gp_ref[...]).astype(BF16), wg_ref[...]))
    h = h + gate * _dot(p_ref[...].astype(BF16), wp_ref[...])
    if final_norm:
        h = _rms(h, gf_ref[...])
    out_ref[...] = h


def _tail(h, mix, w_o, w_o_layer, g_mlp, w_up, w_down, g_ple, w_gate, p_all, w_proj, layer, g_final,
          tm=512):
    t = h.shape[0]
    row = lambda a: a.reshape(1, -1).astype(F32)
    args = [h, *mix, w_o, row(g_mlp), w_up, w_down, row(g_ple), w_gate, p_all, w_proj]
    in_specs = [pl.BlockSpec((tm, D_MODEL), lambda i: (i, 0))]
    in_specs += [pl.BlockSpec((tm, m.shape[1]), lambda i: (i, 0)) for m in mix]
    in_specs += [_layer_spec(w_o.shape[1:], w_o_layer), _const_spec((1, D_MODEL)),
                 _layer_spec(w_up.shape[1:], layer), _layer_spec(w_down.shape[1:], layer),
                 _const_spec((1, D_MODEL)), _layer_spec(w_gate.shape[1:], layer),
                 pl.BlockSpec((None, tm, PLE_DIM), lambda i: (layer, i, 0)),
                 _layer_spec(w_proj.shape[1:], layer)]
    if g_final is not None:
        args.append(row(g_final))
        in_specs.append(_const_spec((1, D_MODEL)))
    return pl.pallas_call(
        functools.partial(_tail_kernel, mix_widths=tuple(m.shape[1] for m in mix),
                          final_norm=g_final is not None),
        grid=(t // tm,),
        in_specs=in_specs,
        out_specs=pl.BlockSpec((tm, D_MODEL), lambda i: (i, 0)),
        out_shape=jax.ShapeDtypeStruct((t, D_MODEL), F32),
        compiler_params=_cparams(("arbitrary",)),
        name="tail",
    )(*args)


def _col_scale(scales):
    cs = np.ones((1, MAIN_W), np.float32)
    for (c0, c1), v in scales.items():
        cs[0, c0:c1] = v
    return jnp.asarray(cs)


def kernel(x, p, norm_mix, norm_mlp, norm_ple, norm_final, even_w_in, even_b_f, lam_q1, lam_k1, lam_q2, lam_k2, diff_subln, even_w_o, odd_w_in, odd_w_gk, odd_b_gk, gla_onorm, odd_w_o, w_up, w_down, w_ple_gate, w_ple_proj):
    batch, seq, _ = x.shape
    depth = p.shape[0]
    t = batch * seq
    h = x.reshape(t, D_MODEL)
    p_all = p.reshape(depth, t, PLE_DIM)
    even_w_o, odd_w_o, w_up, w_down, w_ple_gate, w_ple_proj = (
        w.astype(BF16) for w in (even_w_o, odd_w_o, w_up, w_down, w_ple_gate, w_ple_proj))
    for i in range(depth):
        j = i // 2
        if i % 2 == 0:
            cs = _col_scale({(0, 512): DIFF_HALF ** -0.5 * LOG2E,
                             (1536, 2048): HEAD_DIM ** -0.5 * LOG2E})
            proj, qx, kx = _inproj_fox(h, norm_mix[i], even_w_in, j, even_b_f[j], cs, seq)
            p3 = proj.reshape(batch, seq, MAIN_W)
            lam_init = 0.8 - 0.6 * math.exp(-0.3 * i)
            oa = _diff_attention(p3, lam_q1[j], lam_k1[j], lam_q2[j], lam_k2[j],
                                 diff_subln[j], lam_init)
            ob = _fox_attention(p3, qx.reshape(batch, seq, LANE), kx.reshape(batch, seq, LANE))
            mix = [oa.reshape(t, -1), ob.reshape(t, -1)]
            w_o = even_w_o
        else:
            cs = _col_scale({(0, 512): HEAD_DIM ** -0.5})
            proj, bc = _inproj_gla(h, norm_mix[i], odd_w_in, j, odd_w_gk[j], odd_b_gk[j], cs)
            o = _gla(proj.reshape(batch, seq, MAIN_W), bc.reshape(batch, seq, -1), gla_onorm[j])
            mix = [o.reshape(t, -1)]
            w_o = odd_w_o
        h = _tail(h, mix, w_o, j, norm_mlp[i], w_up, w_down, norm_ple[i], w_ple_gate, p_all,
                  w_ple_proj, i, norm_final if i == depth - 1 else None)
    return h.reshape(batch, seq, D_MODEL)
```

```python
import functools
import math

import jax
import jax.numpy as jnp
import numpy as np
from jax import lax
from jax.experimental import pallas as pl
from jax.experimental.pallas import tpu as pltpu

F32 = jnp.float32
BF16 = jnp.bfloat16

D_MODEL = 1024
PLE_DIM = 256
N_HEADS = 4
HEAD_DIM = 128
DIFF_HALF = 64
GLA_DV = 256
GLA_LOWRANK = 16
GLA_GATE_NORM = 16.0
GLA_CHUNK = 64
GLA_SUB = 16
GLA_FAST_CHUNK = 128
GLA_SAFE_DECAY = 60.0
FOX_EXT_STRIDE = 8
D_FF = 4 * D_MODEL
FF_CHUNK = 1024
EPS = 1e-6
LOG2E = 1.4426950408889634
NEG_BIG = -1e30
LANE = 128
MAIN_W = 3072

VMEM_LIMIT = 56 * 1024 * 1024


def _cparams(sem):
    return pltpu.CompilerParams(dimension_semantics=sem, vmem_limit_bytes=VMEM_LIMIT)


def _const_spec(shape):
    nd = len(shape)
    return pl.BlockSpec(shape, lambda *_: (0,) * nd, pipeline_mode=pl.Buffered(1))


def _layer_spec(shape, layer):
    nd = len(shape)
    return pl.BlockSpec((None,) + tuple(shape), lambda *_: (layer,) + (0,) * nd,
                        pipeline_mode=pl.Buffered(1))


def _rms(x, g):
    return x * lax.rsqrt(jnp.mean(x * x, axis=-1, keepdims=True) + EPS) * g


def _rms_factors(x, g):
    scale = lax.rsqrt(jnp.mean(x * x, axis=-1, keepdims=True) + EPS)
    return (x * g).astype(BF16), jnp.broadcast_to(scale, (x.shape[0], LANE))


def _log_sigmoid(x):
    return jnp.minimum(x, 0.0) - jnp.log(1.0 + jnp.exp(-jnp.abs(x)))


def _split3(x):
    hi = x.astype(BF16)
    r1 = x - hi.astype(F32)
    mid = r1.astype(BF16)
    lo = (r1 - mid.astype(F32)).astype(BF16)
    return hi, mid, lo


def _dot(a, b):
    return jnp.dot(a, b, preferred_element_type=F32)


def _cumsum_rows(x, tri):
    w = x.shape[1]
    y = _dot(tri, jnp.concatenate(_split3(x), axis=1))
    return y[:, :w] + y[:, w:2 * w] + y[:, 2 * w:]


def _dot_nt(a, b):
    return lax.dot_general(a, b, (((1,), (1,)), ((), ())), preferred_element_type=F32)


def _project(x_ref, g_ref, w_ref, wg_ref, cs_ref, o_ref, wb_scr, n_chunk, gate_steps):
    @pl.when(pl.program_id(0) == 0)
    def _():
        for c0 in range(0, MAIN_W, n_chunk):
            wb_scr[:, c0:c0 + n_chunk] = w_ref[c0:c0 + n_chunk, :].T.astype(BF16)

    xg, scale = _rms_factors(x_ref[...], g_ref[...])
    value = _dot(xg, wg_ref[...]) * scale
    scale_n = jnp.concatenate([scale] * (n_chunk // LANE), axis=1)
    steps = list(gate_steps)
    for c0 in range(0, MAIN_W, n_chunk):
        y = _dot(xg, wb_scr[:, c0:c0 + n_chunk]) * cs_ref[:, c0:c0 + n_chunk] * scale_n
        o_ref[:, c0:c0 + n_chunk] = y.astype(o_ref.dtype)
        if steps:
            value = steps.pop(0)(value)
    for step in steps:
        value = step(value)


def _inproj_fox_kernel(x_ref, g_ref, w_ref, wg_ref, cs_ref, bf_ref, tri_ref, eq_ref, ek_ref,
                       oq_ref, ok_ref, o_ref, qx_ref, kx_ref, wb_scr, carry_ref,
                       *, n_chunk, steps_per_seq):
    @pl.when(pl.program_id(0) % steps_per_seq == 0)
    def _():
        carry_ref[...] = jnp.zeros_like(carry_ref)

    def cumulate(gate):
        logf = _log_sigmoid(gate + bf_ref[...]) * LOG2E
        c = carry_ref[...] + _cumsum_rows(logf, tri_ref[...])
        tm = c.shape[0]
        carry_ref[...] = c[tm - 1:tm, :]
        return c

    def extend(c):
        terms = jnp.concatenate(_split3(c), axis=1)
        qx_ref[...] = (_dot(terms, eq_ref[...]) + oq_ref[...]).astype(BF16)
        kx_ref[...] = (ok_ref[...] - _dot(terms, ek_ref[...])).astype(BF16)

    _project(x_ref, g_ref, w_ref, wg_ref, cs_ref, o_ref, wb_scr, n_chunk, [cumulate, extend])


def _inproj_gla_kernel(x_ref, g_ref, w_ref, wg_ref, cs_ref, wgk_ref, bgk_ref, tri_ref,
                       o_ref, bc_ref, wb_scr, *, n_chunk):
    C = GLA_FAST_CHUNK

    def decay(gl):
        x = _dot(gl.astype(BF16), wgk_ref[...]) + bgk_ref[...]
        return _log_sigmoid(x) * (1.0 / GLA_GATE_NORM)

    def cumulate(c):
        def step(gk):
            bc_ref[c * C:(c + 1) * C, :] = _cumsum_rows(gk[c * C:(c + 1) * C, :], tri_ref[...])
            return gk
        return step

    n_row_chunks = x_ref.shape[0] // C
    _project(x_ref, g_ref, w_ref, wg_ref, cs_ref, o_ref, wb_scr, n_chunk,
             [decay] + [cumulate(c) for c in range(n_row_chunks)])


def _inproj_common(x, g, w_in, layer, n_gate, colscale, tm):
    w_gate = jnp.pad(w_in[layer, :, MAIN_W:], ((0, 0), (0, LANE - n_gate))).astype(BF16)
    args = [x, g.reshape(1, -1).astype(F32), jnp.swapaxes(w_in, 1, 2), w_gate, colscale]
    in_specs = [
        pl.BlockSpec((tm, D_MODEL), lambda i: (i, 0)),
        _const_spec((1, D_MODEL)),
        _layer_spec((MAIN_W, D_MODEL), layer),
        _const_spec((D_MODEL, LANE)),
        _const_spec((1, MAIN_W)),
    ]
    return args, in_specs


def _inproj_fox(x, g, w_in, layer, b_f, colscale, seq, tm=512):
    t = x.shape[0]
    args, in_specs = _inproj_common(x, g, w_in, layer, N_HEADS, colscale, tm)
    bf = jnp.pad(b_f.astype(F32).reshape(1, -1), ((0, 0), (0, LANE - N_HEADS)))
    r = np.arange(tm)
    tri = jnp.asarray(r[:, None] >= r[None, :], BF16)
    eq = np.zeros((3 * LANE, LANE), np.float32)
    ek = np.zeros((3 * LANE, LANE), np.float32)
    oq = np.zeros((1, LANE), np.float32)
    ok = np.zeros((1, LANE), np.float32)
    for h in range(N_HEADS):
        for j in range(3):
            eq[j * LANE + h, FOX_EXT_STRIDE * h + j] = 1.0
            ek[j * LANE + h, FOX_EXT_STRIDE * h + 3 + j] = 1.0
            oq[0, FOX_EXT_STRIDE * h + 3 + j] = 1.0
            ok[0, FOX_EXT_STRIDE * h + j] = 1.0
    args += [bf, tri, jnp.asarray(eq, BF16), jnp.asarray(ek, BF16), jnp.asarray(oq), jnp.asarray(ok)]
    in_specs += [_const_spec((1, LANE)), _const_spec((tm, tm)), _const_spec((3 * LANE, LANE)),
                 _const_spec((3 * LANE, LANE)), _const_spec((1, LANE)), _const_spec((1, LANE))]
    row_spec = lambda w: pl.BlockSpec((tm, w), lambda i: (i, 0))
    return pl.pallas_call(
        functools.partial(_inproj_fox_kernel, n_chunk=512, steps_per_seq=seq // tm),
        grid=(t // tm,),
        in_specs=in_specs,
        out_specs=[row_spec(MAIN_W), row_spec(LANE), row_spec(LANE)],
        out_shape=[jax.ShapeDtypeStruct((t, MAIN_W), BF16), jax.ShapeDtypeStruct((t, LANE), BF16),
                   jax.ShapeDtypeStruct((t, LANE), BF16)],
        scratch_shapes=[pltpu.VMEM((D_MODEL, MAIN_W), BF16), pltpu.VMEM((1, LANE), F32)],
        compiler_params=_cparams(("arbitrary",)),
        name="inproj_fox",
    )(*args)


def _inproj_gla(x, g, w_in, layer, w_gk, b_gk, colscale, tm=512):
    t = x.shape[0]
    args, in_specs = _inproj_common(x, g, w_in, layer, GLA_LOWRANK, colscale, tm)
    wgk = jnp.pad(w_gk, ((0, LANE - GLA_LOWRANK), (0, 0))).astype(BF16)
    r = np.arange(GLA_FAST_CHUNK)
    tri = jnp.asarray(r[:, None] >= r[None, :], BF16)
    args += [wgk, b_gk.reshape(1, -1).astype(F32), tri]
    in_specs += [_const_spec(wgk.shape), _const_spec((1, N_HEADS * HEAD_DIM)),
                 _const_spec((GLA_FAST_CHUNK, GLA_FAST_CHUNK))]
    row_spec = lambda w: pl.BlockSpec((tm, w), lambda i: (i, 0))
    return pl.pallas_call(
        functools.partial(_inproj_gla_kernel, n_chunk=512),
        grid=(t // tm,),
        in_specs=in_specs,
        out_specs=[row_spec(MAIN_W), row_spec(N_HEADS * HEAD_DIM)],
        out_shape=[jax.ShapeDtypeStruct((t, MAIN_W), BF16),
                   jax.ShapeDtypeStruct((t, N_HEADS * HEAD_DIM), F32)],
        scratch_shapes=[pltpu.VMEM((D_MODEL, MAIN_W), BF16)],
        compiler_params=_cparams(("arbitrary",)),
        name="inproj_gla",
    )(*args)


def _flash_causal(i, t, u, qs, k_block, v_block, scr):
    n = len(qs)
    g = t // u
    per = g * n
    assert per % 2 == 0
    s_slot, p_slot, (acc, m_ref, alpha_ref) = scr[0:2], scr[2:4], scr[4:]

    def lanes(x, width):
        return jnp.concatenate([x] * (width // LANE), axis=1)

    def logits(a, j, slot, r0=0):
        s_slot[slot][r0:, :] = _dot_nt(qs[a][r0:], k_block(j))

    def pv(a, j, slot, r0=0):
        acc[a, r0:, :] = (lanes(alpha_ref[slot, r0:, :], 2 * HEAD_DIM) * acc[a, r0:, :]
                          + _dot(p_slot[slot][r0:, :], v_block(jnp.maximum(j, 0))))

    def softmax(a, slot, mask, r0=0):
        s_ref = s_slot[slot]
        if mask is None:
            load = lambda: s_ref[r0:, :]
        else:
            load = lambda: jnp.where(mask, s_ref[r0:, :], NEG_BIG)
        m_old = m_ref[a, r0:, :]
        m_new = jnp.maximum(m_old, jnp.max(load(), axis=-1, keepdims=True))
        m_ref[a, r0:, :] = m_new
        alpha_ref[slot, r0:, :] = jnp.exp2(m_old - m_new)
        p_slot[slot][r0:, :] = jnp.exp2(load() - lanes(m_new, u)).astype(BF16)

    def group(j0, tri, last):
        row0 = lambda e: 0 if tri is None or e < 0 else (e // n) * u
        for e in range(per):
            slot = e & 1
            if e + 1 < per:
                logits((e + 1) % n, j0 + (e + 1) // n, 1 - slot, row0(e + 1))
            elif not last:
                logits(0, j0 + g, 1 - slot)
            if e > 0:
                pv((e - 1) % n, j0 + (e - 1) // n, 1 - slot, row0(e - 1))
            else:
                pv(n - 1, j0 - 1, 1 - slot)
            softmax(e % n, slot, None if tri is None else tri[:t - row0(e)], row0(e))

    m_ref[...] = jnp.full(m_ref.shape, NEG_BIG, F32)
    alpha_ref[1] = jnp.ones(alpha_ref.shape[1:], F32)
    acc[...] = jnp.zeros(acc.shape, F32)
    p_slot[1][...] = jnp.zeros(p_slot[1].shape, BF16)
    logits(0, 0, 0)

    def body(k, _):
        group(k * g, None, False)
        return 0

    lax.fori_loop(0, i, body, 0)
    r = lax.broadcasted_iota(jnp.int32, (t, u), 0)
    c = lax.broadcasted_iota(jnp.int32, (t, u), 1)
    group(i * g, c <= r, True)
    pv(n - 1, i * g + g - 1, (per - 1) & 1, (g - 1) * u)
    return [acc[a] for a in range(n)]


def _flash_scratch(n, t, u):
    return [pltpu.VMEM((t, u), F32), pltpu.VMEM((t, u), F32),
            pltpu.VMEM((t, u), BF16), pltpu.VMEM((t, u), BF16),
            pltpu.VMEM((n, t, 2 * HEAD_DIM), F32),
            pltpu.VMEM((n, t, LANE), F32), pltpu.VMEM((2, t, LANE), F32)]


def _ones_block(u):
    return jnp.ones((u, LANE), BF16)


def _fox_attn_kernel(q_ref, qx_ref, k_ref, kx_ref, v_ref, o_ref, *scr, t, u):
    h, i = pl.program_id(1), pl.program_id(2)
    qx = qx_ref[0]
    lane = lax.broadcasted_iota(jnp.int32, qx.shape, 1)
    qx = jnp.where(lane // FOX_EXT_STRIDE == h, qx, jnp.zeros_like(qx))
    q = jnp.concatenate([q_ref[0], qx], axis=1)
    ones = _ones_block(u)

    def k_block(j):
        r = pl.multiple_of(j * u, u)
        return jnp.concatenate([k_ref[0, pl.ds(r, u), :], kx_ref[0, pl.ds(r, u), :]], axis=1)

    def v_block(j):
        return jnp.concatenate([v_ref[0, pl.ds(pl.multiple_of(j * u, u), u), :], ones], axis=1)

    acc, = _flash_causal(i, t, u, [q], k_block, v_block, scr)
    o_ref[0] = (acc[:, :HEAD_DIM] / acc[:, HEAD_DIM:]).astype(o_ref.dtype)


def _fox_attention(p3, qx, kx, t=2048, u=256):
    batch, seq, _ = p3.shape
    return pl.pallas_call(
        functools.partial(_fox_attn_kernel, t=t, u=u),
        grid=(batch, N_HEADS, seq // t),
        in_specs=[
            pl.BlockSpec((1, t, LANE), lambda b, h, i: (b, i, 12 + h)),
            pl.BlockSpec((1, t, LANE), lambda b, h, i: (b, i, 0)),
            pl.BlockSpec((1, seq, LANE), lambda b, h, i: (b, 0, 16 + h)),
            pl.BlockSpec((1, seq, LANE), lambda b, h, i: (b, 0, 0)),
            pl.BlockSpec((1, seq, LANE), lambda b, h, i: (b, 0, 20 + h)),
        ],
        out_specs=pl.BlockSpec((1, t, LANE), lambda b, h, i: (b, i, h)),
        out_shape=jax.ShapeDtypeStruct((batch, seq, N_HEADS * HEAD_DIM), BF16),
        scratch_shapes=_flash_scratch(1, t, u),
        compiler_params=_cparams(("arbitrary", "arbitrary", "arbitrary")),
        name="fox_attn",
    )(p3, qx, p3, kx, p3)


def _diff_attn_kernel(q_ref, k_ref, v_ref, lq1_ref, lk1_ref, lq2_ref, lk2_ref, gs_ref, o_ref,
                      *scr, t, u, lam_init):
    i = pl.program_id(2)
    q = q_ref[0]
    lane = lax.broadcasted_iota(jnp.int32, q.shape, 1)
    zero = jnp.zeros_like(q)
    q1 = jnp.where(lane < DIFF_HALF, q, zero)
    q2 = jnp.where(lane >= DIFF_HALF, q, zero)
    ones = _ones_block(u)
    k_block = lambda j: k_ref[0, pl.ds(pl.multiple_of(j * u, u), u), :]
    v_block = lambda j: jnp.concatenate(
        [v_ref[0, pl.ds(pl.multiple_of(j * u, u), u), :], ones], axis=1)
    a1, a2 = _flash_causal(i, t, u, [q1, q2], k_block, v_block, scr)
    lam = (jnp.exp(jnp.sum(lq1_ref[...] * lk1_ref[...], axis=-1, keepdims=True))
           - jnp.exp(jnp.sum(lq2_ref[...] * lk2_ref[...], axis=-1, keepdims=True)) + lam_init)
    o = a1[:, :HEAD_DIM] / a1[:, HEAD_DIM:] - lam * (a2[:, :HEAD_DIM] / a2[:, HEAD_DIM:])
    o_ref[0] = (_rms(o, gs_ref[...]) * (1.0 - lam_init)).astype(o_ref.dtype)


def _diff_attention(p3, lq1, lk1, lq2, lk2, g_sub, lam_init, t=2048, u=256):
    batch, seq, _ = p3.shape
    vec = lambda a: a.reshape(1, -1).astype(F32)
    return pl.pallas_call(
        functools.partial(_diff_attn_kernel, t=t, u=u, lam_init=lam_init),
        grid=(batch, N_HEADS, seq // t),
        in_specs=[
            pl.BlockSpec((1, t, LANE), lambda b, h, i: (b, i, h)),
            pl.BlockSpec((1, seq, LANE), lambda b, h, i: (b, 0, 4 + h)),
            pl.BlockSpec((1, seq, LANE), lambda b, h, i: (b, 0, 8 + h)),
            _const_spec((1, DIFF_HALF)), _const_spec((1, DIFF_HALF)),
            _const_spec((1, DIFF_HALF)), _const_spec((1, DIFF_HALF)),
            _const_spec((1, HEAD_DIM)),
        ],
        out_specs=pl.BlockSpec((1, t, LANE), lambda b, h, i: (b, i, h)),
        out_shape=jax.ShapeDtypeStruct((batch, seq, N_HEADS * HEAD_DIM), BF16),
        scratch_shapes=_flash_scratch(2, t, u),
        compiler_params=_cparams(("arbitrary", "arbitrary", "arbitrary")),
        name="diff_attn",
    )(p3, p3, p3, vec(lq1), vec(lk1), vec(lq2), vec(lk2), vec(g_sub))


def _gla_kernel(q_ref, k_ref, v_ref, g_ref, bc_ref, on_ref, o_ref, st_ref, *, tc):
    i = pl.program_id(2)

    @pl.when(i == 0)
    def _():
        st_ref[...] = jnp.zeros_like(st_ref)

    C = GLA_FAST_CHUNK

    def finish(o, rows):
        y = _rms(o, on_ref[...])
        g = g_ref[0, rows, :].astype(F32)
        o_ref[0, rows, :] = (y * (g * jax.nn.sigmoid(g))).astype(o_ref.dtype)

    def update_state(st, k, v, b, n):
        b_last = b[n - 1:n, :]
        k_dec = (k * jnp.exp(b_last - b)).astype(BF16)
        vt_k = lax.dot_general(v, k_dec, (((0,), (0,)), ((), ())), preferred_element_type=F32)
        st_ref[...] = st * jnp.exp(b_last) + vt_k

    safe = jnp.min(bc_ref[0]) >= -GLA_SAFE_DECAY

    @pl.when(safe)
    def _():
        r = lax.broadcasted_iota(jnp.int32, (C, C), 0)
        cc = lax.broadcasted_iota(jnp.int32, (C, C), 1)
        causal = cc <= r
        for c in range(tc // C):
            rows = slice(c * C, (c + 1) * C)
            q = q_ref[0, rows, :].astype(F32)
            k = k_ref[0, rows, :].astype(F32)
            v = v_ref[0, rows, :]
            b = bc_ref[0, rows, :]
            st = st_ref[...]
            qe = (q * jnp.exp(b)).astype(BF16)
            ke = (k * jnp.exp(-b)).astype(BF16)
            o = _dot_nt(qe, st.astype(BF16))
            a = jnp.where(causal, _dot_nt(qe, ke), 0.0)
            o = o + _dot(a.astype(BF16), v)
            update_state(st, k, v, b, C)
            finish(o, rows)

    @pl.when(jnp.logical_not(safe))
    def _():
        L, SB = GLA_CHUNK, GLA_SUB
        row = lax.broadcasted_iota(jnp.int32, (L, HEAD_DIM), 0)
        sub_row = lax.broadcasted_iota(jnp.int32, (SB, 1), 0)

        def chunk(c, _):
            r0 = pl.multiple_of(c * L, L)
            rows = pl.ds(r0, L)
            q = q_ref[0, rows, :].astype(F32)
            k = k_ref[0, rows, :].astype(F32)
            v = v_ref[0, rows, :]
            vf = v.astype(F32)
            before = bc_ref[0, pl.ds(jnp.maximum(r0 - 1, 0), 1), :]
            b = bc_ref[0, rows, :] - jnp.where(r0 % C == 0, 0.0, before)
            st = st_ref[...]

            o = _dot_nt((q * jnp.exp(b)).astype(BF16), st.astype(BF16))

            qs, ks = [], []
            for s in range(1, L // SB):
                ref = b[s * SB - 1:s * SB, :]
                in_sub = (row >= s * SB) & (row < (s + 1) * SB)
                qs.append(jnp.where(in_sub, q * jnp.exp(jnp.minimum(b - ref, 0.0)), 0.0))
                ks.append(jnp.where(row < s * SB, k * jnp.exp(jnp.minimum(ref - b, 0.0)), 0.0))
            a = _dot_nt(jnp.concatenate(qs, axis=1).astype(BF16),
                        jnp.concatenate(ks, axis=1).astype(BF16))
            o = o + _dot(a.astype(BF16), v)

            diag = []
            for s in range(L // SB):
                sl = slice(s * SB, (s + 1) * SB)
                qd, kd, bd, vd = q[sl], k[sl], b[sl], vf[sl]
                acc = jnp.zeros((SB, GLA_DV), F32)
                for j in range(SB):
                    w = qd * kd[j:j + 1] * jnp.exp(jnp.minimum(bd - bd[j:j + 1], 0.0))
                    d = jnp.sum(w, axis=-1, keepdims=True)
                    acc = acc + jnp.where(sub_row >= j, d, 0.0) * vd[j:j + 1]
                diag.append(acc)
            o = o + jnp.concatenate(diag, axis=0)

            update_state(st, k, v, b, L)
            finish(o, rows)
            return 0

        lax.fori_loop(0, tc // L, chunk, 0)


def _gla(p3, bc3, g_onorm, tc=1024):
    batch, seq, _ = p3.shape
    return pl.pallas_call(
        functools.partial(_gla_kernel, tc=tc),
        grid=(batch, N_HEADS, seq // tc),
        in_specs=[
            pl.BlockSpec((1, tc, HEAD_DIM), lambda b, h, i: (b, i, h)),
            pl.BlockSpec((1, tc, HEAD_DIM), lambda b, h, i: (b, i, 4 + h)),
            pl.BlockSpec((1, tc, GLA_DV), lambda b, h, i: (b, i, 4 + h)),
            pl.BlockSpec((1, tc, GLA_DV), lambda b, h, i: (b, i, 8 + h)),
            pl.BlockSpec((1, tc, HEAD_DIM), lambda b, h, i: (b, i, h)),
            _const_spec((1, GLA_DV)),
        ],
        out_specs=pl.BlockSpec((1, tc, GLA_DV), lambda b, h, i: (b, i, h)),
        out_shape=jax.ShapeDtypeStruct((batch, seq, N_HEADS * GLA_DV), BF16),
        scratch_shapes=[pltpu.VMEM((GLA_DV, HEAD_DIM), F32)],
        compiler_params=_cparams(("arbitrary", "arbitrary", "arbitrary")),
        name="gla",
    )(p3, p3, p3, p3, bc3, g_onorm.reshape(1, -1).astype(F32))


def _tail_kernel(*refs, mix_widths, final_norm):
    n_mix = len(mix_widths)
    h_ref = refs[0]
    o_refs = refs[1:1 + n_mix]
    wo_ref, gm_ref, wu_ref, wd_ref, gp_ref, wg_ref, p_ref, wp_ref = refs[1 + n_mix:9 + n_mix]
    gf_ref = refs[9 + n_mix] if final_norm else None
    out_ref = refs[-1]

    mixed, c0 = None, 0
    for o_ref, w in zip(o_refs, mix_widths):
        term = _dot(o_ref[...], wo_ref[c0:c0 + w, :])
        mixed = term if mixed is None else mixed + term
        c0 += w
    h = h_ref[...] + mixed
    wide = lambda s: jnp.concatenate([s] * (D_MODEL // LANE), axis=1)
    hg, r = _rms_factors(h, gm_ref[...])
    mlp = None
    for c0 in range(0, D_FF, FF_CHUNK):
        u = jnp.square(jnp.maximum(_dot(hg, wu_ref[:, c0:c0 + FF_CHUNK]), 0.0))
        term = _dot(u.astype(BF16), wd_ref[c0:c0 + FF_CHUNK, :])
        mlp = term if mlp is None else mlp + term
    h = h + wide(r * r) * mlp
    hg, r = _rms_factors(h, gp_ref[...])
    gate = jax.nn.sigmoid(wide(r) * _dot(hg, wg_ref[...]))
    h = h + gate * _dot(p_ref[...].astype(BF16), wp_ref[...])
    if final_norm:
        h = _rms(h, gf_ref[...])
    out_ref[...] = h


def _tail(h, mix, w_o, w_o_layer, g_mlp, w_up, w_down, g_ple, w_gate, p_all, w_proj, layer, g_final,
          tm=512):
    t = h.shape[0]
    row = lambda a: a.reshape(1, -1).astype(F32)
    args = [h, *mix, w_o, row(g_mlp), w_up, w_down, row(g_ple), w_gate, p_all, w_proj]
    in_specs = [pl.BlockSpec((tm, D_MODEL), lambda i: (i, 0))]
    in_specs += [pl.BlockSpec((tm, m.shape[1]), lambda i: (i, 0)) for m in mix]
    in_specs += [_layer_spec(w_o.shape[1:], w_o_layer), _const_spec((1, D_MODEL)),
                 _layer_spec(w_up.shape[1:], layer), _layer_spec(w_down.shape[1:], layer),
                 _const_spec((1, D_MODEL)), _layer_spec(w_gate.shape[1:], layer),
                 pl.BlockSpec((None, tm, PLE_DIM), lambda i: (layer, i, 0)),
                 _layer_spec(w_proj.shape[1:], layer)]
    if g_final is not None:
        args.append(row(g_final))
        in_specs.append(_const_spec((1, D_MODEL)))
    return pl.pallas_call(
        functools.partial(_tail_kernel, mix_widths=tuple(m.shape[1] for m in mix),
                          final_norm=g_final is not None),
        grid=(t // tm,),
        in_specs=in_specs,
        out_specs=pl.BlockSpec((tm, D_MODEL), lambda i: (i, 0)),
        out_shape=jax.ShapeDtypeStruct((t, D_MODEL), F32),
        compiler_params=_cparams(("arbitrary",)),
        name="tail",
    )(*args)


def _col_scale(scales):
    cs = np.ones((1, MAIN_W), np.float32)
    for (c0, c1), v in scales.items():
        cs[0, c0:c1] = v
    return jnp.asarray(cs)


def kernel(x, p, norm_mix, norm_mlp, norm_ple, norm_final, even_w_in, even_b_f, lam_q1, lam_k1, lam_q2, lam_k2, diff_subln, even_w_o, odd_w_in, odd_w_gk, odd_b_gk, gla_onorm, odd_w_o, w_up, w_down, w_ple_gate, w_ple_proj):
    batch, seq, _ = x.shape
    depth = p.shape[0]
    t = batch * seq
    h = x.reshape(t, D_MODEL)
    p_all = p.reshape(depth, t, PLE_DIM)
    even_w_o, odd_w_o, w_up, w_down, w_ple_gate, w_ple_proj = (
        w.astype(BF16) for w in (even_w_o, odd_w_o, w_up, w_down, w_ple_gate, w_ple_proj))
    for i in range(depth):
        j = i // 2
        if i % 2 == 0:
            cs = _col_scale({(0, 512): DIFF_HALF ** -0.5 * LOG2E,
                             (1536, 2048): HEAD_DIM ** -0.5 * LOG2E})
            proj, qx, kx = _inproj_fox(h, norm_mix[i], even_w_in, j, even_b_f[j], cs, seq)
            p3 = proj.reshape(batch, seq, MAIN_W)
            lam_init = 0.8 - 0.6 * math.exp(-0.3 * i)
            oa = _diff_attention(p3, lam_q1[j], lam_k1[j], lam_q2[j], lam_k2[j],
                                 diff_subln[j], lam_init)
            ob = _fox_attention(p3, qx.reshape(batch, seq, LANE), kx.reshape(batch, seq, LANE))
            mix = [oa.reshape(t, -1), ob.reshape(t, -1)]
            w_o = even_w_o
        else:
            cs = _col_scale({(0, 512): HEAD_DIM ** -0.5})
            proj, bc = _inproj_gla(h, norm_mix[i], odd_w_in, j, odd_w_gk[j], odd_b_gk[j], cs)
            o = _gla(proj.reshape(batch, seq, MAIN_W), bc.reshape(batch, seq, -1), gla_onorm[j])
            mix = [o.reshape(t, -1)]
            w_o = odd_w_o
        h = _tail(h, mix, w_o, j, norm_mlp[i], w_up, w_down, norm_ple[i], w_ple_gate, p_all,
                  w_ple_proj, i, norm_final if i == depth - 1 else None)
    return h.reshape(batch, seq, D_MODEL)
```

```python
import functools
import math

import jax
import jax.numpy as jnp
import numpy as np
from jax import lax
from jax.experimental import pallas as pl
from jax.experimental.pallas import tpu as pltpu

F32 = jnp.float32
BF16 = jnp.bfloat16

D_MODEL = 1024
PLE_DIM = 256
N_HEADS = 4
HEAD_DIM = 128
DIFF_HALF = 64
GLA_DV = 256
GLA_LOWRANK = 16
GLA_GATE_NORM = 16.0
GLA_CHUNK = 64
GLA_SUB = 16
GLA_FAST_CHUNK = 128
GLA_SAFE_DECAY = 60.0
FOX_EXT_STRIDE = 8
D_FF = 4 * D_MODEL
FF_CHUNK = 2048
EPS = 1e-6
LOG2E = 1.4426950408889634
NEG_BIG = -1e30
LANE = 128
MAIN_W = 3072

VMEM_LIMIT = 56 * 1024 * 1024


def _cparams(sem):
    return pltpu.CompilerParams(dimension_semantics=sem, vmem_limit_bytes=VMEM_LIMIT)


def _const_spec(shape):
    nd = len(shape)
    return pl.BlockSpec(shape, lambda *_: (0,) * nd, pipeline_mode=pl.Buffered(1))


def _layer_spec(shape, layer):
    nd = len(shape)
    return pl.BlockSpec((None,) + tuple(shape), lambda *_: (layer,) + (0,) * nd,
                        pipeline_mode=pl.Buffered(1))


def _rms(x, g):
    return x * lax.rsqrt(jnp.mean(x * x, axis=-1, keepdims=True) + EPS) * g


def _rms_factors(x, g):
    scale = lax.rsqrt(jnp.mean(x * x, axis=-1, keepdims=True) + EPS)
    return (x * g).astype(BF16), jnp.broadcast_to(scale, (x.shape[0], LANE))


def _log_sigmoid(x):
    return jnp.minimum(x, 0.0) - jnp.log(1.0 + jnp.exp(-jnp.abs(x)))


def _split3(x):
    hi = x.astype(BF16)
    r1 = x - hi.astype(F32)
    mid = r1.astype(BF16)
    lo = (r1 - mid.astype(F32)).astype(BF16)
    return hi, mid, lo


def _dot(a, b):
    return jnp.dot(a, b, preferred_element_type=F32)


def _cumsum_rows(x, tri):
    w = x.shape[1]
    y = _dot(tri, jnp.concatenate(_split3(x), axis=1))
    return y[:, :w] + y[:, w:2 * w] + y[:, 2 * w:]


def _dot_nt(a, b):
    return lax.dot_general(a, b, (((1,), (1,)), ((), ())), preferred_element_type=F32)


def _project(x_ref, g_ref, w_ref, wg_ref, cs_ref, o_ref, wb_scr, n_chunk, gate_steps):
    @pl.when(pl.program_id(0) == 0)
    def _():
        for c0 in range(0, MAIN_W, n_chunk):
            wb_scr[:, c0:c0 + n_chunk] = w_ref[c0:c0 + n_chunk, :].T.astype(BF16)

    xg, scale = _rms_factors(x_ref[...], g_ref[...])
    value = _dot(xg, wg_ref[...]) * scale
    scale_n = jnp.concatenate([scale] * (n_chunk // LANE), axis=1)
    steps = list(gate_steps)
    for c0 in range(0, MAIN_W, n_chunk):
        y = _dot(xg, wb_scr[:, c0:c0 + n_chunk]) * cs_ref[:, c0:c0 + n_chunk] * scale_n
        o_ref[:, c0:c0 + n_chunk] = y.astype(o_ref.dtype)
        if steps:
            value = steps.pop(0)(value)
    for step in steps:
        value = step(value)


def _inproj_fox_kernel(x_ref, g_ref, w_ref, wg_ref, cs_ref, bf_ref, tri_ref, eq_ref, ek_ref,
                       oq_ref, ok_ref, o_ref, qx_ref, kx_ref, wb_scr, carry_ref,
                       *, n_chunk, steps_per_seq):
    @pl.when(pl.program_id(0) % steps_per_seq == 0)
    def _():
        carry_ref[...] = jnp.zeros_like(carry_ref)

    def cumulate(gate):
        logf = _log_sigmoid(gate + bf_ref[...]) * LOG2E
        c = carry_ref[...] + _cumsum_rows(logf, tri_ref[...])
        tm = c.shape[0]
        carry_ref[...] = c[tm - 1:tm, :]
        return c

    def extend(c):
        terms = jnp.concatenate(_split3(c), axis=1)
        qx_ref[...] = (_dot(terms, eq_ref[...]) + oq_ref[...]).astype(BF16)
        kx_ref[...] = (ok_ref[...] - _dot(terms, ek_ref[...])).astype(BF16)

    _project(x_ref, g_ref, w_ref, wg_ref, cs_ref, o_ref, wb_scr, n_chunk, [cumulate, extend])


def _inproj_gla_kernel(x_ref, g_ref, w_ref, wg_ref, cs_ref, wgk_ref, bgk_ref, tri_ref,
                       o_ref, bc_ref, wb_scr, *, n_chunk):
    C = GLA_FAST_CHUNK

    def decay(gl):
        x = _dot(gl.astype(BF16), wgk_ref[...]) + bgk_ref[...]
        return _log_sigmoid(x) * (1.0 / GLA_GATE_NORM)

    def cumulate(c):
        def step(gk):
            bc_ref[c * C:(c + 1) * C, :] = _cumsum_rows(gk[c * C:(c + 1) * C, :], tri_ref[...])
            return gk
        return step

    n_row_chunks = x_ref.shape[0] // C
    _project(x_ref, g_ref, w_ref, wg_ref, cs_ref, o_ref, wb_scr, n_chunk,
             [decay] + [cumulate(c) for c in range(n_row_chunks)])


def _inproj_common(x, g, w_in, layer, n_gate, colscale, tm):
    w_gate = jnp.pad(w_in[layer, :, MAIN_W:], ((0, 0), (0, LANE - n_gate))).astype(BF16)
    args = [x, g.reshape(1, -1).astype(F32), jnp.swapaxes(w_in, 1, 2), w_gate, colscale]
    in_specs = [
        pl.BlockSpec((tm, D_MODEL), lambda i: (i, 0)),
        _const_spec((1, D_MODEL)),
        _layer_spec((MAIN_W, D_MODEL), layer),
        _const_spec((D_MODEL, LANE)),
        _const_spec((1, MAIN_W)),
    ]
    return args, in_specs


def _inproj_fox(x, g, w_in, layer, b_f, colscale, seq, tm=512):
    t = x.shape[0]
    args, in_specs = _inproj_common(x, g, w_in, layer, N_HEADS, colscale, tm)
    bf = jnp.pad(b_f.astype(F32).reshape(1, -1), ((0, 0), (0, LANE - N_HEADS)))
    r = np.arange(tm)
    tri = jnp.asarray(r[:, None] >= r[None, :], BF16)
    eq = np.zeros((3 * LANE, LANE), np.float32)
    ek = np.zeros((3 * LANE, LANE), np.float32)
    oq = np.zeros((1, LANE), np.float32)
    ok = np.zeros((1, LANE), np.float32)
    for h in range(N_HEADS):
        for j in range(3):
            eq[j * LANE + h, FOX_EXT_STRIDE * h + j] = 1.0
            ek[j * LANE + h, FOX_EXT_STRIDE * h + 3 + j] = 1.0
            oq[0, FOX_EXT_STRIDE * h + 3 + j] = 1.0
            ok[0, FOX_EXT_STRIDE * h + j] = 1.0
    args += [bf, tri, jnp.asarray(eq, BF16), jnp.asarray(ek, BF16), jnp.asarray(oq), jnp.asarray(ok)]
    in_specs += [_const_spec((1, LANE)), _const_spec((tm, tm)), _const_spec((3 * LANE, LANE)),
                 _const_spec((3 * LANE, LANE)), _const_spec((1, LANE)), _const_spec((1, LANE))]
    row_spec = lambda w: pl.BlockSpec((tm, w), lambda i: (i, 0))
    return pl.pallas_call(
        functools.partial(_inproj_fox_kernel, n_chunk=1024, steps_per_seq=seq // tm),
        grid=(t // tm,),
        in_specs=in_specs,
        out_specs=[row_spec(MAIN_W), row_spec(LANE), row_spec(LANE)],
        out_shape=[jax.ShapeDtypeStruct((t, MAIN_W), BF16), jax.ShapeDtypeStruct((t, LANE), BF16),
                   jax.ShapeDtypeStruct((t, LANE), BF16)],
        scratch_shapes=[pltpu.VMEM((D_MODEL, MAIN_W), BF16), pltpu.VMEM((1, LANE), F32)],
        compiler_params=_cparams(("arbitrary",)),
        name="inproj_fox",
    )(*args)


def _inproj_gla(x, g, w_in, layer, w_gk, b_gk, colscale, tm=512):
    t = x.shape[0]
    args, in_specs = _inproj_common(x, g, w_in, layer, GLA_LOWRANK, colscale, tm)
    wgk = jnp.pad(w_gk, ((0, LANE - GLA_LOWRANK), (0, 0))).astype(BF16)
    r = np.arange(GLA_FAST_CHUNK)
    tri = jnp.asarray(r[:, None] >= r[None, :], BF16)
    args += [wgk, b_gk.reshape(1, -1).astype(F32), tri]
    in_specs += [_const_spec(wgk.shape), _const_spec((1, N_HEADS * HEAD_DIM)),
                 _const_spec((GLA_FAST_CHUNK, GLA_FAST_CHUNK))]
    row_spec = lambda w: pl.BlockSpec((tm, w), lambda i: (i, 0))
    return pl.pallas_call(
        functools.partial(_inproj_gla_kernel, n_chunk=1024),
        grid=(t // tm,),
        in_specs=in_specs,
        out_specs=[row_spec(MAIN_W), row_spec(N_HEADS * HEAD_DIM)],
        out_shape=[jax.ShapeDtypeStruct((t, MAIN_W), BF16),
                   jax.ShapeDtypeStruct((t, N_HEADS * HEAD_DIM), F32)],
        scratch_shapes=[pltpu.VMEM((D_MODEL, MAIN_W), BF16)],
        compiler_params=_cparams(("arbitrary",)),
        name="inproj_gla",
    )(*args)


def _flash_causal(i, t, u, qs, k_block, v_block, scr):
    n = len(qs)
    g = t // u
    per = g * n
    assert per % 2 == 0
    s_slot, p_slot, (acc, m_ref, alpha_ref) = scr[0:2], scr[2:4], scr[4:]

    def lanes(x, width):
        return jnp.concatenate([x] * (width // LANE), axis=1)

    def logits(a, j, slot, r0=0):
        s_slot[slot][r0:, :] = _dot_nt(qs[a][r0:], k_block(j))

    def pv(a, j, slot, r0=0):
        acc[a, r0:, :] = (lanes(alpha_ref[slot, r0:, :], 2 * HEAD_DIM) * acc[a, r0:, :]
                          + _dot(p_slot[slot][r0:, :], v_block(jnp.maximum(j, 0))))

    def softmax(a, slot, mask, r0=0):
        s_ref = s_slot[slot]
        if mask is None:
            load = lambda: s_ref[r0:, :]
        else:
            load = lambda: jnp.where(mask, s_ref[r0:, :], NEG_BIG)
        m_old = m_ref[a, r0:, :]
        m_new = jnp.maximum(m_old, jnp.max(load(), axis=-1, keepdims=True))
        m_ref[a, r0:, :] = m_new
        alpha_ref[slot, r0:, :] = jnp.exp2(m_old - m_new)
        p_slot[slot][r0:, :] = jnp.exp2(load() - lanes(m_new, u)).astype(BF16)

    def group(j0, tri, last):
        row0 = lambda e: 0 if tri is None or e < 0 else (e // n) * u
        for e in range(per):
            slot = e & 1
            if e + 1 < per:
                logits((e + 1) % n, j0 + (e + 1) // n, 1 - slot, row0(e + 1))
            elif not last:
                logits(0, j0 + g, 1 - slot)
            if e > 0:
                pv((e - 1) % n, j0 + (e - 1) // n, 1 - slot, row0(e - 1))
            else:
                pv(n - 1, j0 - 1, 1 - slot)
            softmax(e % n, slot, None if tri is None else tri[:t - row0(e)], row0(e))

    m_ref[...] = jnp.full(m_ref.shape, NEG_BIG, F32)
    alpha_ref[1] = jnp.ones(alpha_ref.shape[1:], F32)
    acc[...] = jnp.zeros(acc.shape, F32)
    p_slot[1][...] = jnp.zeros(p_slot[1].shape, BF16)
    logits(0, 0, 0)

    def body(k, _):
        group(k * g, None, False)
        return 0

    lax.fori_loop(0, i, body, 0)
    r = lax.broadcasted_iota(jnp.int32, (t, u), 0)
    c = lax.broadcasted_iota(jnp.int32, (t, u), 1)
    group(i * g, c <= r, True)
    pv(n - 1, i * g + g - 1, (per - 1) & 1, (g - 1) * u)
    return [acc[a] for a in range(n)]


def _flash_scratch(n, t, u):
    return [pltpu.VMEM((t, u), F32), pltpu.VMEM((t, u), F32),
            pltpu.VMEM((t, u), BF16), pltpu.VMEM((t, u), BF16),
            pltpu.VMEM((n, t, 2 * HEAD_DIM), F32),
            pltpu.VMEM((n, t, LANE), F32), pltpu.VMEM((2, t, LANE), F32)]


def _ones_block(u):
    return jnp.ones((u, LANE), BF16)


def _fox_attn_kernel(q_ref, qx_ref, k_ref, kx_ref, v_ref, o_ref, *scr, t, u):
    h, i = pl.program_id(1), pl.program_id(2)
    qx = qx_ref[0]
    lane = lax.broadcasted_iota(jnp.int32, qx.shape, 1)
    qx = jnp.where(lane // FOX_EXT_STRIDE == h, qx, jnp.zeros_like(qx))
    q = jnp.concatenate([q_ref[0], qx], axis=1)
    ones = _ones_block(u)

    def k_block(j):
        r = pl.multiple_of(j * u, u)
        return jnp.concatenate([k_ref[0, pl.ds(r, u), :], kx_ref[0, pl.ds(r, u), :]], axis=1)

    def v_block(j):
        return jnp.concatenate([v_ref[0, pl.ds(pl.multiple_of(j * u, u), u), :], ones], axis=1)

    acc, = _flash_causal(i, t, u, [q], k_block, v_block, scr)
    o_ref[0] = (acc[:, :HEAD_DIM] / acc[:, HEAD_DIM:]).astype(o_ref.dtype)


def _fox_attention(p3, qx, kx, t=2048, u=256):
    batch, seq, _ = p3.shape
    return pl.pallas_call(
        functools.partial(_fox_attn_kernel, t=t, u=u),
        grid=(batch, N_HEADS, seq // t),
        in_specs=[
            pl.BlockSpec((1, t, LANE), lambda b, h, i: (b, i, 12 + h)),
            pl.BlockSpec((1, t, LANE), lambda b, h, i: (b, i, 0)),
            pl.BlockSpec((1, seq, LANE), lambda b, h, i: (b, 0, 16 + h)),
            pl.BlockSpec((1, seq, LANE), lambda b, h, i: (b, 0, 0)),
            pl.BlockSpec((1, seq, LANE), lambda b, h, i: (b, 0, 20 + h)),
        ],
        out_specs=pl.BlockSpec((1, t, LANE), lambda b, h, i: (b, i, h)),
        out_shape=jax.ShapeDtypeStruct((batch, seq, N_HEADS * HEAD_DIM), BF16),
        scratch_shapes=_flash_scratch(1, t, u),
        compiler_params=_cparams(("arbitrary", "arbitrary", "arbitrary")),
        name="fox_attn",
    )(p3, qx, p3, kx, p3)


def _diff_attn_kernel(q_ref, k_ref, v_ref, lq1_ref, lk1_ref, lq2_ref, lk2_ref, gs_ref, o_ref,
                      *scr, t, u, lam_init):
    i = pl.program_id(2)
    q = q_ref[0]
    lane = lax.broadcasted_iota(jnp.int32, q.shape, 1)
    zero = jnp.zeros_like(q)
    q1 = jnp.where(lane < DIFF_HALF, q, zero)
    q2 = jnp.where(lane >= DIFF_HALF, q, zero)
    ones = _ones_block(u)
    k_block = lambda j: k_ref[0, pl.ds(pl.multiple_of(j * u, u), u), :]
    v_block = lambda j: jnp.concatenate(
        [v_ref[0, pl.ds(pl.multiple_of(j * u, u), u), :], ones], axis=1)
    a1, a2 = _flash_causal(i, t, u, [q1, q2], k_block, v_block, scr)
    lam = (jnp.exp(jnp.sum(lq1_ref[...] * lk1_ref[...], axis=-1, keepdims=True))
           - jnp.exp(jnp.sum(lq2_ref[...] * lk2_ref[...], axis=-1, keepdims=True)) + lam_init)
    o = a1[:, :HEAD_DIM] / a1[:, HEAD_DIM:] - lam * (a2[:, :HEAD_DIM] / a2[:, HEAD_DIM:])
    o_ref[0] = (_rms(o, gs_ref[...]) * (1.0 - lam_init)).astype(o_ref.dtype)


def _diff_attention(p3, lq1, lk1, lq2, lk2, g_sub, lam_init, t=2048, u=256):
    batch, seq, _ = p3.shape
    vec = lambda a: a.reshape(1, -1).astype(F32)
    return pl.pallas_call(
        functools.partial(_diff_attn_kernel, t=t, u=u, lam_init=lam_init),
        grid=(batch, N_HEADS, seq // t),
        in_specs=[
            pl.BlockSpec((1, t, LANE), lambda b, h, i: (b, i, h)),
            pl.BlockSpec((1, seq, LANE), lambda b, h, i: (b, 0, 4 + h)),
            pl.BlockSpec((1, seq, LANE), lambda b, h, i: (b, 0, 8 + h)),
            _const_spec((1, DIFF_HALF)), _const_spec((1, DIFF_HALF)),
            _const_spec((1, DIFF_HALF)), _const_spec((1, DIFF_HALF)),
            _const_spec((1, HEAD_DIM)),
        ],
        out_specs=pl.BlockSpec((1, t, LANE), lambda b, h, i: (b, i, h)),
        out_shape=jax.ShapeDtypeStruct((batch, seq, N_HEADS * HEAD_DIM), BF16),
        scratch_shapes=_flash_scratch(2, t, u),
        compiler_params=_cparams(("arbitrary", "arbitrary", "arbitrary")),
        name="diff_attn",
    )(p3, p3, p3, vec(lq1), vec(lk1), vec(lq2), vec(lk2), vec(g_sub))


def _gla_kernel(q_ref, k_ref, v_ref, g_ref, bc_ref, bc_next_ref, on_ref, o_ref, st_ref, low_ref,
                *, tc):
    i = pl.program_id(2)

    @pl.when(i == 0)
    def _():
        st_ref[...] = jnp.zeros_like(st_ref)
        low_ref[0] = jnp.min(bc_ref[0])

    C = GLA_FAST_CHUNK

    def finish(o, rows):
        y = _rms(o, on_ref[...])
        g = g_ref[0, rows, :].astype(F32)
        o_ref[0, rows, :] = (y * (g * jax.nn.sigmoid(g))).astype(o_ref.dtype)

    def update_state(st, k, v, b, n):
        b_last = b[n - 1:n, :]
        k_dec = (k * jnp.exp(b_last - b)).astype(BF16)
        vt_k = lax.dot_general(v, k_dec, (((0,), (0,)), ((), ())), preferred_element_type=F32)
        st_ref[...] = st * jnp.exp(b_last) + vt_k

    safe = low_ref[0] >= -GLA_SAFE_DECAY

    def publish_next_low():
        low_ref[0] = jnp.min(bc_next_ref[0])

    @pl.when(safe)
    def _():
        publish_next_low()
        r = lax.broadcasted_iota(jnp.int32, (C, C), 0)
        cc = lax.broadcasted_iota(jnp.int32, (C, C), 1)
        causal = cc <= r
        for c in range(tc // C):
            rows = slice(c * C, (c + 1) * C)
            q = q_ref[0, rows, :].astype(F32)
            k = k_ref[0, rows, :].astype(F32)
            v = v_ref[0, rows, :]
            b = bc_ref[0, rows, :]
            st = st_ref[...]
            qe = (q * jnp.exp(b)).astype(BF16)
            ke = (k * jnp.exp(-b)).astype(BF16)
            o = _dot_nt(qe, st.astype(BF16))
            a = jnp.where(causal, _dot_nt(qe, ke), 0.0)
            o = o + _dot(a.astype(BF16), v)
            update_state(st, k, v, b, C)
            finish(o, rows)

    @pl.when(jnp.logical_not(safe))
    def _():
        publish_next_low()
        L, SB = GLA_CHUNK, GLA_SUB
        row = lax.broadcasted_iota(jnp.int32, (L, HEAD_DIM), 0)
        sub_row = lax.broadcasted_iota(jnp.int32, (SB, 1), 0)

        def chunk(c, _):
            r0 = pl.multiple_of(c * L, L)
            rows = pl.ds(r0, L)
            q = q_ref[0, rows, :].astype(F32)
            k = k_ref[0, rows, :].astype(F32)
            v = v_ref[0, rows, :]
            vf = v.astype(F32)
            before = bc_ref[0, pl.ds(jnp.maximum(r0 - 1, 0), 1), :]
            b = bc_ref[0, rows, :] - jnp.where(r0 % C == 0, 0.0, before)
            st = st_ref[...]

            o = _dot_nt((q * jnp.exp(b)).astype(BF16), st.astype(BF16))

            qs, ks = [], []
            for s in range(1, L // SB):
                ref = b[s * SB - 1:s * SB, :]
                in_sub = (row >= s * SB) & (row < (s + 1) * SB)
                qs.append(jnp.where(in_sub, q * jnp.exp(jnp.minimum(b - ref, 0.0)), 0.0))
                ks.append(jnp.where(row < s * SB, k * jnp.exp(jnp.minimum(ref - b, 0.0)), 0.0))
            a = _dot_nt(jnp.concatenate(qs, axis=1).astype(BF16),
                        jnp.concatenate(ks, axis=1).astype(BF16))
            o = o + _dot(a.astype(BF16), v)

            diag = []
            for s in range(L // SB):
                sl = slice(s * SB, (s + 1) * SB)
                qd, kd, bd, vd = q[sl], k[sl], b[sl], vf[sl]
                acc = jnp.zeros((SB, GLA_DV), F32)
                for j in range(SB):
                    w = qd * kd[j:j + 1] * jnp.exp(jnp.minimum(bd - bd[j:j + 1], 0.0))
                    d = jnp.sum(w, axis=-1, keepdims=True)
                    acc = acc + jnp.where(sub_row >= j, d, 0.0) * vd[j:j + 1]
                diag.append(acc)
            o = o + jnp.concatenate(diag, axis=0)

            update_state(st, k, v, b, L)
            finish(o, rows)
            return 0

        lax.fori_loop(0, tc // L, chunk, 0)


def _gla(p3, bc3, g_onorm, tc=1024):
    batch, seq, _ = p3.shape
    last = seq // tc - 1
    return pl.pallas_call(
        functools.partial(_gla_kernel, tc=tc),
        grid=(batch, N_HEADS, seq // tc),
        in_specs=[
            pl.BlockSpec((1, tc, HEAD_DIM), lambda b, h, i: (b, i, h)),
            pl.BlockSpec((1, tc, HEAD_DIM), lambda b, h, i: (b, i, 4 + h)),
            pl.BlockSpec((1, tc, GLA_DV), lambda b, h, i: (b, i, 4 + h)),
            pl.BlockSpec((1, tc, GLA_DV), lambda b, h, i: (b, i, 8 + h)),
            pl.BlockSpec((1, tc, HEAD_DIM), lambda b, h, i: (b, i, h)),
            pl.BlockSpec((1, tc, HEAD_DIM), lambda b, h, i: (b, jnp.minimum(i + 1, last), h)),
            _const_spec((1, GLA_DV)),
        ],
        out_specs=pl.BlockSpec((1, tc, GLA_DV), lambda b, h, i: (b, i, h)),
        out_shape=jax.ShapeDtypeStruct((batch, seq, N_HEADS * GLA_DV), BF16),
        scratch_shapes=[pltpu.VMEM((GLA_DV, HEAD_DIM), F32), pltpu.SMEM((1,), F32)],
        compiler_params=_cparams(("arbitrary", "arbitrary", "arbitrary")),
        name="gla",
    )(p3, p3, p3, p3, bc3, bc3, g_onorm.reshape(1, -1).astype(F32))


def _tail_kernel(*refs, mix_widths, final_norm):
    n_mix = len(mix_widths)
    h_ref = refs[0]
    o_refs = refs[1:1 + n_mix]
    wo_ref, gm_ref, wu_ref, wd_ref, gp_ref, wg_ref, p_ref, wp_ref = refs[1 + n_mix:9 + n_mix]
    gf_ref = refs[9 + n_mix] if final_norm else None
    out_ref = refs[-1]

    mixed, c0 = None, 0
    for o_ref, w in zip(o_refs, mix_widths):
        term = _dot(o_ref[...], wo_ref[c0:c0 + w, :])
        mixed = term if mixed is None else mixed + term
        c0 += w
    h = h_ref[...] + mixed
    wide = lambda s: jnp.concatenate([s] * (D_MODEL // LANE), axis=1)
    hg, r = _rms_factors(h, gm_ref[...])
    mlp = None
    for c0 in range(0, D_FF, FF_CHUNK):
        u = jnp.square(jnp.maximum(_dot(hg, wu_ref[:, c0:c0 + FF_CHUNK]), 0.0))
        term = _dot(u.astype(BF16), wd_ref[c0:c0 + FF_CHUNK, :])
        mlp = term if mlp is None else mlp + term
    h = h + wide(r * r) * mlp
    hg, r = _rms_factors(h, gp_ref[...])
    gate = jax.nn.sigmoid(wide(r) * _dot(hg, wg_ref[...]))
    h = h + gate * _dot(p_ref[...].astype(BF16), wp_ref[...])
    if final_norm:
        h = _rms(h, gf_ref[...])
    out_ref[...] = h


def _tail(h, mix, w_o, w_o_layer, g_mlp, w_up, w_down, g_ple, w_gate, p_all, w_proj, layer, g_final,
          tm=512):
    t = h.shape[0]
    row = lambda a: a.reshape(1, -1).astype(F32)
    args = [h, *mix, w_o, row(g_mlp), w_up, w_down, row(g_ple), w_gate, p_all, w_proj]
    in_specs = [pl.BlockSpec((tm, D_MODEL), lambda i: (i, 0))]
    in_specs += [pl.BlockSpec((tm, m.shape[1]), lambda i: (i, 0)) for m in mix]
    in_specs += [_layer_spec(w_o.shape[1:], w_o_layer), _const_spec((1, D_MODEL)),
                 _layer_spec(w_up.shape[1:], layer), _layer_spec(w_down.shape[1:], layer),
                 _const_spec((1, D_MODEL)), _layer_spec(w_gate.shape[1:], layer),
                 pl.BlockSpec((None, tm, PLE_DIM), lambda i: (layer, i, 0)),
                 _layer_spec(w_proj.shape[1:], layer)]
    if g_final is not None:
        args.append(row(g_final))
        in_specs.append(_const_spec((1, D_MODEL)))
    return pl.pallas_call(
        functools.partial(_tail_kernel, mix_widths=tuple(m.shape[1] for m in mix),
                          final_norm=g_final is not None),
        grid=(t // tm,),
        in_specs=in_specs,
        out_specs=pl.BlockSpec((tm, D_MODEL), lambda i: (i, 0)),
        out_shape=jax.ShapeDtypeStruct((t, D_MODEL), F32),
        compiler_params=_cparams(("arbitrary",)),
        name="tail",
    )(*args)


def _col_scale(scales):
    cs = np.ones((1, MAIN_W), np.float32)
    for (c0, c1), v in scales.items():
        cs[0, c0:c1] = v
    return jnp.asarray(cs)


def kernel(x, p, norm_mix, norm_mlp, norm_ple, norm_final, even_w_in, even_b_f, lam_q1, lam_k1, lam_q2, lam_k2, diff_subln, even_w_o, odd_w_in, odd_w_gk, odd_b_gk, gla_onorm, odd_w_o, w_up, w_down, w_ple_gate, w_ple_proj):
    batch, seq, _ = x.shape
    depth = p.shape[0]
    t = batch * seq
    h = x.reshape(t, D_MODEL)
    p_all = p.reshape(depth, t, PLE_DIM)
    even_w_o, odd_w_o, w_up, w_down, w_ple_gate, w_ple_proj = (
        w.astype(BF16) for w in (even_w_o, odd_w_o, w_up, w_down, w_ple_gate, w_ple_proj))
    for i in range(depth):
        j = i // 2
        if i % 2 == 0:
            cs = _col_scale({(0, 512): DIFF_HALF ** -0.5 * LOG2E,
                             (1536, 2048): HEAD_DIM ** -0.5 * LOG2E})
            proj, qx, kx = _inproj_fox(h, norm_mix[i], even_w_in, j, even_b_f[j], cs, seq)
            p3 = proj.reshape(batch, seq, MAIN_W)
            lam_init = 0.8 - 0.6 * math.exp(-0.3 * i)
            oa = _diff_attention(p3, lam_q1[j], lam_k1[j], lam_q2[j], lam_k2[j],
                                 diff_subln[j], lam_init)
            ob = _fox_attention(p3, qx.reshape(batch, seq, LANE), kx.reshape(batch, seq, LANE))
            mix = [oa.reshape(t, -1), ob.reshape(t, -1)]
            w_o = even_w_o
        else:
            cs = _col_scale({(0, 512): HEAD_DIM ** -0.5})
            proj, bc = _inproj_gla(h, norm_mix[i], odd_w_in, j, odd_w_gk[j], odd_b_gk[j], cs)
            o = _gla(proj.reshape(batch, seq, MAIN_W), bc.reshape(batch, seq, -1), gla_onorm[j])
            mix = [o.reshape(t, -1)]
            w_o = odd_w_o
        h = _tail(h, mix, w_o, j, norm_mlp[i], w_up, w_down, norm_ple[i], w_ple_gate, p_all,
                  w_ple_proj, i, norm_final if i == depth - 1 else None)
    return h.reshape(batch, seq, D_MODEL)
```

```python
import functools
import math

import jax
import jax.numpy as jnp
import numpy as np
from jax import lax
from jax.experimental import pallas as pl
from jax.experimental.pallas import tpu as pltpu

F32 = jnp.float32
BF16 = jnp.bfloat16

D_MODEL = 1024
PLE_DIM = 256
N_HEADS = 4
HEAD_DIM = 128
DIFF_HALF = 64
GLA_DV = 256
GLA_LOWRANK = 16
GLA_GATE_NORM = 16.0
GLA_CHUNK = 64
GLA_SUB = 16
GLA_FAST_CHUNK = 128
GLA_SAFE_DECAY = 60.0
FOX_EXT_STRIDE = 8
D_FF = 4 * D_MODEL
FF_CHUNK = 2048
EPS = 1e-6
LOG2E = 1.4426950408889634
NEG_BIG = -1e30
LANE = 128
MAIN_W = 3072

VMEM_LIMIT = 56 * 1024 * 1024


def _cparams(sem):
    return pltpu.CompilerParams(dimension_semantics=sem, vmem_limit_bytes=VMEM_LIMIT)


def _const_spec(shape):
    nd = len(shape)
    return pl.BlockSpec(shape, lambda *_: (0,) * nd, pipeline_mode=pl.Buffered(1))


def _layer_spec(shape, layer):
    nd = len(shape)
    return pl.BlockSpec((None,) + tuple(shape), lambda *_: (layer,) + (0,) * nd,
                        pipeline_mode=pl.Buffered(1))


def _rms(x, g):
    return x * lax.rsqrt(jnp.mean(x * x, axis=-1, keepdims=True) + EPS) * g


def _rms_factors(x, g):
    scale = lax.rsqrt(jnp.mean(x * x, axis=-1, keepdims=True) + EPS)
    return (x * g).astype(BF16), jnp.broadcast_to(scale, (x.shape[0], LANE))


def _log_sigmoid(x):
    return jnp.minimum(x, 0.0) - jnp.log(1.0 + jnp.exp(-jnp.abs(x)))


def _split3(x):
    hi = x.astype(BF16)
    r1 = x - hi.astype(F32)
    mid = r1.astype(BF16)
    lo = (r1 - mid.astype(F32)).astype(BF16)
    return hi, mid, lo


def _dot(a, b):
    return jnp.dot(a, b, preferred_element_type=F32)


def _cumsum_rows(x, tri):
    w = x.shape[1]
    y = _dot(tri, jnp.concatenate(_split3(x), axis=1))
    return y[:, :w] + y[:, w:2 * w] + y[:, 2 * w:]


def _dot_nt(a, b):
    return lax.dot_general(a, b, (((1,), (1,)), ((), ())), preferred_element_type=F32)


def _project(x_ref, g_ref, w_ref, wg_ref, cs_ref, o_ref, wb_scr, n_chunk, gate_steps):
    @pl.when(pl.program_id(0) == 0)
    def _():
        for c0 in range(0, MAIN_W, n_chunk):
            wb_scr[:, c0:c0 + n_chunk] = w_ref[c0:c0 + n_chunk, :].T.astype(BF16)

    xg, scale = _rms_factors(x_ref[...], g_ref[...])
    value = _dot(xg, wg_ref[...]) * scale
    scale_n = jnp.concatenate([scale] * (n_chunk // LANE), axis=1)
    steps = list(gate_steps)
    for c0 in range(0, MAIN_W, n_chunk):
        y = _dot(xg, wb_scr[:, c0:c0 + n_chunk]) * cs_ref[:, c0:c0 + n_chunk] * scale_n
        o_ref[:, c0:c0 + n_chunk] = y.astype(o_ref.dtype)
        if steps:
            value = steps.pop(0)(value)
    for step in steps:
        value = step(value)


def _inproj_fox_kernel(x_ref, g_ref, w_ref, wg_ref, cs_ref, bf_ref, tri_ref, eq_ref, ek_ref,
                       oq_ref, ok_ref, o_ref, qx_ref, kx_ref, wb_scr, carry_ref,
                       *, n_chunk, steps_per_seq):
    @pl.when(pl.program_id(0) % steps_per_seq == 0)
    def _():
        carry_ref[...] = jnp.zeros_like(carry_ref)

    def cumulate(gate):
        logf = _log_sigmoid(gate + bf_ref[...]) * LOG2E
        c = carry_ref[...] + _cumsum_rows(logf, tri_ref[...])
        tm = c.shape[0]
        carry_ref[...] = c[tm - 1:tm, :]
        return c

    def extend(c):
        terms = jnp.concatenate(_split3(c), axis=1)
        qx_ref[...] = (_dot(terms, eq_ref[...]) + oq_ref[...]).astype(BF16)
        kx_ref[...] = (ok_ref[...] - _dot(terms, ek_ref[...])).astype(BF16)

    _project(x_ref, g_ref, w_ref, wg_ref, cs_ref, o_ref, wb_scr, n_chunk, [cumulate, extend])


def _inproj_gla_kernel(x_ref, g_ref, w_ref, wg_ref, cs_ref, wgk_ref, bgk_ref, tri_ref,
                       o_ref, bc_ref, wb_scr, *, n_chunk):
    C = GLA_FAST_CHUNK

    def decay(gl):
        x = _dot(gl.astype(BF16), wgk_ref[...]) + bgk_ref[...]
        return _log_sigmoid(x) * (1.0 / GLA_GATE_NORM)

    def cumulate(c):
        def step(gk):
            bc_ref[c * C:(c + 1) * C, :] = _cumsum_rows(gk[c * C:(c + 1) * C, :], tri_ref[...])
            return gk
        return step

    n_row_chunks = x_ref.shape[0] // C
    _project(x_ref, g_ref, w_ref, wg_ref, cs_ref, o_ref, wb_scr, n_chunk,
             [decay] + [cumulate(c) for c in range(n_row_chunks)])


def _inproj_common(x, g, w_in, layer, n_gate, colscale, tm):
    w_gate = jnp.pad(w_in[layer, :, MAIN_W:], ((0, 0), (0, LANE - n_gate))).astype(BF16)
    args = [x, g.reshape(1, -1).astype(F32), jnp.swapaxes(w_in, 1, 2), w_gate, colscale]
    in_specs = [
        pl.BlockSpec((tm, D_MODEL), lambda i: (i, 0)),
        _const_spec((1, D_MODEL)),
        _layer_spec((MAIN_W, D_MODEL), layer),
        _const_spec((D_MODEL, LANE)),
        _const_spec((1, MAIN_W)),
    ]
    return args, in_specs


def _inproj_fox(x, g, w_in, layer, b_f, colscale, seq, tm=512):
    t = x.shape[0]
    args, in_specs = _inproj_common(x, g, w_in, layer, N_HEADS, colscale, tm)
    bf = jnp.pad(b_f.astype(F32).reshape(1, -1), ((0, 0), (0, LANE - N_HEADS)))
    r = np.arange(tm)
    tri = jnp.asarray(r[:, None] >= r[None, :], BF16)
    eq = np.zeros((3 * LANE, LANE), np.float32)
    ek = np.zeros((3 * LANE, LANE), np.float32)
    oq = np.zeros((1, LANE), np.float32)
    ok = np.zeros((1, LANE), np.float32)
    for h in range(N_HEADS):
        for j in range(3):
            eq[j * LANE + h, FOX_EXT_STRIDE * h + j] = 1.0
            ek[j * LANE + h, FOX_EXT_STRIDE * h + 3 + j] = 1.0
            oq[0, FOX_EXT_STRIDE * h + 3 + j] = 1.0
            ok[0, FOX_EXT_STRIDE * h + j] = 1.0
    args += [bf, tri, jnp.asarray(eq, BF16), jnp.asarray(ek, BF16), jnp.asarray(oq), jnp.asarray(ok)]
    in_specs += [_const_spec((1, LANE)), _const_spec((tm, tm)), _const_spec((3 * LANE, LANE)),
                 _const_spec((3 * LANE, LANE)), _const_spec((1, LANE)), _const_spec((1, LANE))]
    row_spec = lambda w: pl.BlockSpec((tm, w), lambda i: (i, 0))
    return pl.pallas_call(
        functools.partial(_inproj_fox_kernel, n_chunk=1024, steps_per_seq=seq // tm),
        grid=(t // tm,),
        in_specs=in_specs,
        out_specs=[row_spec(MAIN_W), row_spec(LANE), row_spec(LANE)],
        out_shape=[jax.ShapeDtypeStruct((t, MAIN_W), BF16), jax.ShapeDtypeStruct((t, LANE), BF16),
                   jax.ShapeDtypeStruct((t, LANE), BF16)],
        scratch_shapes=[pltpu.VMEM((D_MODEL, MAIN_W), BF16), pltpu.VMEM((1, LANE), F32)],
        compiler_params=_cparams(("arbitrary",)),
        name="inproj_fox",
    )(*args)


def _inproj_gla(x, g, w_in, layer, w_gk, b_gk, colscale, tm=1024):
    t = x.shape[0]
    args, in_specs = _inproj_common(x, g, w_in, layer, GLA_LOWRANK, colscale, tm)
    wgk = jnp.pad(w_gk, ((0, LANE - GLA_LOWRANK), (0, 0))).astype(BF16)
    r = np.arange(GLA_FAST_CHUNK)
    tri = jnp.asarray(r[:, None] >= r[None, :], BF16)
    args += [wgk, b_gk.reshape(1, -1).astype(F32), tri]
    in_specs += [_const_spec(wgk.shape), _const_spec((1, N_HEADS * HEAD_DIM)),
                 _const_spec((GLA_FAST_CHUNK, GLA_FAST_CHUNK))]
    row_spec = lambda w: pl.BlockSpec((tm, w), lambda i: (i, 0))
    return pl.pallas_call(
        functools.partial(_inproj_gla_kernel, n_chunk=1024),
        grid=(t // tm,),
        in_specs=in_specs,
        out_specs=[row_spec(MAIN_W), row_spec(N_HEADS * HEAD_DIM)],
        out_shape=[jax.ShapeDtypeStruct((t, MAIN_W), BF16),
                   jax.ShapeDtypeStruct((t, N_HEADS * HEAD_DIM), F32)],
        scratch_shapes=[pltpu.VMEM((D_MODEL, MAIN_W), BF16)],
        compiler_params=_cparams(("arbitrary",)),
        name="inproj_gla",
    )(*args)


def _flash_causal(i, t, u, qs, k_block, v_block, scr):
    n = len(qs)
    g = t // u
    per = g * n
    assert per % 2 == 0
    s_slot, p_slot, (acc, m_ref, alpha_ref) = scr[0:2], scr[2:4], scr[4:]

    def lanes(x, width):
        return jnp.concatenate([x] * (width // LANE), axis=1)

    def logits(a, j, slot, r0=0):
        s_slot[slot][r0:, :] = _dot_nt(qs[a][r0:], k_block(j))

    def pv(a, j, slot, r0=0):
        acc[a, r0:, :] = (lanes(alpha_ref[slot, r0:, :], 2 * HEAD_DIM) * acc[a, r0:, :]
                          + _dot(p_slot[slot][r0:, :], v_block(jnp.maximum(j, 0))))

    def softmax(a, slot, mask, r0=0):
        s_ref = s_slot[slot]
        if mask is None:
            load = lambda: s_ref[r0:, :]
        else:
            load = lambda: jnp.where(mask, s_ref[r0:, :], NEG_BIG)
        m_old = m_ref[a, r0:, :]
        m_new = jnp.maximum(m_old, jnp.max(load(), axis=-1, keepdims=True))
        m_ref[a, r0:, :] = m_new
        alpha_ref[slot, r0:, :] = jnp.exp2(m_old - m_new)
        p_slot[slot][r0:, :] = jnp.exp2(load() - lanes(m_new, u)).astype(BF16)

    def group(j0, tri, last):
        row0 = lambda e: 0 if tri is None or e < 0 else (e // n) * u
        for e in range(per):
            slot = e & 1
            if e + 1 < per:
                logits((e + 1) % n, j0 + (e + 1) // n, 1 - slot, row0(e + 1))
            elif not last:
                logits(0, j0 + g, 1 - slot)
            if e > 0:
                pv((e - 1) % n, j0 + (e - 1) // n, 1 - slot, row0(e - 1))
            else:
                pv(n - 1, j0 - 1, 1 - slot)
            softmax(e % n, slot, None if tri is None else tri[:t - row0(e)], row0(e))

    m_ref[...] = jnp.full(m_ref.shape, NEG_BIG, F32)
    alpha_ref[1] = jnp.ones(alpha_ref.shape[1:], F32)
    acc[...] = jnp.zeros(acc.shape, F32)
    p_slot[1][...] = jnp.zeros(p_slot[1].shape, BF16)
    logits(0, 0, 0)

    def body(k, _):
        group(k * g, None, False)
        return 0

    lax.fori_loop(0, i, body, 0)
    r = lax.broadcasted_iota(jnp.int32, (t, u), 0)
    c = lax.broadcasted_iota(jnp.int32, (t, u), 1)
    group(i * g, c <= r, True)
    pv(n - 1, i * g + g - 1, (per - 1) & 1, (g - 1) * u)
    return [acc[a] for a in range(n)]


def _flash_scratch(n, t, u):
    return [pltpu.VMEM((t, u), F32), pltpu.VMEM((t, u), F32),
            pltpu.VMEM((t, u), BF16), pltpu.VMEM((t, u), BF16),
            pltpu.VMEM((n, t, 2 * HEAD_DIM), F32),
            pltpu.VMEM((n, t, LANE), F32), pltpu.VMEM((2, t, LANE), F32)]


def _ones_block(u):
    return jnp.ones((u, LANE), BF16)


def _fox_attn_kernel(q_ref, qx_ref, k_ref, kx_ref, v_ref, o_ref, *scr, t, u):
    h, i = pl.program_id(1), pl.program_id(2)
    qx = qx_ref[0]
    lane = lax.broadcasted_iota(jnp.int32, qx.shape, 1)
    qx = jnp.where(lane // FOX_EXT_STRIDE == h, qx, jnp.zeros_like(qx))
    q = jnp.concatenate([q_ref[0], qx], axis=1)
    ones = _ones_block(u)

    def k_block(j):
        r = pl.multiple_of(j * u, u)
        return jnp.concatenate([k_ref[0, pl.ds(r, u), :], kx_ref[0, pl.ds(r, u), :]], axis=1)

    def v_block(j):
        return jnp.concatenate([v_ref[0, pl.ds(pl.multiple_of(j * u, u), u), :], ones], axis=1)

    acc, = _flash_causal(i, t, u, [q], k_block, v_block, scr)
    o_ref[0] = (acc[:, :HEAD_DIM] / acc[:, HEAD_DIM:]).astype(o_ref.dtype)


def _fox_attention(p3, qx, kx, t=2048, u=256):
    batch, seq, _ = p3.shape
    return pl.pallas_call(
        functools.partial(_fox_attn_kernel, t=t, u=u),
        grid=(batch, N_HEADS, seq // t),
        in_specs=[
            pl.BlockSpec((1, t, LANE), lambda b, h, i: (b, i, 12 + h)),
            pl.BlockSpec((1, t, LANE), lambda b, h, i: (b, i, 0)),
            pl.BlockSpec((1, seq, LANE), lambda b, h, i: (b, 0, 16 + h)),
            pl.BlockSpec((1, seq, LANE), lambda b, h, i: (b, 0, 0)),
            pl.BlockSpec((1, seq, LANE), lambda b, h, i: (b, 0, 20 + h)),
        ],
        out_specs=pl.BlockSpec((1, t, LANE), lambda b, h, i: (b, i, h)),
        out_shape=jax.ShapeDtypeStruct((batch, seq, N_HEADS * HEAD_DIM), BF16),
        scratch_shapes=_flash_scratch(1, t, u),
        compiler_params=_cparams(("arbitrary", "arbitrary", "arbitrary")),
        name="fox_attn",
    )(p3, qx, p3, kx, p3)


def _diff_attn_kernel(q_ref, k_ref, v_ref, lq1_ref, lk1_ref, lq2_ref, lk2_ref, gs_ref, o_ref,
                      *scr, t, u, lam_init):
    i = pl.program_id(2)
    q = q_ref[0]
    lane = lax.broadcasted_iota(jnp.int32, q.shape, 1)
    zero = jnp.zeros_like(q)
    q1 = jnp.where(lane < DIFF_HALF, q, zero)
    q2 = jnp.where(lane >= DIFF_HALF, q, zero)
    ones = _ones_block(u)
    k_block = lambda j: k_ref[0, pl.ds(pl.multiple_of(j * u, u), u), :]
    v_block = lambda j: jnp.concatenate(
        [v_ref[0, pl.ds(pl.multiple_of(j * u, u), u), :], ones], axis=1)
    a1, a2 = _flash_causal(i, t, u, [q1, q2], k_block, v_block, scr)
    lam = (jnp.exp(jnp.sum(lq1_ref[...] * lk1_ref[...], axis=-1, keepdims=True))
           - jnp.exp(jnp.sum(lq2_ref[...] * lk2_ref[...], axis=-1, keepdims=True)) + lam_init)
    o = a1[:, :HEAD_DIM] / a1[:, HEAD_DIM:] - lam * (a2[:, :HEAD_DIM] / a2[:, HEAD_DIM:])
    o_ref[0] = (_rms(o, gs_ref[...]) * (1.0 - lam_init)).astype(o_ref.dtype)


def _diff_attention(p3, lq1, lk1, lq2, lk2, g_sub, lam_init, t=2048, u=256):
    batch, seq, _ = p3.shape
    vec = lambda a: a.reshape(1, -1).astype(F32)
    return pl.pallas_call(
        functools.partial(_diff_attn_kernel, t=t, u=u, lam_init=lam_init),
        grid=(batch, N_HEADS, seq // t),
        in_specs=[
            pl.BlockSpec((1, t, LANE), lambda b, h, i: (b, i, h)),
            pl.BlockSpec((1, seq, LANE), lambda b, h, i: (b, 0, 4 + h)),
            pl.BlockSpec((1, seq, LANE), lambda b, h, i: (b, 0, 8 + h)),
            _const_spec((1, DIFF_HALF)), _const_spec((1, DIFF_HALF)),
            _const_spec((1, DIFF_HALF)), _const_spec((1, DIFF_HALF)),
            _const_spec((1, HEAD_DIM)),
        ],
        out_specs=pl.BlockSpec((1, t, LANE), lambda b, h, i: (b, i, h)),
        out_shape=jax.ShapeDtypeStruct((batch, seq, N_HEADS * HEAD_DIM), BF16),
        scratch_shapes=_flash_scratch(2, t, u),
        compiler_params=_cparams(("arbitrary", "arbitrary", "arbitrary")),
        name="diff_attn",
    )(p3, p3, p3, vec(lq1), vec(lk1), vec(lq2), vec(lk2), vec(g_sub))


def _gla_kernel(q_ref, k_ref, v_ref, g_ref, bc_ref, bc_next_ref, on_ref, o_ref, st_ref, low_ref,
                *, tc):
    i = pl.program_id(2)

    @pl.when(i == 0)
    def _():
        st_ref[...] = jnp.zeros_like(st_ref)
        low_ref[0] = jnp.min(bc_ref[0])

    C = GLA_FAST_CHUNK

    def finish(o, rows):
        y = _rms(o, on_ref[...])
        g = g_ref[0, rows, :].astype(F32)
        o_ref[0, rows, :] = (y * (g * jax.nn.sigmoid(g))).astype(o_ref.dtype)

    def update_state(st, k, v, b, n):
        b_last = b[n - 1:n, :]
        k_dec = (k * jnp.exp(b_last - b)).astype(BF16)
        vt_k = lax.dot_general(v, k_dec, (((0,), (0,)), ((), ())), preferred_element_type=F32)
        st_ref[...] = st * jnp.exp(b_last) + vt_k

    safe = low_ref[0] >= -GLA_SAFE_DECAY

    def publish_next_low():
        low_ref[0] = jnp.min(bc_next_ref[0])

    @pl.when(safe)
    def _():
        publish_next_low()
        r = lax.broadcasted_iota(jnp.int32, (C, C), 0)
        cc = lax.broadcasted_iota(jnp.int32, (C, C), 1)
        causal = cc <= r
        for c in range(tc // C):
            rows = slice(c * C, (c + 1) * C)
            q = q_ref[0, rows, :].astype(F32)
            k = k_ref[0, rows, :].astype(F32)
            v = v_ref[0, rows, :]
            b = bc_ref[0, rows, :]
            st = st_ref[...]
            qe = (q * jnp.exp(b)).astype(BF16)
            ke = (k * jnp.exp(-b)).astype(BF16)
            o = _dot_nt(qe, st.astype(BF16))
            a = jnp.where(causal, _dot_nt(qe, ke), 0.0)
            o = o + _dot(a.astype(BF16), v)
            update_state(st, k, v, b, C)
            finish(o, rows)

    @pl.when(jnp.logical_not(safe))
    def _():
        publish_next_low()
        L, SB = GLA_CHUNK, GLA_SUB
        row = lax.broadcasted_iota(jnp.int32, (L, HEAD_DIM), 0)
        sub_row = lax.broadcasted_iota(jnp.int32, (SB, 1), 0)

        def chunk(c, _):
            r0 = pl.multiple_of(c * L, L)
            rows = pl.ds(r0, L)
            q = q_ref[0, rows, :].astype(F32)
            k = k_ref[0, rows, :].astype(F32)
            v = v_ref[0, rows, :]
            vf = v.astype(F32)
            before = bc_ref[0, pl.ds(jnp.maximum(r0 - 1, 0), 1), :]
            b = bc_ref[0, rows, :] - jnp.where(r0 % C == 0, 0.0, before)
            st = st_ref[...]

            o = _dot_nt((q * jnp.exp(b)).astype(BF16), st.astype(BF16))

            qs, ks = [], []
            for s in range(1, L // SB):
                ref = b[s * SB - 1:s * SB, :]
                in_sub = (row >= s * SB) & (row < (s + 1) * SB)
                qs.append(jnp.where(in_sub, q * jnp.exp(jnp.minimum(b - ref, 0.0)), 0.0))
                ks.append(jnp.where(row < s * SB, k * jnp.exp(jnp.minimum(ref - b, 0.0)), 0.0))
            a = _dot_nt(jnp.concatenate(qs, axis=1).astype(BF16),
                        jnp.concatenate(ks, axis=1).astype(BF16))
            o = o + _dot(a.astype(BF16), v)

            diag = []
            for s in range(L // SB):
                sl = slice(s * SB, (s + 1) * SB)
                qd, kd, bd, vd = q[sl], k[sl], b[sl], vf[sl]
                acc = jnp.zeros((SB, GLA_DV), F32)
                for j in range(SB):
                    w = qd * kd[j:j + 1] * jnp.exp(jnp.minimum(bd - bd[j:j + 1], 0.0))
                    d = jnp.sum(w, axis=-1, keepdims=True)
                    acc = acc + jnp.where(sub_row >= j, d, 0.0) * vd[j:j + 1]
                diag.append(acc)
            o = o + jnp.concatenate(diag, axis=0)

            update_state(st, k, v, b, L)
            finish(o, rows)
            return 0

        lax.fori_loop(0, tc // L, chunk, 0)


def _gla(p3, bc3, g_onorm, tc=2048):
    batch, seq, _ = p3.shape
    last = seq // tc - 1
    return pl.pallas_call(
        functools.partial(_gla_kernel, tc=tc),
        grid=(batch, N_HEADS, seq // tc),
        in_specs=[
            pl.BlockSpec((1, tc, HEAD_DIM), lambda b, h, i: (b, i, h)),
            pl.BlockSpec((1, tc, HEAD_DIM), lambda b, h, i: (b, i, 4 + h)),
            pl.BlockSpec((1, tc, GLA_DV), lambda b, h, i: (b, i, 4 + h)),
            pl.BlockSpec((1, tc, GLA_DV), lambda b, h, i: (b, i, 8 + h)),
            pl.BlockSpec((1, tc, HEAD_DIM), lambda b, h, i: (b, i, h)),
            pl.BlockSpec((1, tc, HEAD_DIM), lambda b, h, i: (b, jnp.minimum(i + 1, last), h)),
            _const_spec((1, GLA_DV)),
        ],
        out_specs=pl.BlockSpec((1, tc, GLA_DV), lambda b, h, i: (b, i, h)),
        out_shape=jax.ShapeDtypeStruct((batch, seq, N_HEADS * GLA_DV), BF16),
        scratch_shapes=[pltpu.VMEM((GLA_DV, HEAD_DIM), F32), pltpu.SMEM((1,), F32)],
        compiler_params=_cparams(("arbitrary", "arbitrary", "arbitrary")),
        name="gla",
    )(p3, p3, p3, p3, bc3, bc3, g_onorm.reshape(1, -1).astype(F32))


def _tail_kernel(*refs, mix_widths, final_norm):
    n_mix = len(mix_widths)
    h_ref = refs[0]
    o_refs = refs[1:1 + n_mix]
    wo_ref, gm_ref, wu_ref, wd_ref, gp_ref, wg_ref, p_ref, wp_ref = refs[1 + n_mix:9 + n_mix]
    gf_ref = refs[9 + n_mix] if final_norm else None
    out_ref = refs[-1]

    mixed, c0 = None, 0
    for o_ref, w in zip(o_refs, mix_widths):
        term = _dot(o_ref[...], wo_ref[c0:c0 + w, :])
        mixed = term if mixed is None else mixed + term
        c0 += w
    h = h_ref[...] + mixed
    wide = lambda s: jnp.concatenate([s] * (D_MODEL // LANE), axis=1)
    hg, r = _rms_factors(h, gm_ref[...])
    mlp = None
    for c0 in range(0, D_FF, FF_CHUNK):
        u = jnp.square(jnp.maximum(_dot(hg, wu_ref[:, c0:c0 + FF_CHUNK]), 0.0))
        term = _dot(u.astype(BF16), wd_ref[c0:c0 + FF_CHUNK, :])
        mlp = term if mlp is None else mlp + term
    h = h + wide(r * r) * mlp
    hg, r = _rms_factors(h, gp_ref[...])
    gate = jax.nn.sigmoid(wide(r) * _dot(hg, wg_ref[...]))
    h = h + gate * _dot(p_ref[...].astype(BF16), wp_ref[...])
    if final_norm:
        h = _rms(h, gf_ref[...])
    out_ref[...] = h


def _tail(h, mix, w_o, w_o_layer, g_mlp, w_up, w_down, g_ple, w_gate, p_all, w_proj, layer, g_final,
          tm=512):
    t = h.shape[0]
    row = lambda a: a.reshape(1, -1).astype(F32)
    args = [h, *mix, w_o, row(g_mlp), w_up, w_down, row(g_ple), w_gate, p_all, w_proj]
    in_specs = [pl.BlockSpec((tm, D_MODEL), lambda i: (i, 0))]
    in_specs += [pl.BlockSpec((tm, m.shape[1]), lambda i: (i, 0)) for m in mix]
    in_specs += [_layer_spec(w_o.shape[1:], w_o_layer), _const_spec((1, D_MODEL)),
                 _layer_spec(w_up.shape[1:], layer), _layer_spec(w_down.shape[1:], layer),
                 _const_spec((1, D_MODEL)), _layer_spec(w_gate.shape[1:], layer),
                 pl.BlockSpec((None, tm, PLE_DIM), lambda i: (layer, i, 0)),
                 _layer_spec(w_proj.shape[1:], layer)]
    if g_final is not None:
        args.append(row(g_final))
        in_specs.append(_const_spec((1, D_MODEL)))
    return pl.pallas_call(
        functools.partial(_tail_kernel, mix_widths=tuple(m.shape[1] for m in mix),
                          final_norm=g_final is not None),
        grid=(t // tm,),
        in_specs=in_specs,
        out_specs=pl.BlockSpec((tm, D_MODEL), lambda i: (i, 0)),
        out_shape=jax.ShapeDtypeStruct((t, D_MODEL), F32),
        compiler_params=_cparams(("arbitrary",)),
        name="tail",
    )(*args)


def _col_scale(scales):
    cs = np.ones((1, MAIN_W), np.float32)
    for (c0, c1), v in scales.items():
        cs[0, c0:c1] = v
    return jnp.asarray(cs)


def kernel(x, p, norm_mix, norm_mlp, norm_ple, norm_final, even_w_in, even_b_f, lam_q1, lam_k1, lam_q2, lam_k2, diff_subln, even_w_o, odd_w_in, odd_w_gk, odd_b_gk, gla_onorm, odd_w_o, w_up, w_down, w_ple_gate, w_ple_proj):
    batch, seq, _ = x.shape
    depth = p.shape[0]
    t = batch * seq
    h = x.reshape(t, D_MODEL)
    p_all = p.reshape(depth, t, PLE_DIM)
    even_w_o, odd_w_o, w_up, w_down, w_ple_gate, w_ple_proj = (
        w.astype(BF16) for w in (even_w_o, odd_w_o, w_up, w_down, w_ple_gate, w_ple_proj))
    for i in range(depth):
        j = i // 2
        if i % 2 == 0:
            cs = _col_scale({(0, 512): DIFF_HALF ** -0.5 * LOG2E,
                             (1536, 2048): HEAD_DIM ** -0.5 * LOG2E})
            proj, qx, kx = _inproj_fox(h, norm_mix[i], even_w_in, j, even_b_f[j], cs, seq)
            p3 = proj.reshape(batch, seq, MAIN_W)
            lam_init = 0.8 - 0.6 * math.exp(-0.3 * i)
            oa = _diff_attention(p3, lam_q1[j], lam_k1[j], lam_q2[j], lam_k2[j],
                                 diff_subln[j], lam_init)
            ob = _fox_attention(p3, qx.reshape(batch, seq, LANE), kx.reshape(batch, seq, LANE))
            mix = [oa.reshape(t, -1), ob.reshape(t, -1)]
            w_o = even_w_o
        else:
            cs = _col_scale({(0, 512): HEAD_DIM ** -0.5})
            proj, bc = _inproj_gla(h, norm_mix[i], odd_w_in, j, odd_w_gk[j], odd_b_gk[j], cs)
            o = _gla(proj.reshape(batch, seq, MAIN_W), bc.reshape(batch, seq, -1), gla_onorm[j])
            mix = [o.reshape(t, -1)]
            w_o = odd_w_o
        h = _tail(h, mix, w_o, j, norm_mlp[i], w_up, w_down, norm_ple[i], w_ple_gate, p_all,
                  w_ple_proj, i, norm_final if i == depth - 1 else None)
    return h.reshape(batch, seq, D_MODEL)
```

```python
import functools
import math

import jax
import jax.numpy as jnp
import numpy as np
from jax import lax
from jax.experimental import pallas as pl
from jax.experimental.pallas import tpu as pltpu

F32 = jnp.float32
BF16 = jnp.bfloat16

D_MODEL = 1024
PLE_DIM = 256
N_HEADS = 4
HEAD_DIM = 128
DIFF_HALF = 64
GLA_DV = 256
GLA_LOWRANK = 16
GLA_GATE_NORM = 16.0
GLA_CHUNK = 64
GLA_SUB = 16
GLA_FAST_CHUNK = 128
GLA_SAFE_DECAY = 60.0
FOX_EXT_STRIDE = 8
D_FF = 4 * D_MODEL
FF_CHUNK = 2048
EPS = 1e-6
LOG2E = 1.4426950408889634
NEG_BIG = -1e30
LANE = 128
MAIN_W = 3072

VMEM_LIMIT = 56 * 1024 * 1024


def _cparams(sem):
    return pltpu.CompilerParams(dimension_semantics=sem, vmem_limit_bytes=VMEM_LIMIT)


def _const_spec(shape):
    nd = len(shape)
    return pl.BlockSpec(shape, lambda *_: (0,) * nd, pipeline_mode=pl.Buffered(1))


def _layer_spec(shape, layer):
    nd = len(shape)
    return pl.BlockSpec((None,) + tuple(shape), lambda *_: (layer,) + (0,) * nd,
                        pipeline_mode=pl.Buffered(1))


def _rms(x, g):
    return x * lax.rsqrt(jnp.mean(x * x, axis=-1, keepdims=True) + EPS) * g


def _rms_factors(x, g):
    scale = lax.rsqrt(jnp.mean(x * x, axis=-1, keepdims=True) + EPS)
    return (x * g).astype(BF16), jnp.broadcast_to(scale, (x.shape[0], LANE))


def _log_sigmoid(x):
    return jnp.minimum(x, 0.0) - jnp.log(1.0 + jnp.exp(-jnp.abs(x)))


def _split3(x):
    hi = x.astype(BF16)
    r1 = x - hi.astype(F32)
    mid = r1.astype(BF16)
    lo = (r1 - mid.astype(F32)).astype(BF16)
    return hi, mid, lo


def _dot(a, b):
    return jnp.dot(a, b, preferred_element_type=F32)


def _cumsum_rows(x, tri):
    w = x.shape[1]
    y = _dot(tri, jnp.concatenate(_split3(x), axis=1))
    return y[:, :w] + y[:, w:2 * w] + y[:, 2 * w:]


def _dot_nt(a, b):
    return lax.dot_general(a, b, (((1,), (1,)), ((), ())), preferred_element_type=F32)


N_CAST = 4


def _project(x_ref, g_ref, w_ref, wg_ref, cs_ref, o_ref, wb_scr, n_chunk, gate_steps, casts):
    @pl.when(pl.program_id(0) == 0)
    def _():
        for c0 in range(0, MAIN_W, n_chunk):
            wb_scr[:, c0:c0 + n_chunk] = w_ref[c0:c0 + n_chunk, :].T.astype(BF16)

    xg, scale = _rms_factors(x_ref[...], g_ref[...])
    value = _dot(xg, wg_ref[...]) * scale
    scale_n = jnp.concatenate([scale] * (n_chunk // LANE), axis=1)
    steps = list(gate_steps)
    for c0 in range(0, MAIN_W, n_chunk):
        y = _dot(xg, wb_scr[:, c0:c0 + n_chunk]) * cs_ref[:, c0:c0 + n_chunk] * scale_n
        o_ref[:, c0:c0 + n_chunk] = y.astype(o_ref.dtype)
        if steps:
            value = steps.pop(0)(value)
    for step in steps:
        value = step(value)
    for src_ref, dst_ref in casts:
        dst_ref[...] = src_ref[...].astype(BF16)


def _inproj_fox_kernel(x_ref, g_ref, w_ref, wg_ref, cs_ref, bf_ref, tri_ref, eq_ref, ek_ref,
                       oq_ref, ok_ref, *rest, n_chunk, steps_per_seq):
    cast_in, (o_ref, qx_ref, kx_ref), cast_out = rest[:N_CAST], rest[N_CAST:N_CAST + 3], rest[N_CAST + 3:2 * N_CAST + 3]
    wb_scr, carry_ref = rest[2 * N_CAST + 3:]
    @pl.when(pl.program_id(0) % steps_per_seq == 0)
    def _():
        carry_ref[...] = jnp.zeros_like(carry_ref)

    def cumulate(gate):
        logf = _log_sigmoid(gate + bf_ref[...]) * LOG2E
        c = carry_ref[...] + _cumsum_rows(logf, tri_ref[...])
        tm = c.shape[0]
        carry_ref[...] = c[tm - 1:tm, :]
        return c

    def extend(c):
        terms = jnp.concatenate(_split3(c), axis=1)
        qx_ref[...] = (_dot(terms, eq_ref[...]) + oq_ref[...]).astype(BF16)
        kx_ref[...] = (ok_ref[...] - _dot(terms, ek_ref[...])).astype(BF16)

    _project(x_ref, g_ref, w_ref, wg_ref, cs_ref, o_ref, wb_scr, n_chunk, [cumulate, extend],
             list(zip(cast_in, cast_out)))


def _inproj_gla_kernel(x_ref, g_ref, w_ref, wg_ref, cs_ref, wgk_ref, bgk_ref, tri_ref,
                       *rest, n_chunk):
    cast_in, (o_ref, bc_ref), cast_out = rest[:N_CAST], rest[N_CAST:N_CAST + 2], rest[N_CAST + 2:2 * N_CAST + 2]
    wb_scr, = rest[2 * N_CAST + 2:]
    C = GLA_FAST_CHUNK

    def decay(gl):
        x = _dot(gl.astype(BF16), wgk_ref[...]) + bgk_ref[...]
        return _log_sigmoid(x) * (1.0 / GLA_GATE_NORM)

    def cumulate(c):
        def step(gk):
            bc_ref[c * C:(c + 1) * C, :] = _cumsum_rows(gk[c * C:(c + 1) * C, :], tri_ref[...])
            return gk
        return step

    n_row_chunks = x_ref.shape[0] // C
    _project(x_ref, g_ref, w_ref, wg_ref, cs_ref, o_ref, wb_scr, n_chunk,
             [decay] + [cumulate(c) for c in range(n_row_chunks)], list(zip(cast_in, cast_out)))


def _inproj_common(x, g, w_in, layer, n_gate, colscale, tm):
    w_gate = jnp.pad(w_in[layer, :, MAIN_W:], ((0, 0), (0, LANE - n_gate))).astype(BF16)
    args = [x, g.reshape(1, -1).astype(F32), jnp.swapaxes(w_in, 1, 2), w_gate, colscale]
    in_specs = [
        pl.BlockSpec((tm, D_MODEL), lambda i: (i, 0)),
        _const_spec((1, D_MODEL)),
        _layer_spec((MAIN_W, D_MODEL), layer),
        _const_spec((D_MODEL, LANE)),
        _const_spec((1, MAIN_W)),
    ]
    return args, in_specs


def _cast_specs(tail_weights, steps):
    args, in_specs, out_specs, out_shapes = [], [], [], []
    for k, (w, layer) in enumerate(tail_weights):
        _, rows, cols = w.shape
        if k == 0:
            blk = (rows, cols // steps)
            in_specs.append(pl.BlockSpec((None,) + blk, lambda i, layer=layer: (layer, 0, i)))
            out_specs.append(pl.BlockSpec(blk, lambda i: (0, i)))
        else:
            blk = (rows // steps, cols)
            in_specs.append(pl.BlockSpec((None,) + blk, lambda i, layer=layer: (layer, i, 0)))
            out_specs.append(pl.BlockSpec(blk, lambda i: (i, 0)))
        args.append(w)
        out_shapes.append(jax.ShapeDtypeStruct((rows, cols), BF16))
    return args, in_specs, out_specs, out_shapes


def _inproj_fox(x, g, w_in, layer, b_f, colscale, seq, tail_weights, tm=512):
    t = x.shape[0]
    args, in_specs = _inproj_common(x, g, w_in, layer, N_HEADS, colscale, tm)
    c_args, c_in, c_out, c_shapes = _cast_specs(tail_weights, t // tm)
    bf = jnp.pad(b_f.astype(F32).reshape(1, -1), ((0, 0), (0, LANE - N_HEADS)))
    r = np.arange(tm)
    tri = jnp.asarray(r[:, None] >= r[None, :], BF16)
    eq = np.zeros((3 * LANE, LANE), np.float32)
    ek = np.zeros((3 * LANE, LANE), np.float32)
    oq = np.zeros((1, LANE), np.float32)
    ok = np.zeros((1, LANE), np.float32)
    for h in range(N_HEADS):
        for j in range(3):
            eq[j * LANE + h, FOX_EXT_STRIDE * h + j] = 1.0
            ek[j * LANE + h, FOX_EXT_STRIDE * h + 3 + j] = 1.0
            oq[0, FOX_EXT_STRIDE * h + 3 + j] = 1.0
            ok[0, FOX_EXT_STRIDE * h + j] = 1.0
    args += [bf, tri, jnp.asarray(eq, BF16), jnp.asarray(ek, BF16), jnp.asarray(oq), jnp.asarray(ok)]
    args += c_args
    in_specs += [_const_spec((1, LANE)), _const_spec((tm, tm)), _const_spec((3 * LANE, LANE)),
                 _const_spec((3 * LANE, LANE)), _const_spec((1, LANE)), _const_spec((1, LANE))]
    in_specs += c_in
    row_spec = lambda w: pl.BlockSpec((tm, w), lambda i: (i, 0))
    outs = pl.pallas_call(
        functools.partial(_inproj_fox_kernel, n_chunk=1024, steps_per_seq=seq // tm),
        grid=(t // tm,),
        in_specs=in_specs,
        out_specs=[row_spec(MAIN_W), row_spec(LANE), row_spec(LANE)] + c_out,
        out_shape=[jax.ShapeDtypeStruct((t, MAIN_W), BF16), jax.ShapeDtypeStruct((t, LANE), BF16),
                   jax.ShapeDtypeStruct((t, LANE), BF16)] + c_shapes,
        scratch_shapes=[pltpu.VMEM((D_MODEL, MAIN_W), BF16), pltpu.VMEM((1, LANE), F32)],
        compiler_params=_cparams(("arbitrary",)),
        name="inproj_fox",
    )(*args)
    return outs[:3], outs[3:]


def _inproj_gla(x, g, w_in, layer, w_gk, b_gk, colscale, tail_weights, tm=512):
    t = x.shape[0]
    args, in_specs = _inproj_common(x, g, w_in, layer, GLA_LOWRANK, colscale, tm)
    c_args, c_in, c_out, c_shapes = _cast_specs(tail_weights, t // tm)
    wgk = jnp.pad(w_gk, ((0, LANE - GLA_LOWRANK), (0, 0))).astype(BF16)
    r = np.arange(GLA_FAST_CHUNK)
    tri = jnp.asarray(r[:, None] >= r[None, :], BF16)
    args += [wgk, b_gk.reshape(1, -1).astype(F32), tri] + c_args
    in_specs += [_const_spec(wgk.shape), _const_spec((1, N_HEADS * HEAD_DIM)),
                 _const_spec((GLA_FAST_CHUNK, GLA_FAST_CHUNK))] + c_in
    row_spec = lambda w: pl.BlockSpec((tm, w), lambda i: (i, 0))
    outs = pl.pallas_call(
        functools.partial(_inproj_gla_kernel, n_chunk=1024),
        grid=(t // tm,),
        in_specs=in_specs,
        out_specs=[row_spec(MAIN_W), row_spec(N_HEADS * HEAD_DIM)] + c_out,
        out_shape=[jax.ShapeDtypeStruct((t, MAIN_W), BF16),
                   jax.ShapeDtypeStruct((t, N_HEADS * HEAD_DIM), F32)] + c_shapes,
        scratch_shapes=[pltpu.VMEM((D_MODEL, MAIN_W), BF16)],
        compiler_params=_cparams(("arbitrary",)),
        name="inproj_gla",
    )(*args)
    return outs[:2], outs[2:]


def _flash_causal(i, t, u, qs, k_block, v_block, scr):
    n = len(qs)
    g = t // u
    per = g * n
    assert per % 2 == 0
    s_slot, p_slot, (acc, m_ref, alpha_ref) = scr[0:2], scr[2:4], scr[4:]

    def lanes(x, width):
        return jnp.concatenate([x] * (width // LANE), axis=1)

    def logits(a, j, slot, r0=0):
        s_slot[slot][r0:, :] = _dot_nt(qs[a][r0:], k_block(j))

    def pv(a, j, slot, r0=0):
        acc[a, r0:, :] = (lanes(alpha_ref[slot, r0:, :], 2 * HEAD_DIM) * acc[a, r0:, :]
                          + _dot(p_slot[slot][r0:, :], v_block(jnp.maximum(j, 0))))

    def softmax(a, slot, mask, r0=0):
        s_ref = s_slot[slot]
        if mask is None:
            load = lambda: s_ref[r0:, :]
        else:
            load = lambda: jnp.where(mask, s_ref[r0:, :], NEG_BIG)
        m_old = m_ref[a, r0:, :]
        m_new = jnp.maximum(m_old, jnp.max(load(), axis=-1, keepdims=True))
        m_ref[a, r0:, :] = m_new
        alpha_ref[slot, r0:, :] = jnp.exp2(m_old - m_new)
        p_slot[slot][r0:, :] = jnp.exp2(load() - lanes(m_new, u)).astype(BF16)

    def group(j0, tri, last):
        row0 = lambda e: 0 if tri is None or e < 0 else (e // n) * u
        for e in range(per):
            slot = e & 1
            if e + 1 < per:
                logits((e + 1) % n, j0 + (e + 1) // n, 1 - slot, row0(e + 1))
            elif not last:
                logits(0, j0 + g, 1 - slot)
            if e > 0:
                pv((e - 1) % n, j0 + (e - 1) // n, 1 - slot, row0(e - 1))
            else:
                pv(n - 1, j0 - 1, 1 - slot)
            softmax(e % n, slot, None if tri is None else tri[:t - row0(e)], row0(e))

    m_ref[...] = jnp.full(m_ref.shape, NEG_BIG, F32)
    alpha_ref[1] = jnp.ones(alpha_ref.shape[1:], F32)
    acc[...] = jnp.zeros(acc.shape, F32)
    p_slot[1][...] = jnp.zeros(p_slot[1].shape, BF16)
    logits(0, 0, 0)

    def body(k, _):
        group(k * g, None, False)
        return 0

    lax.fori_loop(0, i, body, 0)
    r = lax.broadcasted_iota(jnp.int32, (t, u), 0)
    c = lax.broadcasted_iota(jnp.int32, (t, u), 1)
    group(i * g, c <= r, True)
    pv(n - 1, i * g + g - 1, (per - 1) & 1, (g - 1) * u)
    return [acc[a] for a in range(n)]


def _flash_scratch(n, t, u):
    return [pltpu.VMEM((t, u), F32), pltpu.VMEM((t, u), F32),
            pltpu.VMEM((t, u), BF16), pltpu.VMEM((t, u), BF16),
            pltpu.VMEM((n, t, 2 * HEAD_DIM), F32),
            pltpu.VMEM((n, t, LANE), F32), pltpu.VMEM((2, t, LANE), F32)]


def _ones_block(u):
    return jnp.ones((u, LANE), BF16)


def _fox_attn_kernel(q_ref, qx_ref, k_ref, kx_ref, v_ref, o_ref, *scr, t, u):
    h, i = pl.program_id(1), pl.program_id(2)
    qx = qx_ref[0]
    lane = lax.broadcasted_iota(jnp.int32, qx.shape, 1)
    qx = jnp.where(lane // FOX_EXT_STRIDE == h, qx, jnp.zeros_like(qx))
    q = jnp.concatenate([q_ref[0], qx], axis=1)
    ones = _ones_block(u)

    def k_block(j):
        r = pl.multiple_of(j * u, u)
        return jnp.concatenate([k_ref[0, pl.ds(r, u), :], kx_ref[0, pl.ds(r, u), :]], axis=1)

    def v_block(j):
        return jnp.concatenate([v_ref[0, pl.ds(pl.multiple_of(j * u, u), u), :], ones], axis=1)

    acc, = _flash_causal(i, t, u, [q], k_block, v_block, scr)
    o_ref[0] = (acc[:, :HEAD_DIM] / acc[:, HEAD_DIM:]).astype(o_ref.dtype)


def _fox_attention(p3, qx, kx, t=2048, u=256):
    batch, seq, _ = p3.shape
    return pl.pallas_call(
        functools.partial(_fox_attn_kernel, t=t, u=u),
        grid=(batch, N_HEADS, seq // t),
        in_specs=[
            pl.BlockSpec((1, t, LANE), lambda b, h, i: (b, i, 12 + h)),
            pl.BlockSpec((1, t, LANE), lambda b, h, i: (b, i, 0)),
            pl.BlockSpec((1, seq, LANE), lambda b, h, i: (b, 0, 16 + h)),
            pl.BlockSpec((1, seq, LANE), lambda b, h, i: (b, 0, 0)),
            pl.BlockSpec((1, seq, LANE), lambda b, h, i: (b, 0, 20 + h)),
        ],
        out_specs=pl.BlockSpec((1, t, LANE), lambda b, h, i: (b, i, h)),
        out_shape=jax.ShapeDtypeStruct((batch, seq, N_HEADS * HEAD_DIM), BF16),
        scratch_shapes=_flash_scratch(1, t, u),
        compiler_params=_cparams(("arbitrary", "arbitrary", "arbitrary")),
        name="fox_attn",
    )(p3, qx, p3, kx, p3)


def _diff_attn_kernel(q_ref, k_ref, v_ref, lq1_ref, lk1_ref, lq2_ref, lk2_ref, gs_ref, o_ref,
                      *scr, t, u, lam_init):
    i = pl.program_id(2)
    q = q_ref[0]
    lane = lax.broadcasted_iota(jnp.int32, q.shape, 1)
    zero = jnp.zeros_like(q)
    q1 = jnp.where(lane < DIFF_HALF, q, zero)
    q2 = jnp.where(lane >= DIFF_HALF, q, zero)
    ones = _ones_block(u)
    k_block = lambda j: k_ref[0, pl.ds(pl.multiple_of(j * u, u), u), :]
    v_block = lambda j: jnp.concatenate(
        [v_ref[0, pl.ds(pl.multiple_of(j * u, u), u), :], ones], axis=1)
    a1, a2 = _flash_causal(i, t, u, [q1, q2], k_block, v_block, scr)
    lam = (jnp.exp(jnp.sum(lq1_ref[...] * lk1_ref[...], axis=-1, keepdims=True))
           - jnp.exp(jnp.sum(lq2_ref[...] * lk2_ref[...], axis=-1, keepdims=True)) + lam_init)
    o = a1[:, :HEAD_DIM] / a1[:, HEAD_DIM:] - lam * (a2[:, :HEAD_DIM] / a2[:, HEAD_DIM:])
    o_ref[0] = (_rms(o, gs_ref[...]) * (1.0 - lam_init)).astype(o_ref.dtype)


def _diff_attention(p3, lq1, lk1, lq2, lk2, g_sub, lam_init, t=2048, u=256):
    batch, seq, _ = p3.shape
    vec = lambda a: a.reshape(1, -1).astype(F32)
    return pl.pallas_call(
        functools.partial(_diff_attn_kernel, t=t, u=u, lam_init=lam_init),
        grid=(batch, N_HEADS, seq // t),
        in_specs=[
            pl.BlockSpec((1, t, LANE), lambda b, h, i: (b, i, h)),
            pl.BlockSpec((1, seq, LANE), lambda b, h, i: (b, 0, 4 + h)),
            pl.BlockSpec((1, seq, LANE), lambda b, h, i: (b, 0, 8 + h)),
            _const_spec((1, DIFF_HALF)), _const_spec((1, DIFF_HALF)),
            _const_spec((1, DIFF_HALF)), _const_spec((1, DIFF_HALF)),
            _const_spec((1, HEAD_DIM)),
        ],
        out_specs=pl.BlockSpec((1, t, LANE), lambda b, h, i: (b, i, h)),
        out_shape=jax.ShapeDtypeStruct((batch, seq, N_HEADS * HEAD_DIM), BF16),
        scratch_shapes=_flash_scratch(2, t, u),
        compiler_params=_cparams(("arbitrary", "arbitrary", "arbitrary")),
        name="diff_attn",
    )(p3, p3, p3, vec(lq1), vec(lk1), vec(lq2), vec(lk2), vec(g_sub))


def _gla_kernel(q_ref, k_ref, v_ref, g_ref, bc_ref, bc_next_ref, on_ref, o_ref, st_ref, low_ref,
                *, tc):
    i = pl.program_id(2)

    @pl.when(i == 0)
    def _():
        st_ref[...] = jnp.zeros_like(st_ref)
        low_ref[0] = jnp.min(bc_ref[0])

    C = GLA_FAST_CHUNK

    def finish(o, rows):
        y = _rms(o, on_ref[...])
        g = g_ref[0, rows, :].astype(F32)
        o_ref[0, rows, :] = (y * (g * jax.nn.sigmoid(g))).astype(o_ref.dtype)

    def update_state(st, k, v, b, n):
        b_last = b[n - 1:n, :]
        k_dec = (k * jnp.exp(b_last - b)).astype(BF16)
        vt_k = lax.dot_general(v, k_dec, (((0,), (0,)), ((), ())), preferred_element_type=F32)
        st_ref[...] = st * jnp.exp(b_last) + vt_k

    safe = low_ref[0] >= -GLA_SAFE_DECAY

    def publish_next_low():
        low_ref[0] = jnp.min(bc_next_ref[0])

    @pl.when(safe)
    def _():
        publish_next_low()
        r = lax.broadcasted_iota(jnp.int32, (C, C), 0)
        cc = lax.broadcasted_iota(jnp.int32, (C, C), 1)
        causal = cc <= r
        for c in range(tc // C):
            rows = slice(c * C, (c + 1) * C)
            q = q_ref[0, rows, :].astype(F32)
            k = k_ref[0, rows, :].astype(F32)
            v = v_ref[0, rows, :]
            b = bc_ref[0, rows, :]
            st = st_ref[...]
            qe = (q * jnp.exp(b)).astype(BF16)
            ke = (k * jnp.exp(-b)).astype(BF16)
            o = _dot_nt(qe, st.astype(BF16))
            a = jnp.where(causal, _dot_nt(qe, ke), 0.0)
            o = o + _dot(a.astype(BF16), v)
            update_state(st, k, v, b, C)
            finish(o, rows)

    @pl.when(jnp.logical_not(safe))
    def _():
        publish_next_low()
        L, SB = GLA_CHUNK, GLA_SUB
        row = lax.broadcasted_iota(jnp.int32, (L, HEAD_DIM), 0)
        sub_row = lax.broadcasted_iota(jnp.int32, (SB, 1), 0)

        def chunk(c, _):
            r0 = pl.multiple_of(c * L, L)
            rows = pl.ds(r0, L)
            q = q_ref[0, rows, :].astype(F32)
            k = k_ref[0, rows, :].astype(F32)
            v = v_ref[0, rows, :]
            vf = v.astype(F32)
            before = bc_ref[0, pl.ds(jnp.maximum(r0 - 1, 0), 1), :]
            b = bc_ref[0, rows, :] - jnp.where(r0 % C == 0, 0.0, before)
            st = st_ref[...]

            o = _dot_nt((q * jnp.exp(b)).astype(BF16), st.astype(BF16))

            qs, ks = [], []
            for s in range(1, L // SB):
                ref = b[s * SB - 1:s * SB, :]
                in_sub = (row >= s * SB) & (row < (s + 1) * SB)
                qs.append(jnp.where(in_sub, q * jnp.exp(jnp.minimum(b - ref, 0.0)), 0.0))
                ks.append(jnp.where(row < s * SB, k * jnp.exp(jnp.minimum(ref - b, 0.0)), 0.0))
            a = _dot_nt(jnp.concatenate(qs, axis=1).astype(BF16),
                        jnp.concatenate(ks, axis=1).astype(BF16))
            o = o + _dot(a.astype(BF16), v)

            diag = []
            for s in range(L // SB):
                sl = slice(s * SB, (s + 1) * SB)
                qd, kd, bd, vd = q[sl], k[sl], b[sl], vf[sl]
                acc = jnp.zeros((SB, GLA_DV), F32)
                for j in range(SB):
                    w = qd * kd[j:j + 1] * jnp.exp(jnp.minimum(bd - bd[j:j + 1], 0.0))
                    d = jnp.sum(w, axis=-1, keepdims=True)
                    acc = acc + jnp.where(sub_row >= j, d, 0.0) * vd[j:j + 1]
                diag.append(acc)
            o = o + jnp.concatenate(diag, axis=0)

            update_state(st, k, v, b, L)
            finish(o, rows)
            return 0

        lax.fori_loop(0, tc // L, chunk, 0)


def _gla(p3, bc3, g_onorm, tc=2048):
    batch, seq, _ = p3.shape
    last = seq // tc - 1
    return pl.pallas_call(
        functools.partial(_gla_kernel, tc=tc),
        grid=(batch, N_HEADS, seq // tc),
        in_specs=[
            pl.BlockSpec((1, tc, HEAD_DIM), lambda b, h, i: (b, i, h)),
            pl.BlockSpec((1, tc, HEAD_DIM), lambda b, h, i: (b, i, 4 + h)),
            pl.BlockSpec((1, tc, GLA_DV), lambda b, h, i: (b, i, 4 + h)),
            pl.BlockSpec((1, tc, GLA_DV), lambda b, h, i: (b, i, 8 + h)),
            pl.BlockSpec((1, tc, HEAD_DIM), lambda b, h, i: (b, i, h)),
            pl.BlockSpec((1, tc, HEAD_DIM), lambda b, h, i: (b, jnp.minimum(i + 1, last), h)),
            _const_spec((1, GLA_DV)),
        ],
        out_specs=pl.BlockSpec((1, tc, GLA_DV), lambda b, h, i: (b, i, h)),
        out_shape=jax.ShapeDtypeStruct((batch, seq, N_HEADS * GLA_DV), BF16),
        scratch_shapes=[pltpu.VMEM((GLA_DV, HEAD_DIM), F32), pltpu.SMEM((1,), F32)],
        compiler_params=_cparams(("arbitrary", "arbitrary", "arbitrary")),
        name="gla",
    )(p3, p3, p3, p3, bc3, bc3, g_onorm.reshape(1, -1).astype(F32))


def _tail_kernel(*refs, mix_widths, final_norm):
    n_mix = len(mix_widths)
    h_ref = refs[0]
    o_refs = refs[1:1 + n_mix]
    wo_ref, gm_ref, wu_ref, wd_ref, gp_ref, wg_ref, p_ref, wp_ref = refs[1 + n_mix:9 + n_mix]
    gf_ref = refs[9 + n_mix] if final_norm else None
    out_ref = refs[-1]

    mixed, c0 = None, 0
    for o_ref, w in zip(o_refs, mix_widths):
        term = _dot(o_ref[...], wo_ref[c0:c0 + w, :])
        mixed = term if mixed is None else mixed + term
        c0 += w
    h = h_ref[...] + mixed
    wide = lambda s: jnp.concatenate([s] * (D_MODEL // LANE), axis=1)
    hg, r = _rms_factors(h, gm_ref[...])
    mlp = None
    for c0 in range(0, D_FF, FF_CHUNK):
        u = jnp.square(jnp.maximum(_dot(hg, wu_ref[:, c0:c0 + FF_CHUNK]), 0.0))
        term = _dot(u.astype(BF16), wd_ref[c0:c0 + FF_CHUNK, :])
        mlp = term if mlp is None else mlp + term
    h = h + wide(r * r) * mlp
    hg, r = _rms_factors(h, gp_ref[...])
    gate = jax.nn.sigmoid(wide(r) * _dot(hg, wg_ref[...]))
    h = h + gate * _dot(p_ref[...].astype(BF16), wp_ref[...])
    if final_norm:
        h = _rms(h, gf_ref[...])
    out_ref[...] = h


def _tail(h, mix, w_o, g_mlp, w_up, w_down, g_ple, w_gate, p_all, w_proj, layer, g_final, tm=512):
    t = h.shape[0]
    row = lambda a: a.reshape(1, -1).astype(F32)
    args = [h, *mix, w_o, row(g_mlp), w_up, w_down, row(g_ple), w_gate, p_all, w_proj]
    in_specs = [pl.BlockSpec((tm, D_MODEL), lambda i: (i, 0))]
    in_specs += [pl.BlockSpec((tm, m.shape[1]), lambda i: (i, 0)) for m in mix]
    in_specs += [_const_spec(w_o.shape), _const_spec((1, D_MODEL)),
                 _const_spec(w_up.shape), _const_spec(w_down.shape),
                 _const_spec((1, D_MODEL)), _const_spec(w_gate.shape),
                 pl.BlockSpec((None, tm, PLE_DIM), lambda i: (layer, i, 0)),
                 _layer_spec(w_proj.shape[1:], layer)]
    if g_final is not None:
        args.append(row(g_final))
        in_specs.append(_const_spec((1, D_MODEL)))
    return pl.pallas_call(
        functools.partial(_tail_kernel, mix_widths=tuple(m.shape[1] for m in mix),
                          final_norm=g_final is not None),
        grid=(t // tm,),
        in_specs=in_specs,
        out_specs=pl.BlockSpec((tm, D_MODEL), lambda i: (i, 0)),
        out_shape=jax.ShapeDtypeStruct((t, D_MODEL), F32),
        compiler_params=_cparams(("arbitrary",)),
        name="tail",
    )(*args)


def _col_scale(scales):
    cs = np.ones((1, MAIN_W), np.float32)
    for (c0, c1), v in scales.items():
        cs[0, c0:c1] = v
    return jnp.asarray(cs)


def kernel(x, p, norm_mix, norm_mlp, norm_ple, norm_final, even_w_in, even_b_f, lam_q1, lam_k1, lam_q2, lam_k2, diff_subln, even_w_o, odd_w_in, odd_w_gk, odd_b_gk, gla_onorm, odd_w_o, w_up, w_down, w_ple_gate, w_ple_proj):
    batch, seq, _ = x.shape
    depth = p.shape[0]
    t = batch * seq
    h = x.reshape(t, D_MODEL)
    p_all = p.reshape(depth, t, PLE_DIM)
    w_proj = w_ple_proj.astype(BF16)
    for i in range(depth):
        j = i // 2
        if i % 2 == 0:
            cs = _col_scale({(0, 512): DIFF_HALF ** -0.5 * LOG2E,
                             (1536, 2048): HEAD_DIM ** -0.5 * LOG2E})
            tail_w = ((w_up, i), (w_down, i), (w_ple_gate, i), (even_w_o, j))
            (proj, qx, kx), tail_w = _inproj_fox(h, norm_mix[i], even_w_in, j, even_b_f[j], cs, seq,
                                                 tail_w)
            p3 = proj.reshape(batch, seq, MAIN_W)
            lam_init = 0.8 - 0.6 * math.exp(-0.3 * i)
            oa = _diff_attention(p3, lam_q1[j], lam_k1[j], lam_q2[j], lam_k2[j],
                                 diff_subln[j], lam_init)
            ob = _fox_attention(p3, qx.reshape(batch, seq, LANE), kx.reshape(batch, seq, LANE))
            mix = [oa.reshape(t, -1), ob.reshape(t, -1)]
        else:
            cs = _col_scale({(0, 512): HEAD_DIM ** -0.5})
            tail_w = ((w_up, i), (w_down, i), (w_ple_gate, i), (odd_w_o, j))
            (proj, bc), tail_w = _inproj_gla(h, norm_mix[i], odd_w_in, j, odd_w_gk[j], odd_b_gk[j], cs,
                                             tail_w)
            o = _gla(proj.reshape(batch, seq, MAIN_W), bc.reshape(batch, seq, -1), gla_onorm[j])
            mix = [o.reshape(t, -1)]
        w_up_b, w_down_b, w_gate_b, w_o_b = tail_w
        h = _tail(h, mix, w_o_b, norm_mlp[i], w_up_b, w_down_b, norm_ple[i], w_gate_b, p_all,
                  w_proj, i, norm_final if i == depth - 1 else None)
    return h.reshape(batch, seq, D_MODEL)
```

```python
import functools
import math

import jax
import jax.numpy as jnp
import numpy as np
from jax import lax
from jax.experimental import pallas as pl
from jax.experimental.pallas import tpu as pltpu

F32 = jnp.float32
BF16 = jnp.bfloat16

D_MODEL = 1024
PLE_DIM = 256
N_HEADS = 4
HEAD_DIM = 128
DIFF_HALF = 64
GLA_DV = 256
GLA_LOWRANK = 16
GLA_GATE_NORM = 16.0
GLA_CHUNK = 64
GLA_SUB = 16
GLA_FAST_CHUNK = 256
GLA_SAFE_DECAY = 60.0
FOX_EXT_STRIDE = 8
D_FF = 4 * D_MODEL
FF_CHUNK = 2048
EPS = 1e-6
LOG2E = 1.4426950408889634
NEG_BIG = -1e30
LANE = 128
MAIN_W = 3072

VMEM_LIMIT = 56 * 1024 * 1024


def _cparams(sem):
    return pltpu.CompilerParams(dimension_semantics=sem, vmem_limit_bytes=VMEM_LIMIT)


def _const_spec(shape):
    nd = len(shape)
    return pl.BlockSpec(shape, lambda *_: (0,) * nd, pipeline_mode=pl.Buffered(1))


def _layer_spec(shape, layer):
    nd = len(shape)
    return pl.BlockSpec((None,) + tuple(shape), lambda *_: (layer,) + (0,) * nd,
                        pipeline_mode=pl.Buffered(1))


def _rms(x, g):
    return x * lax.rsqrt(jnp.mean(x * x, axis=-1, keepdims=True) + EPS) * g


def _rms_factors(x, g):
    scale = lax.rsqrt(jnp.mean(x * x, axis=-1, keepdims=True) + EPS)
    return (x * g).astype(BF16), jnp.broadcast_to(scale, (x.shape[0], LANE))


def _log_sigmoid(x):
    return jnp.minimum(x, 0.0) - jnp.log(1.0 + jnp.exp(-jnp.abs(x)))


def _split3(x):
    hi = x.astype(BF16)
    r1 = x - hi.astype(F32)
    mid = r1.astype(BF16)
    lo = (r1 - mid.astype(F32)).astype(BF16)
    return hi, mid, lo


def _dot(a, b):
    return jnp.dot(a, b, preferred_element_type=F32)


def _cumsum_rows(x, tri):
    w = x.shape[1]
    y = _dot(tri, jnp.concatenate(_split3(x), axis=1))
    return y[:, :w] + y[:, w:2 * w] + y[:, 2 * w:]


def _dot_nt(a, b):
    return lax.dot_general(a, b, (((1,), (1,)), ((), ())), preferred_element_type=F32)


N_CAST = 4


def _project(x_ref, g_ref, w_ref, wg_ref, cs_ref, o_ref, wb_scr, n_chunk, gate_steps, casts):
    @pl.when(pl.program_id(0) == 0)
    def _():
        for c0 in range(0, MAIN_W, n_chunk):
            wb_scr[:, c0:c0 + n_chunk] = w_ref[c0:c0 + n_chunk, :].T.astype(BF16)

    xg, scale = _rms_factors(x_ref[...], g_ref[...])
    value = _dot(xg, wg_ref[...]) * scale
    scale_n = jnp.concatenate([scale] * (n_chunk // LANE), axis=1)
    steps = list(gate_steps)
    for c0 in range(0, MAIN_W, n_chunk):
        y = _dot(xg, wb_scr[:, c0:c0 + n_chunk]) * cs_ref[:, c0:c0 + n_chunk] * scale_n
        o_ref[:, c0:c0 + n_chunk] = y.astype(o_ref.dtype)
        if steps:
            value = steps.pop(0)(value)
    for step in steps:
        value = step(value)
    for src_ref, dst_ref in casts:
        dst_ref[...] = src_ref[...].astype(BF16)


def _inproj_fox_kernel(x_ref, g_ref, w_ref, wg_ref, cs_ref, bf_ref, tri_ref, eq_ref, ek_ref,
                       oq_ref, ok_ref, *rest, n_chunk, steps_per_seq):
    cast_in, (o_ref, qx_ref, kx_ref), cast_out = rest[:N_CAST], rest[N_CAST:N_CAST + 3], rest[N_CAST + 3:2 * N_CAST + 3]
    wb_scr, carry_ref = rest[2 * N_CAST + 3:]
    @pl.when(pl.program_id(0) % steps_per_seq == 0)
    def _():
        carry_ref[...] = jnp.zeros_like(carry_ref)

    def cumulate(gate):
        logf = _log_sigmoid(gate + bf_ref[...]) * LOG2E
        c = carry_ref[...] + _cumsum_rows(logf, tri_ref[...])
        tm = c.shape[0]
        carry_ref[...] = c[tm - 1:tm, :]
        return c

    def extend(c):
        terms = jnp.concatenate(_split3(c), axis=1)
        qx_ref[...] = (_dot(terms, eq_ref[...]) + oq_ref[...]).astype(BF16)
        kx_ref[...] = (ok_ref[...] - _dot(terms, ek_ref[...])).astype(BF16)

    _project(x_ref, g_ref, w_ref, wg_ref, cs_ref, o_ref, wb_scr, n_chunk, [cumulate, extend],
             list(zip(cast_in, cast_out)))


def _inproj_gla_kernel(x_ref, g_ref, w_ref, wg_ref, cs_ref, wgk_ref, bgk_ref, tri_ref,
                       *rest, n_chunk):
    cast_in, (o_ref, bc_ref), cast_out = rest[:N_CAST], rest[N_CAST:N_CAST + 2], rest[N_CAST + 2:2 * N_CAST + 2]
    wb_scr, = rest[2 * N_CAST + 2:]
    C = GLA_FAST_CHUNK

    def decay(gl):
        x = _dot(gl.astype(BF16), wgk_ref[...]) + bgk_ref[...]
        return _log_sigmoid(x) * (1.0 / GLA_GATE_NORM)

    def cumulate(c):
        def step(gk):
            bc_ref[c * C:(c + 1) * C, :] = _cumsum_rows(gk[c * C:(c + 1) * C, :], tri_ref[...])
            return gk
        return step

    n_row_chunks = x_ref.shape[0] // C
    _project(x_ref, g_ref, w_ref, wg_ref, cs_ref, o_ref, wb_scr, n_chunk,
             [decay] + [cumulate(c) for c in range(n_row_chunks)], list(zip(cast_in, cast_out)))


def _inproj_common(x, g, w_in, layer, n_gate, colscale, tm):
    w_gate = jnp.pad(w_in[layer, :, MAIN_W:], ((0, 0), (0, LANE - n_gate))).astype(BF16)
    args = [x, g.reshape(1, -1).astype(F32), jnp.swapaxes(w_in, 1, 2), w_gate, colscale]
    in_specs = [
        pl.BlockSpec((tm, D_MODEL), lambda i: (i, 0)),
        _const_spec((1, D_MODEL)),
        _layer_spec((MAIN_W, D_MODEL), layer),
        _const_spec((D_MODEL, LANE)),
        _const_spec((1, MAIN_W)),
    ]
    return args, in_specs


def _cast_specs(tail_weights, steps):
    args, in_specs, out_specs, out_shapes = [], [], [], []
    for k, (w, layer) in enumerate(tail_weights):
        _, rows, cols = w.shape
        if k == 0:
            blk = (rows, cols // steps)
            in_specs.append(pl.BlockSpec((None,) + blk, lambda i, layer=layer: (layer, 0, i)))
            out_specs.append(pl.BlockSpec(blk, lambda i: (0, i)))
        else:
            blk = (rows // steps, cols)
            in_specs.append(pl.BlockSpec((None,) + blk, lambda i, layer=layer: (layer, i, 0)))
            out_specs.append(pl.BlockSpec(blk, lambda i: (i, 0)))
        args.append(w)
        out_shapes.append(jax.ShapeDtypeStruct((rows, cols), BF16))
    return args, in_specs, out_specs, out_shapes


def _inproj_fox(x, g, w_in, layer, b_f, colscale, seq, tail_weights, tm=512):
    t = x.shape[0]
    args, in_specs = _inproj_common(x, g, w_in, layer, N_HEADS, colscale, tm)
    c_args, c_in, c_out, c_shapes = _cast_specs(tail_weights, t // tm)
    bf = jnp.pad(b_f.astype(F32).reshape(1, -1), ((0, 0), (0, LANE - N_HEADS)))
    r = np.arange(tm)
    tri = jnp.asarray(r[:, None] >= r[None, :], BF16)
    eq = np.zeros((3 * LANE, LANE), np.float32)
    ek = np.zeros((3 * LANE, LANE), np.float32)
    oq = np.zeros((1, LANE), np.float32)
    ok = np.zeros((1, LANE), np.float32)
    for h in range(N_HEADS):
        for j in range(3):
            eq[j * LANE + h, FOX_EXT_STRIDE * h + j] = 1.0
            ek[j * LANE + h, FOX_EXT_STRIDE * h + 3 + j] = 1.0
            oq[0, FOX_EXT_STRIDE * h + 3 + j] = 1.0
            ok[0, FOX_EXT_STRIDE * h + j] = 1.0
    args += [bf, tri, jnp.asarray(eq, BF16), jnp.asarray(ek, BF16), jnp.asarray(oq), jnp.asarray(ok)]
    args += c_args
    in_specs += [_const_spec((1, LANE)), _const_spec((tm, tm)), _const_spec((3 * LANE, LANE)),
                 _const_spec((3 * LANE, LANE)), _const_spec((1, LANE)), _const_spec((1, LANE))]
    in_specs += c_in
    row_spec = lambda w: pl.BlockSpec((tm, w), lambda i: (i, 0))
    outs = pl.pallas_call(
        functools.partial(_inproj_fox_kernel, n_chunk=1024, steps_per_seq=seq // tm),
        grid=(t // tm,),
        in_specs=in_specs,
        out_specs=[row_spec(MAIN_W), row_spec(LANE), row_spec(LANE)] + c_out,
        out_shape=[jax.ShapeDtypeStruct((t, MAIN_W), BF16), jax.ShapeDtypeStruct((t, LANE), BF16),
                   jax.ShapeDtypeStruct((t, LANE), BF16)] + c_shapes,
        scratch_shapes=[pltpu.VMEM((D_MODEL, MAIN_W), BF16), pltpu.VMEM((1, LANE), F32)],
        compiler_params=_cparams(("arbitrary",)),
        name="inproj_fox",
    )(*args)
    return outs[:3], outs[3:]


def _inproj_gla(x, g, w_in, layer, w_gk, b_gk, colscale, tail_weights, tm=512):
    t = x.shape[0]
    args, in_specs = _inproj_common(x, g, w_in, layer, GLA_LOWRANK, colscale, tm)
    c_args, c_in, c_out, c_shapes = _cast_specs(tail_weights, t // tm)
    wgk = jnp.pad(w_gk, ((0, LANE - GLA_LOWRANK), (0, 0))).astype(BF16)
    r = np.arange(GLA_FAST_CHUNK)
    tri = jnp.asarray(r[:, None] >= r[None, :], BF16)
    args += [wgk, b_gk.reshape(1, -1).astype(F32), tri] + c_args
    in_specs += [_const_spec(wgk.shape), _const_spec((1, N_HEADS * HEAD_DIM)),
                 _const_spec((GLA_FAST_CHUNK, GLA_FAST_CHUNK))] + c_in
    row_spec = lambda w: pl.BlockSpec((tm, w), lambda i: (i, 0))
    outs = pl.pallas_call(
        functools.partial(_inproj_gla_kernel, n_chunk=1024),
        grid=(t // tm,),
        in_specs=in_specs,
        out_specs=[row_spec(MAIN_W), row_spec(N_HEADS * HEAD_DIM)] + c_out,
        out_shape=[jax.ShapeDtypeStruct((t, MAIN_W), BF16),
                   jax.ShapeDtypeStruct((t, N_HEADS * HEAD_DIM), F32)] + c_shapes,
        scratch_shapes=[pltpu.VMEM((D_MODEL, MAIN_W), BF16)],
        compiler_params=_cparams(("arbitrary",)),
        name="inproj_gla",
    )(*args)
    return outs[:2], outs[2:]


def _flash_causal(i, t, u, qs, k_block, v_block, scr):
    n = len(qs)
    g = t // u
    per = g * n
    assert per % 2 == 0
    s_slot, p_slot, (acc, m_ref, alpha_ref) = scr[0:2], scr[2:4], scr[4:]

    def lanes(x, width):
        return jnp.concatenate([x] * (width // LANE), axis=1)

    def logits(a, j, slot, r0=0):
        s_slot[slot][r0:, :] = _dot_nt(qs[a][r0:], k_block(j))

    def pv(a, j, slot, r0=0):
        acc[a, r0:, :] = (lanes(alpha_ref[slot, r0:, :], 2 * HEAD_DIM) * acc[a, r0:, :]
                          + _dot(p_slot[slot][r0:, :], v_block(jnp.maximum(j, 0))))

    def softmax(a, slot, mask, r0=0):
        s_ref = s_slot[slot]
        if mask is None:
            load = lambda: s_ref[r0:, :]
        else:
            load = lambda: jnp.where(mask, s_ref[r0:, :], NEG_BIG)
        m_old = m_ref[a, r0:, :]
        m_new = jnp.maximum(m_old, jnp.max(load(), axis=-1, keepdims=True))
        m_ref[a, r0:, :] = m_new
        alpha_ref[slot, r0:, :] = jnp.exp2(m_old - m_new)
        p_slot[slot][r0:, :] = jnp.exp2(load() - lanes(m_new, u)).astype(BF16)

    def group(j0, tri, last):
        row0 = lambda e: 0 if tri is None or e < 0 else (e // n) * u
        for e in range(per):
            slot = e & 1
            if e + 1 < per:
                logits((e + 1) % n, j0 + (e + 1) // n, 1 - slot, row0(e + 1))
            elif not last:
                logits(0, j0 + g, 1 - slot)
            if e > 0:
                pv((e - 1) % n, j0 + (e - 1) // n, 1 - slot, row0(e - 1))
            else:
                pv(n - 1, j0 - 1, 1 - slot)
            softmax(e % n, slot, None if tri is None else tri[:t - row0(e)], row0(e))

    m_ref[...] = jnp.full(m_ref.shape, NEG_BIG, F32)
    alpha_ref[1] = jnp.ones(alpha_ref.shape[1:], F32)
    acc[...] = jnp.zeros(acc.shape, F32)
    p_slot[1][...] = jnp.zeros(p_slot[1].shape, BF16)
    logits(0, 0, 0)

    def body(k, _):
        group(k * g, None, False)
        return 0

    lax.fori_loop(0, i, body, 0)
    r = lax.broadcasted_iota(jnp.int32, (t, u), 0)
    c = lax.broadcasted_iota(jnp.int32, (t, u), 1)
    group(i * g, c <= r, True)
    pv(n - 1, i * g + g - 1, (per - 1) & 1, (g - 1) * u)
    return [acc[a] for a in range(n)]


def _flash_scratch(n, t, u):
    return [pltpu.VMEM((t, u), F32), pltpu.VMEM((t, u), F32),
            pltpu.VMEM((t, u), BF16), pltpu.VMEM((t, u), BF16),
            pltpu.VMEM((n, t, 2 * HEAD_DIM), F32),
            pltpu.VMEM((n, t, LANE), F32), pltpu.VMEM((2, t, LANE), F32)]


def _ones_block(u):
    return jnp.ones((u, LANE), BF16)


def _fox_attn_kernel(q_ref, qx_ref, k_ref, kx_ref, v_ref, o_ref, *scr, t, u):
    h, i = pl.program_id(1), pl.program_id(2)
    qx = qx_ref[0]
    lane = lax.broadcasted_iota(jnp.int32, qx.shape, 1)
    qx = jnp.where(lane // FOX_EXT_STRIDE == h, qx, jnp.zeros_like(qx))
    q = jnp.concatenate([q_ref[0], qx], axis=1)
    ones = _ones_block(u)

    def k_block(j):
        r = pl.multiple_of(j * u, u)
        return jnp.concatenate([k_ref[0, pl.ds(r, u), :], kx_ref[0, pl.ds(r, u), :]], axis=1)

    def v_block(j):
        return jnp.concatenate([v_ref[0, pl.ds(pl.multiple_of(j * u, u), u), :], ones], axis=1)

    acc, = _flash_causal(i, t, u, [q], k_block, v_block, scr)
    o_ref[0] = (acc[:, :HEAD_DIM] / acc[:, HEAD_DIM:]).astype(o_ref.dtype)


def _fox_attention(p3, qx, kx, t=2048, u=256):
    batch, seq, _ = p3.shape
    return pl.pallas_call(
        functools.partial(_fox_attn_kernel, t=t, u=u),
        grid=(batch, N_HEADS, seq // t),
        in_specs=[
            pl.BlockSpec((1, t, LANE), lambda b, h, i: (b, i, 12 + h)),
            pl.BlockSpec((1, t, LANE), lambda b, h, i: (b, i, 0)),
            pl.BlockSpec((1, seq, LANE), lambda b, h, i: (b, 0, 16 + h)),
            pl.BlockSpec((1, seq, LANE), lambda b, h, i: (b, 0, 0)),
            pl.BlockSpec((1, seq, LANE), lambda b, h, i: (b, 0, 20 + h)),
        ],
        out_specs=pl.BlockSpec((1, t, LANE), lambda b, h, i: (b, i, h)),
        out_shape=jax.ShapeDtypeStruct((batch, seq, N_HEADS * HEAD_DIM), BF16),
        scratch_shapes=_flash_scratch(1, t, u),
        compiler_params=_cparams(("arbitrary", "arbitrary", "arbitrary")),
        name="fox_attn",
    )(p3, qx, p3, kx, p3)


def _diff_attn_kernel(q_ref, k_ref, v_ref, lq1_ref, lk1_ref, lq2_ref, lk2_ref, gs_ref, o_ref,
                      *scr, t, u, lam_init):
    i = pl.program_id(2)
    q = q_ref[0]
    lane = lax.broadcasted_iota(jnp.int32, q.shape, 1)
    zero = jnp.zeros_like(q)
    q1 = jnp.where(lane < DIFF_HALF, q, zero)
    q2 = jnp.where(lane >= DIFF_HALF, q, zero)
    ones = _ones_block(u)
    k_block = lambda j: k_ref[0, pl.ds(pl.multiple_of(j * u, u), u), :]
    v_block = lambda j: jnp.concatenate(
        [v_ref[0, pl.ds(pl.multiple_of(j * u, u), u), :], ones], axis=1)
    a1, a2 = _flash_causal(i, t, u, [q1, q2], k_block, v_block, scr)
    lam = (jnp.exp(jnp.sum(lq1_ref[...] * lk1_ref[...], axis=-1, keepdims=True))
           - jnp.exp(jnp.sum(lq2_ref[...] * lk2_ref[...], axis=-1, keepdims=True)) + lam_init)
    o = a1[:, :HEAD_DIM] / a1[:, HEAD_DIM:] - lam * (a2[:, :HEAD_DIM] / a2[:, HEAD_DIM:])
    o_ref[0] = (_rms(o, gs_ref[...]) * (1.0 - lam_init)).astype(o_ref.dtype)


def _diff_attention(p3, lq1, lk1, lq2, lk2, g_sub, lam_init, t=2048, u=256):
    batch, seq, _ = p3.shape
    vec = lambda a: a.reshape(1, -1).astype(F32)
    return pl.pallas_call(
        functools.partial(_diff_attn_kernel, t=t, u=u, lam_init=lam_init),
        grid=(batch, N_HEADS, seq // t),
        in_specs=[
            pl.BlockSpec((1, t, LANE), lambda b, h, i: (b, i, h)),
            pl.BlockSpec((1, seq, LANE), lambda b, h, i: (b, 0, 4 + h)),
            pl.BlockSpec((1, seq, LANE), lambda b, h, i: (b, 0, 8 + h)),
            _const_spec((1, DIFF_HALF)), _const_spec((1, DIFF_HALF)),
            _const_spec((1, DIFF_HALF)), _const_spec((1, DIFF_HALF)),
            _const_spec((1, HEAD_DIM)),
        ],
        out_specs=pl.BlockSpec((1, t, LANE), lambda b, h, i: (b, i, h)),
        out_shape=jax.ShapeDtypeStruct((batch, seq, N_HEADS * HEAD_DIM), BF16),
        scratch_shapes=_flash_scratch(2, t, u),
        compiler_params=_cparams(("arbitrary", "arbitrary", "arbitrary")),
        name="diff_attn",
    )(p3, p3, p3, vec(lq1), vec(lk1), vec(lq2), vec(lk2), vec(g_sub))


def _gla_kernel(q_ref, k_ref, v_ref, g_ref, bc_ref, bc_next_ref, on_ref, o_ref, st_ref, low_ref,
                *, tc):
    i = pl.program_id(2)

    @pl.when(i == 0)
    def _():
        st_ref[...] = jnp.zeros_like(st_ref)
        low_ref[0] = jnp.min(bc_ref[0])

    C = GLA_FAST_CHUNK

    def finish(o, rows):
        y = _rms(o, on_ref[...])
        g = g_ref[0, rows, :].astype(F32)
        o_ref[0, rows, :] = (y * (g * jax.nn.sigmoid(g))).astype(o_ref.dtype)

    def update_state(st, k, v, b, n):
        b_last = b[n - 1:n, :]
        k_dec = (k * jnp.exp(b_last - b)).astype(BF16)
        vt_k = lax.dot_general(v, k_dec, (((0,), (0,)), ((), ())), preferred_element_type=F32)
        st_ref[...] = st * jnp.exp(b_last) + vt_k

    safe = low_ref[0] >= -GLA_SAFE_DECAY

    def publish_next_low():
        low_ref[0] = jnp.min(bc_next_ref[0])

    @pl.when(safe)
    def _():
        publish_next_low()
        r = lax.broadcasted_iota(jnp.int32, (C, C), 0)
        cc = lax.broadcasted_iota(jnp.int32, (C, C), 1)
        causal = cc <= r
        for c in range(tc // C):
            rows = slice(c * C, (c + 1) * C)
            q = q_ref[0, rows, :].astype(F32)
            k = k_ref[0, rows, :].astype(F32)
            v = v_ref[0, rows, :]
            b = bc_ref[0, rows, :]
            st = st_ref[...]
            qe = (q * jnp.exp(b)).astype(BF16)
            ke = (k * jnp.exp(-b)).astype(BF16)
            o = _dot_nt(qe, st.astype(BF16))
            a = jnp.where(causal, _dot_nt(qe, ke), 0.0)
            o = o + _dot(a.astype(BF16), v)
            update_state(st, k, v, b, C)
            finish(o, rows)

    @pl.when(jnp.logical_not(safe))
    def _():
        publish_next_low()
        L, SB = GLA_CHUNK, GLA_SUB
        row = lax.broadcasted_iota(jnp.int32, (L, HEAD_DIM), 0)
        sub_row = lax.broadcasted_iota(jnp.int32, (SB, 1), 0)

        def chunk(c, _):
            r0 = pl.multiple_of(c * L, L)
            rows = pl.ds(r0, L)
            q = q_ref[0, rows, :].astype(F32)
            k = k_ref[0, rows, :].astype(F32)
            v = v_ref[0, rows, :]
            vf = v.astype(F32)
            before = bc_ref[0, pl.ds(jnp.maximum(r0 - 1, 0), 1), :]
            b = bc_ref[0, rows, :] - jnp.where(r0 % C == 0, 0.0, before)
            st = st_ref[...]

            o = _dot_nt((q * jnp.exp(b)).astype(BF16), st.astype(BF16))

            qs, ks = [], []
            for s in range(1, L // SB):
                ref = b[s * SB - 1:s * SB, :]
                in_sub = (row >= s * SB) & (row < (s + 1) * SB)
                qs.append(jnp.where(in_sub, q * jnp.exp(jnp.minimum(b - ref, 0.0)), 0.0))
                ks.append(jnp.where(row < s * SB, k * jnp.exp(jnp.minimum(ref - b, 0.0)), 0.0))
            a = _dot_nt(jnp.concatenate(qs, axis=1).astype(BF16),
                        jnp.concatenate(ks, axis=1).astype(BF16))
            o = o + _dot(a.astype(BF16), v)

            diag = []
            for s in range(L // SB):
                sl = slice(s * SB, (s + 1) * SB)
                qd, kd, bd, vd = q[sl], k[sl], b[sl], vf[sl]
                acc = jnp.zeros((SB, GLA_DV), F32)
                for j in range(SB):
                    w = qd * kd[j:j + 1] * jnp.exp(jnp.minimum(bd - bd[j:j + 1], 0.0))
                    d = jnp.sum(w, axis=-1, keepdims=True)
                    acc = acc + jnp.where(sub_row >= j, d, 0.0) * vd[j:j + 1]
                diag.append(acc)
            o = o + jnp.concatenate(diag, axis=0)

            update_state(st, k, v, b, L)
            finish(o, rows)
            return 0

        lax.fori_loop(0, tc // L, chunk, 0)


def _gla(p3, bc3, g_onorm, tc=2048):
    batch, seq, _ = p3.shape
    last = seq // tc - 1
    return pl.pallas_call(
        functools.partial(_gla_kernel, tc=tc),
        grid=(batch, N_HEADS, seq // tc),
        in_specs=[
            pl.BlockSpec((1, tc, HEAD_DIM), lambda b, h, i: (b, i, h)),
            pl.BlockSpec((1, tc, HEAD_DIM), lambda b, h, i: (b, i, 4 + h)),
            pl.BlockSpec((1, tc, GLA_DV), lambda b, h, i: (b, i, 4 + h)),
            pl.BlockSpec((1, tc, GLA_DV), lambda b, h, i: (b, i, 8 + h)),
            pl.BlockSpec((1, tc, HEAD_DIM), lambda b, h, i: (b, i, h)),
            pl.BlockSpec((1, tc, HEAD_DIM), lambda b, h, i: (b, jnp.minimum(i + 1, last), h)),
            _const_spec((1, GLA_DV)),
        ],
        out_specs=pl.BlockSpec((1, tc, GLA_DV), lambda b, h, i: (b, i, h)),
        out_shape=jax.ShapeDtypeStruct((batch, seq, N_HEADS * GLA_DV), BF16),
        scratch_shapes=[pltpu.VMEM((GLA_DV, HEAD_DIM), F32), pltpu.SMEM((1,), F32)],
        compiler_params=_cparams(("arbitrary", "arbitrary", "arbitrary")),
        name="gla",
    )(p3, p3, p3, p3, bc3, bc3, g_onorm.reshape(1, -1).astype(F32))


def _tail_kernel(*refs, mix_widths, final_norm):
    n_mix = len(mix_widths)
    h_ref = refs[0]
    o_refs = refs[1:1 + n_mix]
    wo_ref, gm_ref, wu_ref, wd_ref, gp_ref, wg_ref, p_ref, wp_ref = refs[1 + n_mix:9 + n_mix]
    gf_ref = refs[9 + n_mix] if final_norm else None
    out_ref = refs[-1]

    mixed, c0 = None, 0
    for o_ref, w in zip(o_refs, mix_widths):
        term = _dot(o_ref[...], wo_ref[c0:c0 + w, :])
        mixed = term if mixed is None else mixed + term
        c0 += w
    h = h_ref[...] + mixed
    wide = lambda s: jnp.concatenate([s] * (D_MODEL // LANE), axis=1)
    hg, r = _rms_factors(h, gm_ref[...])
    mlp = None
    for c0 in range(0, D_FF, FF_CHUNK):
        u = jnp.square(jnp.maximum(_dot(hg, wu_ref[:, c0:c0 + FF_CHUNK]), 0.0))
        term = _dot(u.astype(BF16), wd_ref[c0:c0 + FF_CHUNK, :])
        mlp = term if mlp is None else mlp + term
    h = h + wide(r * r) * mlp
    hg, r = _rms_factors(h, gp_ref[...])
    gate = jax.nn.sigmoid(wide(r) * _dot(hg, wg_ref[...]))
    h = h + gate * _dot(p_ref[...].astype(BF16), wp_ref[...])
    if final_norm:
        h = _rms(h, gf_ref[...])
    out_ref[...] = h


def _tail(h, mix, w_o, g_mlp, w_up, w_down, g_ple, w_gate, p_all, w_proj, layer, g_final, tm=512):
    t = h.shape[0]
    row = lambda a: a.reshape(1, -1).astype(F32)
    args = [h, *mix, w_o, row(g_mlp), w_up, w_down, row(g_ple), w_gate, p_all, w_proj]
    in_specs = [pl.BlockSpec((tm, D_MODEL), lambda i: (i, 0))]
    in_specs += [pl.BlockSpec((tm, m.shape[1]), lambda i: (i, 0)) for m in mix]
    in_specs += [_const_spec(w_o.shape), _const_spec((1, D_MODEL)),
                 _const_spec(w_up.shape), _const_spec(w_down.shape),
                 _const_spec((1, D_MODEL)), _const_spec(w_gate.shape),
                 pl.BlockSpec((None, tm, PLE_DIM), lambda i: (layer, i, 0)),
                 _layer_spec(w_proj.shape[1:], layer)]
    if g_final is not None:
        args.append(row(g_final))
        in_specs.append(_const_spec((1, D_MODEL)))
    return pl.pallas_call(
        functools.partial(_tail_kernel, mix_widths=tuple(m.shape[1] for m in mix),
                          final_norm=g_final is not None),
        grid=(t // tm,),
        in_specs=in_specs,
        out_specs=pl.BlockSpec((tm, D_MODEL), lambda i: (i, 0)),
        out_shape=jax.ShapeDtypeStruct((t, D_MODEL), F32),
        compiler_params=_cparams(("arbitrary",)),
        name="tail",
    )(*args)


def _col_scale(scales):
    cs = np.ones((1, MAIN_W), np.float32)
    for (c0, c1), v in scales.items():
        cs[0, c0:c1] = v
    return jnp.asarray(cs)


def kernel(x, p, norm_mix, norm_mlp, norm_ple, norm_final, even_w_in, even_b_f, lam_q1, lam_k1, lam_q2, lam_k2, diff_subln, even_w_o, odd_w_in, odd_w_gk, odd_b_gk, gla_onorm, odd_w_o, w_up, w_down, w_ple_gate, w_ple_proj):
    batch, seq, _ = x.shape
    depth = p.shape[0]
    t = batch * seq
    h = x.reshape(t, D_MODEL)
    p_all = p.reshape(depth, t, PLE_DIM)
    w_proj = w_ple_proj.astype(BF16)
    for i in range(depth):
        j = i // 2
        if i % 2 == 0:
            cs = _col_scale({(0, 512): DIFF_HALF ** -0.5 * LOG2E,
                             (1536, 2048): HEAD_DIM ** -0.5 * LOG2E})
            tail_w = ((w_up, i), (w_down, i), (w_ple_gate, i), (even_w_o, j))
            (proj, qx, kx), tail_w = _inproj_fox(h, norm_mix[i], even_w_in, j, even_b_f[j], cs, seq,
                                                 tail_w)
            p3 = proj.reshape(batch, seq, MAIN_W)
            lam_init = 0.8 - 0.6 * math.exp(-0.3 * i)
            oa = _diff_attention(p3, lam_q1[j], lam_k1[j], lam_q2[j], lam_k2[j],
                                 diff_subln[j], lam_init)
            ob = _fox_attention(p3, qx.reshape(batch, seq, LANE), kx.reshape(batch, seq, LANE))
            mix = [oa.reshape(t, -1), ob.reshape(t, -1)]
        else:
            cs = _col_scale({(0, 512): HEAD_DIM ** -0.5})
            tail_w = ((w_up, i), (w_down, i), (w_ple_gate, i), (odd_w_o, j))
            (proj, bc), tail_w = _inproj_gla(h, norm_mix[i], odd_w_in, j, odd_w_gk[j], odd_b_gk[j], cs,
                                             tail_w)
            o = _gla(proj.reshape(batch, seq, MAIN_W), bc.reshape(batch, seq, -1), gla_onorm[j])
            mix = [o.reshape(t, -1)]
        w_up_b, w_down_b, w_gate_b, w_o_b = tail_w
        h = _tail(h, mix, w_o_b, norm_mlp[i], w_up_b, w_down_b, norm_ple[i], w_gate_b, p_all,
                  w_proj, i, norm_final if i == depth - 1 else None)
    return h.reshape(batch, seq, D_MODEL)
```

```python
import functools
import math

import jax
import jax.numpy as jnp
import numpy as np
from jax import lax
from jax.experimental import pallas as pl
from jax.experimental.pallas import tpu as pltpu

F32 = jnp.float32
BF16 = jnp.bfloat16

D_MODEL = 1024
PLE_DIM = 256
N_HEADS = 4
HEAD_DIM = 128
DIFF_HALF = 64
GLA_DV = 256
GLA_LOWRANK = 16
GLA_GATE_NORM = 16.0
GLA_CHUNK = 64
GLA_SUB = 16
GLA_FAST_CHUNK = 256
GLA_SAFE_DECAY = 60.0
FOX_EXT_STRIDE = 8
D_FF = 4 * D_MODEL
FF_CHUNK = 2048
EPS = 1e-6
LOG2E = 1.4426950408889634
NEG_BIG = -1e30
LANE = 128
MAIN_W = 3072

VMEM_LIMIT = 56 * 1024 * 1024


def _cparams(sem):
    return pltpu.CompilerParams(dimension_semantics=sem, vmem_limit_bytes=VMEM_LIMIT)


def _const_spec(shape):
    nd = len(shape)
    return pl.BlockSpec(shape, lambda *_: (0,) * nd, pipeline_mode=pl.Buffered(1))


def _layer_spec(shape, layer):
    nd = len(shape)
    return pl.BlockSpec((None,) + tuple(shape), lambda *_: (layer,) + (0,) * nd,
                        pipeline_mode=pl.Buffered(1))


def _rms(x, g):
    return x * lax.rsqrt(jnp.mean(x * x, axis=-1, keepdims=True) + EPS) * g


def _rms_factors(x, g):
    scale = lax.rsqrt(jnp.mean(x * x, axis=-1, keepdims=True) + EPS)
    return (x * g).astype(BF16), jnp.broadcast_to(scale, (x.shape[0], LANE))


def _log_sigmoid(x):
    return jnp.minimum(x, 0.0) - jnp.log(1.0 + jnp.exp(-jnp.abs(x)))


def _split3(x):
    hi = x.astype(BF16)
    r1 = x - hi.astype(F32)
    mid = r1.astype(BF16)
    lo = (r1 - mid.astype(F32)).astype(BF16)
    return hi, mid, lo


def _dot(a, b):
    return jnp.dot(a, b, preferred_element_type=F32)


def _cumsum_rows(x, tri):
    w = x.shape[1]
    y = _dot(tri, jnp.concatenate(_split3(x), axis=1))
    return y[:, :w] + y[:, w:2 * w] + y[:, 2 * w:]


def _dot_nt(a, b):
    return lax.dot_general(a, b, (((1,), (1,)), ((), ())), preferred_element_type=F32)


N_CAST = 4


def _project(x_ref, g_ref, w_ref, wg_ref, cs_ref, o_ref, wb_scr, n_chunk, gate_steps, casts):
    @pl.when(pl.program_id(0) == 0)
    def _():
        for c0 in range(0, MAIN_W, n_chunk):
            wb_scr[:, c0:c0 + n_chunk] = w_ref[c0:c0 + n_chunk, :].T.astype(BF16)

    xg, scale = _rms_factors(x_ref[...], g_ref[...])
    value = _dot(xg, wg_ref[...]) * scale
    scale_n = jnp.concatenate([scale] * (n_chunk // LANE), axis=1)
    steps = list(gate_steps)
    for c0 in range(0, MAIN_W, n_chunk):
        y = _dot(xg, wb_scr[:, c0:c0 + n_chunk]) * cs_ref[:, c0:c0 + n_chunk] * scale_n
        o_ref[:, c0:c0 + n_chunk] = y.astype(o_ref.dtype)
        if steps:
            value = steps.pop(0)(value)
    for step in steps:
        value = step(value)
    for src_ref, dst_ref in casts:
        dst_ref[...] = src_ref[...].astype(BF16)


def _inproj_fox_kernel(x_ref, g_ref, w_ref, wg_ref, cs_ref, bf_ref, tri_ref, eq_ref, ek_ref,
                       oq_ref, ok_ref, *rest, n_chunk, steps_per_seq):
    cast_in, (o_ref, qx_ref, kx_ref), cast_out = rest[:N_CAST], rest[N_CAST:N_CAST + 3], rest[N_CAST + 3:2 * N_CAST + 3]
    wb_scr, carry_ref = rest[2 * N_CAST + 3:]
    @pl.when(pl.program_id(0) % steps_per_seq == 0)
    def _():
        carry_ref[...] = jnp.zeros_like(carry_ref)

    def cumulate(gate):
        logf = _log_sigmoid(gate + bf_ref[...]) * LOG2E
        c = carry_ref[...] + _cumsum_rows(logf, tri_ref[...])
        tm = c.shape[0]
        carry_ref[...] = c[tm - 1:tm, :]
        return c

    def extend(c):
        terms = jnp.concatenate(_split3(c), axis=1)
        qx_ref[...] = (_dot(terms, eq_ref[...]) + oq_ref[...]).astype(BF16)
        kx_ref[...] = (ok_ref[...] - _dot(terms, ek_ref[...])).astype(BF16)

    _project(x_ref, g_ref, w_ref, wg_ref, cs_ref, o_ref, wb_scr, n_chunk, [cumulate, extend],
             list(zip(cast_in, cast_out)))


def _inproj_gla_kernel(x_ref, g_ref, w_ref, wg_ref, cs_ref, wgk_ref, bgk_ref, tri_ref,
                       *rest, n_chunk):
    cast_in, (o_ref, bc_ref), cast_out = rest[:N_CAST], rest[N_CAST:N_CAST + 2], rest[N_CAST + 2:2 * N_CAST + 2]
    wb_scr, = rest[2 * N_CAST + 2:]
    C = GLA_FAST_CHUNK

    def decay(gl):
        x = _dot(gl.astype(BF16), wgk_ref[...]) + bgk_ref[...]
        return _log_sigmoid(x) * (1.0 / GLA_GATE_NORM)

    def cumulate(c):
        def step(gk):
            bc_ref[c * C:(c + 1) * C, :] = _cumsum_rows(gk[c * C:(c + 1) * C, :], tri_ref[...])
            return gk
        return step

    n_row_chunks = x_ref.shape[0] // C
    _project(x_ref, g_ref, w_ref, wg_ref, cs_ref, o_ref, wb_scr, n_chunk,
             [decay] + [cumulate(c) for c in range(n_row_chunks)], list(zip(cast_in, cast_out)))


def _inproj_common(x, g, w_in, layer, n_gate, colscale, tm):
    w_gate = jnp.pad(w_in[layer, :, MAIN_W:], ((0, 0), (0, LANE - n_gate))).astype(BF16)
    args = [x, g.reshape(1, -1).astype(F32), jnp.swapaxes(w_in, 1, 2), w_gate, colscale]
    in_specs = [
        pl.BlockSpec((tm, D_MODEL), lambda i: (i, 0)),
        _const_spec((1, D_MODEL)),
        _layer_spec((MAIN_W, D_MODEL), layer),
        _const_spec((D_MODEL, LANE)),
        _const_spec((1, MAIN_W)),
    ]
    return args, in_specs


def _cast_specs(tail_weights, steps):
    args, in_specs, out_specs, out_shapes = [], [], [], []
    for k, (w, layer) in enumerate(tail_weights):
        _, rows, cols = w.shape
        if k == 0:
            blk = (rows, cols // steps)
            in_specs.append(pl.BlockSpec((None,) + blk, lambda i, layer=layer: (layer, 0, i)))
            out_specs.append(pl.BlockSpec(blk, lambda i: (0, i)))
        else:
            blk = (rows // steps, cols)
            in_specs.append(pl.BlockSpec((None,) + blk, lambda i, layer=layer: (layer, i, 0)))
            out_specs.append(pl.BlockSpec(blk, lambda i: (i, 0)))
        args.append(w)
        out_shapes.append(jax.ShapeDtypeStruct((rows, cols), BF16))
    return args, in_specs, out_specs, out_shapes


def _inproj_fox(x, g, w_in, layer, b_f, colscale, seq, tail_weights, tm=512):
    t = x.shape[0]
    args, in_specs = _inproj_common(x, g, w_in, layer, N_HEADS, colscale, tm)
    c_args, c_in, c_out, c_shapes = _cast_specs(tail_weights, t // tm)
    bf = jnp.pad(b_f.astype(F32).reshape(1, -1), ((0, 0), (0, LANE - N_HEADS)))
    r = np.arange(tm)
    tri = jnp.asarray(r[:, None] >= r[None, :], BF16)
    eq = np.zeros((3 * LANE, LANE), np.float32)
    ek = np.zeros((3 * LANE, LANE), np.float32)
    oq = np.zeros((1, LANE), np.float32)
    ok = np.zeros((1, LANE), np.float32)
    for h in range(N_HEADS):
        for j in range(3):
            eq[j * LANE + h, FOX_EXT_STRIDE * h + j] = 1.0
            ek[j * LANE + h, FOX_EXT_STRIDE * h + 3 + j] = 1.0
            oq[0, FOX_EXT_STRIDE * h + 3 + j] = 1.0
            ok[0, FOX_EXT_STRIDE * h + j] = 1.0
    args += [bf, tri, jnp.asarray(eq, BF16), jnp.asarray(ek, BF16), jnp.asarray(oq), jnp.asarray(ok)]
    args += c_args
    in_specs += [_const_spec((1, LANE)), _const_spec((tm, tm)), _const_spec((3 * LANE, LANE)),
                 _const_spec((3 * LANE, LANE)), _const_spec((1, LANE)), _const_spec((1, LANE))]
    in_specs += c_in
    row_spec = lambda w: pl.BlockSpec((tm, w), lambda i: (i, 0))
    outs = pl.pallas_call(
        functools.partial(_inproj_fox_kernel, n_chunk=1024, steps_per_seq=seq // tm),
        grid=(t // tm,),
        in_specs=in_specs,
        out_specs=[row_spec(MAIN_W), row_spec(LANE), row_spec(LANE)] + c_out,
        out_shape=[jax.ShapeDtypeStruct((t, MAIN_W), BF16), jax.ShapeDtypeStruct((t, LANE), BF16),
                   jax.ShapeDtypeStruct((t, LANE), BF16)] + c_shapes,
        scratch_shapes=[pltpu.VMEM((D_MODEL, MAIN_W), BF16), pltpu.VMEM((1, LANE), F32)],
        compiler_params=_cparams(("arbitrary",)),
        name="inproj_fox",
    )(*args)
    return outs[:3], outs[3:]


def _inproj_gla(x, g, w_in, layer, w_gk, b_gk, colscale, tail_weights, tm=512):
    t = x.shape[0]
    args, in_specs = _inproj_common(x, g, w_in, layer, GLA_LOWRANK, colscale, tm)
    c_args, c_in, c_out, c_shapes = _cast_specs(tail_weights, t // tm)
    wgk = jnp.pad(w_gk, ((0, LANE - GLA_LOWRANK), (0, 0))).astype(BF16)
    r = np.arange(GLA_FAST_CHUNK)
    tri = jnp.asarray(r[:, None] >= r[None, :], BF16)
    args += [wgk, b_gk.reshape(1, -1).astype(F32), tri] + c_args
    in_specs += [_const_spec(wgk.shape), _const_spec((1, N_HEADS * HEAD_DIM)),
                 _const_spec((GLA_FAST_CHUNK, GLA_FAST_CHUNK))] + c_in
    row_spec = lambda w: pl.BlockSpec((tm, w), lambda i: (i, 0))
    outs = pl.pallas_call(
        functools.partial(_inproj_gla_kernel, n_chunk=1024),
        grid=(t // tm,),
        in_specs=in_specs,
        out_specs=[row_spec(MAIN_W), row_spec(N_HEADS * HEAD_DIM)] + c_out,
        out_shape=[jax.ShapeDtypeStruct((t, MAIN_W), BF16),
                   jax.ShapeDtypeStruct((t, N_HEADS * HEAD_DIM), F32)] + c_shapes,
        scratch_shapes=[pltpu.VMEM((D_MODEL, MAIN_W), BF16)],
        compiler_params=_cparams(("arbitrary",)),
        name="inproj_gla",
    )(*args)
    return outs[:2], outs[2:]


def _flash_causal(i, t, u, qs, k_block, v_block, scr):
    n = len(qs)
    g = t // u
    per = g * n
    assert per % 2 == 0
    s_slot, p_slot, (acc, m_ref, alpha_ref) = scr[0:2], scr[2:4], scr[4:]

    def lanes(x, width):
        return jnp.concatenate([x] * (width // LANE), axis=1)

    def logits(a, j, slot, r0=0):
        s_slot[slot][r0:, :] = _dot_nt(qs[a][r0:], k_block(a, j))

    def pv(a, j, slot, r0=0):
        acc[a, r0:, :] = (lanes(alpha_ref[slot, r0:, :], 2 * HEAD_DIM) * acc[a, r0:, :]
                          + _dot(p_slot[slot][r0:, :], v_block(a, jnp.maximum(j, 0))))

    def softmax(a, slot, mask, r0=0):
        s_ref = s_slot[slot]
        if mask is None:
            load = lambda: s_ref[r0:, :]
        else:
            load = lambda: jnp.where(mask, s_ref[r0:, :], NEG_BIG)
        m_old = m_ref[a, r0:, :]
        m_new = jnp.maximum(m_old, jnp.max(load(), axis=-1, keepdims=True))
        m_ref[a, r0:, :] = m_new
        alpha_ref[slot, r0:, :] = jnp.exp2(m_old - m_new)
        p_slot[slot][r0:, :] = jnp.exp2(load() - lanes(m_new, u)).astype(BF16)

    def group(j0, tri, last):
        row0 = lambda e: 0 if tri is None or e < 0 else (e // n) * u
        for e in range(per):
            slot = e & 1
            if e + 1 < per:
                logits((e + 1) % n, j0 + (e + 1) // n, 1 - slot, row0(e + 1))
            elif not last:
                logits(0, j0 + g, 1 - slot)
            if e > 0:
                pv((e - 1) % n, j0 + (e - 1) // n, 1 - slot, row0(e - 1))
            else:
                pv(n - 1, j0 - 1, 1 - slot)
            softmax(e % n, slot, None if tri is None else tri[:t - row0(e)], row0(e))

    m_ref[...] = jnp.full(m_ref.shape, NEG_BIG, F32)
    alpha_ref[1] = jnp.ones(alpha_ref.shape[1:], F32)
    acc[...] = jnp.zeros(acc.shape, F32)
    p_slot[1][...] = jnp.zeros(p_slot[1].shape, BF16)
    logits(0, 0, 0)

    def body(k, _):
        group(k * g, None, False)
        return 0

    lax.fori_loop(0, i, body, 0)
    r = lax.broadcasted_iota(jnp.int32, (t, u), 0)
    c = lax.broadcasted_iota(jnp.int32, (t, u), 1)
    group(i * g, c <= r, True)
    pv(n - 1, i * g + g - 1, (per - 1) & 1, (g - 1) * u)
    return [acc[a] for a in range(n)]


def _flash_scratch(n, t, u):
    return [pltpu.VMEM((t, u), F32), pltpu.VMEM((t, u), F32),
            pltpu.VMEM((t, u), BF16), pltpu.VMEM((t, u), BF16),
            pltpu.VMEM((n, t, 2 * HEAD_DIM), F32),
            pltpu.VMEM((n, t, LANE), F32), pltpu.VMEM((2, t, LANE), F32)]


def _ones_block(u):
    return jnp.ones((u, LANE), BF16)


def _fox_attn_kernel(q_ref, qx_ref, k_ref, kx_ref, v_ref, o_ref, *scr, t, u):
    hp, i = pl.program_id(1), pl.program_id(2)
    qx_all = qx_ref[0]
    lane = lax.broadcasted_iota(jnp.int32, qx_all.shape, 1)
    ones = _ones_block(u)
    head = lambda a: slice(a * HEAD_DIM, (a + 1) * HEAD_DIM)
    qs = []
    for a in range(2):
        qx = jnp.where(lane // FOX_EXT_STRIDE == 2 * hp + a, qx_all, jnp.zeros_like(qx_all))
        qs.append(jnp.concatenate([q_ref[0, :, head(a)], qx], axis=1))

    def k_block(a, j):
        r = pl.ds(pl.multiple_of(j * u, u), u)
        return jnp.concatenate([k_ref[0, r, head(a)], kx_ref[0, r, :]], axis=1)

    def v_block(a, j):
        return jnp.concatenate([v_ref[0, pl.ds(pl.multiple_of(j * u, u), u), head(a)], ones], axis=1)

    accs = _flash_causal(i, t, u, qs, k_block, v_block, scr)
    for a, acc in enumerate(accs):
        o_ref[0, :, head(a)] = (acc[:, :HEAD_DIM] / acc[:, HEAD_DIM:]).astype(o_ref.dtype)


def _fox_attention(p3, qx, kx, t=2048, u=256):
    batch, seq, _ = p3.shape
    return pl.pallas_call(
        functools.partial(_fox_attn_kernel, t=t, u=u),
        grid=(batch, N_HEADS // 2, seq // t),
        in_specs=[
            pl.BlockSpec((1, t, 2 * LANE), lambda b, h, i: (b, i, 6 + h)),
            pl.BlockSpec((1, t, LANE), lambda b, h, i: (b, i, 0)),
            pl.BlockSpec((1, seq, 2 * LANE), lambda b, h, i: (b, 0, 8 + h)),
            pl.BlockSpec((1, seq, LANE), lambda b, h, i: (b, 0, 0)),
            pl.BlockSpec((1, seq, 2 * LANE), lambda b, h, i: (b, 0, 10 + h)),
        ],
        out_specs=pl.BlockSpec((1, t, 2 * LANE), lambda b, h, i: (b, i, h)),
        out_shape=jax.ShapeDtypeStruct((batch, seq, N_HEADS * HEAD_DIM), BF16),
        scratch_shapes=_flash_scratch(2, t, u),
        compiler_params=_cparams(("arbitrary", "arbitrary", "arbitrary")),
        name="fox_attn",
    )(p3, qx, p3, kx, p3)


def _diff_attn_kernel(q_ref, k_ref, v_ref, lq1_ref, lk1_ref, lq2_ref, lk2_ref, gs_ref, o_ref,
                      *scr, t, u, lam_init):
    i = pl.program_id(2)
    q = q_ref[0]
    lane = lax.broadcasted_iota(jnp.int32, q.shape, 1)
    zero = jnp.zeros_like(q)
    q1 = jnp.where(lane < DIFF_HALF, q, zero)
    q2 = jnp.where(lane >= DIFF_HALF, q, zero)
    ones = _ones_block(u)
    k_block = lambda a, j: k_ref[0, pl.ds(pl.multiple_of(j * u, u), u), :]
    v_block = lambda a, j: jnp.concatenate(
        [v_ref[0, pl.ds(pl.multiple_of(j * u, u), u), :], ones], axis=1)
    a1, a2 = _flash_causal(i, t, u, [q1, q2], k_block, v_block, scr)
    lam = (jnp.exp(jnp.sum(lq1_ref[...] * lk1_ref[...], axis=-1, keepdims=True))
           - jnp.exp(jnp.sum(lq2_ref[...] * lk2_ref[...], axis=-1, keepdims=True)) + lam_init)
    o = a1[:, :HEAD_DIM] / a1[:, HEAD_DIM:] - lam * (a2[:, :HEAD_DIM] / a2[:, HEAD_DIM:])
    o_ref[0] = (_rms(o, gs_ref[...]) * (1.0 - lam_init)).astype(o_ref.dtype)


def _diff_attention(p3, lq1, lk1, lq2, lk2, g_sub, lam_init, t=2048, u=256):
    batch, seq, _ = p3.shape
    vec = lambda a: a.reshape(1, -1).astype(F32)
    return pl.pallas_call(
        functools.partial(_diff_attn_kernel, t=t, u=u, lam_init=lam_init),
        grid=(batch, N_HEADS, seq // t),
        in_specs=[
            pl.BlockSpec((1, t, LANE), lambda b, h, i: (b, i, h)),
            pl.BlockSpec((1, seq, LANE), lambda b, h, i: (b, 0, 4 + h)),
            pl.BlockSpec((1, seq, LANE), lambda b, h, i: (b, 0, 8 + h)),
            _const_spec((1, DIFF_HALF)), _const_spec((1, DIFF_HALF)),
            _const_spec((1, DIFF_HALF)), _const_spec((1, DIFF_HALF)),
            _const_spec((1, HEAD_DIM)),
        ],
        out_specs=pl.BlockSpec((1, t, LANE), lambda b, h, i: (b, i, h)),
        out_shape=jax.ShapeDtypeStruct((batch, seq, N_HEADS * HEAD_DIM), BF16),
        scratch_shapes=_flash_scratch(2, t, u),
        compiler_params=_cparams(("arbitrary", "arbitrary", "arbitrary")),
        name="diff_attn",
    )(p3, p3, p3, vec(lq1), vec(lk1), vec(lq2), vec(lk2), vec(g_sub))


def _gla_kernel(q_ref, k_ref, v_ref, g_ref, bc_ref, bc_next_ref, on_ref, o_ref, st_ref, low_ref,
                *, tc):
    i = pl.program_id(2)

    @pl.when(i == 0)
    def _():
        st_ref[...] = jnp.zeros_like(st_ref)
        low_ref[0] = jnp.min(bc_ref[0])

    C = GLA_FAST_CHUNK

    def finish(o, rows):
        y = _rms(o, on_ref[...])
        g = g_ref[0, rows, :].astype(F32)
        o_ref[0, rows, :] = (y * (g * jax.nn.sigmoid(g))).astype(o_ref.dtype)

    def update_state(st, k, v, b, n):
        b_last = b[n - 1:n, :]
        k_dec = (k * jnp.exp(b_last - b)).astype(BF16)
        vt_k = lax.dot_general(v, k_dec, (((0,), (0,)), ((), ())), preferred_element_type=F32)
        st_ref[...] = st * jnp.exp(b_last) + vt_k

    safe = low_ref[0] >= -GLA_SAFE_DECAY

    def publish_next_low():
        low_ref[0] = jnp.min(bc_next_ref[0])

    @pl.when(safe)
    def _():
        publish_next_low()
        r = lax.broadcasted_iota(jnp.int32, (C, C), 0)
        cc = lax.broadcasted_iota(jnp.int32, (C, C), 1)
        causal = cc <= r
        for c in range(tc // C):
            rows = slice(c * C, (c + 1) * C)
            q = q_ref[0, rows, :].astype(F32)
            k = k_ref[0, rows, :].astype(F32)
            v = v_ref[0, rows, :]
            b = bc_ref[0, rows, :]
            st = st_ref[...]
            qe = (q * jnp.exp(b)).astype(BF16)
            ke = (k * jnp.exp(-b)).astype(BF16)
            o = _dot_nt(qe, st.astype(BF16))
            a = jnp.where(causal, _dot_nt(qe, ke), 0.0)
            o = o + _dot(a.astype(BF16), v)
            update_state(st, k, v, b, C)
            finish(o, rows)

    @pl.when(jnp.logical_not(safe))
    def _():
        publish_next_low()
        L, SB = GLA_CHUNK, GLA_SUB
        row = lax.broadcasted_iota(jnp.int32, (L, HEAD_DIM), 0)
        sub_row = lax.broadcasted_iota(jnp.int32, (SB, 1), 0)

        def chunk(c, _):
            r0 = pl.multiple_of(c * L, L)
            rows = pl.ds(r0, L)
            q = q_ref[0, rows, :].astype(F32)
            k = k_ref[0, rows, :].astype(F32)
            v = v_ref[0, rows, :]
            vf = v.astype(F32)
            before = bc_ref[0, pl.ds(jnp.maximum(r0 - 1, 0), 1), :]
            b = bc_ref[0, rows, :] - jnp.where(r0 % C == 0, 0.0, before)
            st = st_ref[...]

            o = _dot_nt((q * jnp.exp(b)).astype(BF16), st.astype(BF16))

            qs, ks = [], []
            for s in range(1, L // SB):
                ref = b[s * SB - 1:s * SB, :]
                in_sub = (row >= s * SB) & (row < (s + 1) * SB)
                qs.append(jnp.where(in_sub, q * jnp.exp(jnp.minimum(b - ref, 0.0)), 0.0))
                ks.append(jnp.where(row < s * SB, k * jnp.exp(jnp.minimum(ref - b, 0.0)), 0.0))
            a = _dot_nt(jnp.concatenate(qs, axis=1).astype(BF16),
                        jnp.concatenate(ks, axis=1).astype(BF16))
            o = o + _dot(a.astype(BF16), v)

            diag = []
            for s in range(L // SB):
                sl = slice(s * SB, (s + 1) * SB)
                qd, kd, bd, vd = q[sl], k[sl], b[sl], vf[sl]
                acc = jnp.zeros((SB, GLA_DV), F32)
                for j in range(SB):
                    w = qd * kd[j:j + 1] * jnp.exp(jnp.minimum(bd - bd[j:j + 1], 0.0))
                    d = jnp.sum(w, axis=-1, keepdims=True)
                    acc = acc + jnp.where(sub_row >= j, d, 0.0) * vd[j:j + 1]
                diag.append(acc)
            o = o + jnp.concatenate(diag, axis=0)

            update_state(st, k, v, b, L)
            finish(o, rows)
            return 0

        lax.fori_loop(0, tc // L, chunk, 0)


def _gla(p3, bc3, g_onorm, tc=2048):
    batch, seq, _ = p3.shape
    last = seq // tc - 1
    return pl.pallas_call(
        functools.partial(_gla_kernel, tc=tc),
        grid=(batch, N_HEADS, seq // tc),
        in_specs=[
            pl.BlockSpec((1, tc, HEAD_DIM), lambda b, h, i: (b, i, h)),
            pl.BlockSpec((1, tc, HEAD_DIM), lambda b, h, i: (b, i, 4 + h)),
            pl.BlockSpec((1, tc, GLA_DV), lambda b, h, i: (b, i, 4 + h)),
            pl.BlockSpec((1, tc, GLA_DV), lambda b, h, i: (b, i, 8 + h)),
            pl.BlockSpec((1, tc, HEAD_DIM), lambda b, h, i: (b, i, h)),
            pl.BlockSpec((1, tc, HEAD_DIM), lambda b, h, i: (b, jnp.minimum(i + 1, last), h)),
            _const_spec((1, GLA_DV)),
        ],
        out_specs=pl.BlockSpec((1, tc, GLA_DV), lambda b, h, i: (b, i, h)),
        out_shape=jax.ShapeDtypeStruct((batch, seq, N_HEADS * GLA_DV), BF16),
        scratch_shapes=[pltpu.VMEM((GLA_DV, HEAD_DIM), F32), pltpu.SMEM((1,), F32)],
        compiler_params=_cparams(("arbitrary", "arbitrary", "arbitrary")),
        name="gla",
    )(p3, p3, p3, p3, bc3, bc3, g_onorm.reshape(1, -1).astype(F32))


def _tail_kernel(*refs, mix_widths, final_norm):
    n_mix = len(mix_widths)
    h_ref = refs[0]
    o_refs = refs[1:1 + n_mix]
    wo_ref, gm_ref, wu_ref, wd_ref, gp_ref, wg_ref, p_ref, wp_ref = refs[1 + n_mix:9 + n_mix]
    gf_ref = refs[9 + n_mix] if final_norm else None
    out_ref = refs[-1]

    mixed, c0 = None, 0
    for o_ref, w in zip(o_refs, mix_widths):
        term = _dot(o_ref[...], wo_ref[c0:c0 + w, :])
        mixed = term if mixed is None else mixed + term
        c0 += w
    h = h_ref[...] + mixed
    wide = lambda s: jnp.concatenate([s] * (D_MODEL // LANE), axis=1)
    hg, r = _rms_factors(h, gm_ref[...])
    mlp = None
    for c0 in range(0, D_FF, FF_CHUNK):
        u = jnp.square(jnp.maximum(_dot(hg, wu_ref[:, c0:c0 + FF_CHUNK]), 0.0))
        term = _dot(u.astype(BF16), wd_ref[c0:c0 + FF_CHUNK, :])
        mlp = term if mlp is None else mlp + term
    h = h + wide(r * r) * mlp
    hg, r = _rms_factors(h, gp_ref[...])
    gate = jax.nn.sigmoid(wide(r) * _dot(hg, wg_ref[...]))
    h = h + gate * _dot(p_ref[...].astype(BF16), wp_ref[...])
    if final_norm:
        h = _rms(h, gf_ref[...])
    out_ref[...] = h


def _tail(h, mix, w_o, g_mlp, w_up, w_down, g_ple, w_gate, p_all, w_proj, layer, g_final, tm=512):
    t = h.shape[0]
    row = lambda a: a.reshape(1, -1).astype(F32)
    args = [h, *mix, w_o, row(g_mlp), w_up, w_down, row(g_ple), w_gate, p_all, w_proj]
    in_specs = [pl.BlockSpec((tm, D_MODEL), lambda i: (i, 0))]
    in_specs += [pl.BlockSpec((tm, m.shape[1]), lambda i: (i, 0)) for m in mix]
    in_specs += [_const_spec(w_o.shape), _const_spec((1, D_MODEL)),
                 _const_spec(w_up.shape), _const_spec(w_down.shape),
                 _const_spec((1, D_MODEL)), _const_spec(w_gate.shape),
                 pl.BlockSpec((None, tm, PLE_DIM), lambda i: (layer, i, 0)),
                 _layer_spec(w_proj.shape[1:], layer)]
    if g_final is not None:
        args.append(row(g_final))
        in_specs.append(_const_spec((1, D_MODEL)))
    return pl.pallas_call(
        functools.partial(_tail_kernel, mix_widths=tuple(m.shape[1] for m in mix),
                          final_norm=g_final is not None),
        grid=(t // tm,),
        in_specs=in_specs,
        out_specs=pl.BlockSpec((tm, D_MODEL), lambda i: (i, 0)),
        out_shape=jax.ShapeDtypeStruct((t, D_MODEL), F32),
        compiler_params=_cparams(("arbitrary",)),
        name="tail",
    )(*args)


def _col_scale(scales):
    cs = np.ones((1, MAIN_W), np.float32)
    for (c0, c1), v in scales.items():
        cs[0, c0:c1] = v
    return jnp.asarray(cs)


def kernel(x, p, norm_mix, norm_mlp, norm_ple, norm_final, even_w_in, even_b_f, lam_q1, lam_k1, lam_q2, lam_k2, diff_subln, even_w_o, odd_w_in, odd_w_gk, odd_b_gk, gla_onorm, odd_w_o, w_up, w_down, w_ple_gate, w_ple_proj):
    batch, seq, _ = x.shape
    depth = p.shape[0]
    t = batch * seq
    h = x.reshape(t, D_MODEL)
    p_all = p.reshape(depth, t, PLE_DIM)
    w_proj = w_ple_proj.astype(BF16)
    for i in range(depth):
        j = i // 2
        if i % 2 == 0:
            cs = _col_scale({(0, 512): DIFF_HALF ** -0.5 * LOG2E,
                             (1536, 2048): HEAD_DIM ** -0.5 * LOG2E})
            tail_w = ((w_up, i), (w_down, i), (w_ple_gate, i), (even_w_o, j))
            (proj, qx, kx), tail_w = _inproj_fox(h, norm_mix[i], even_w_in, j, even_b_f[j], cs, seq,
                                                 tail_w)
            p3 = proj.reshape(batch, seq, MAIN_W)
            lam_init = 0.8 - 0.6 * math.exp(-0.3 * i)
            oa = _diff_attention(p3, lam_q1[j], lam_k1[j], lam_q2[j], lam_k2[j],
                                 diff_subln[j], lam_init)
            ob = _fox_attention(p3, qx.reshape(batch, seq, LANE), kx.reshape(batch, seq, LANE))
            mix = [oa.reshape(t, -1), ob.reshape(t, -1)]
        else:
            cs = _col_scale({(0, 512): HEAD_DIM ** -0.5})
            tail_w = ((w_up, i), (w_down, i), (w_ple_gate, i), (odd_w_o, j))
            (proj, bc), tail_w = _inproj_gla(h, norm_mix[i], odd_w_in, j, odd_w_gk[j], odd_b_gk[j], cs,
                                             tail_w)
            o = _gla(proj.reshape(batch, seq, MAIN_W), bc.reshape(batch, seq, -1), gla_onorm[j])
            mix = [o.reshape(t, -1)]
        w_up_b, w_down_b, w_gate_b, w_o_b = tail_w
        h = _tail(h, mix, w_o_b, norm_mlp[i], w_up_b, w_down_b, norm_ple[i], w_gate_b, p_all,
                  w_proj, i, norm_final if i == depth - 1 else None)
    return h.reshape(batch, seq, D_MODEL)
```

```python
import functools
import math

import jax
import jax.numpy as jnp
import numpy as np
from jax import lax
from jax.experimental import pallas as pl
from jax.experimental.pallas import tpu as pltpu

F32 = jnp.float32
BF16 = jnp.bfloat16

D_MODEL = 1024
PLE_DIM = 256
N_HEADS = 4
HEAD_DIM = 128
DIFF_HALF = 64
GLA_DV = 256
GLA_LOWRANK = 16
GLA_GATE_NORM = 16.0
GLA_CHUNK = 64
GLA_SUB = 16
GLA_FAST_CHUNK = 256
GLA_SAFE_DECAY = 60.0
FOX_EXT_STRIDE = 8
D_FF = 4 * D_MODEL
FF_CHUNK = 2048
EPS = 1e-6
LOG2E = 1.4426950408889634
NEG_BIG = -1e30
LANE = 128
MAIN_W = 3072

VMEM_LIMIT = 56 * 1024 * 1024


def _cparams(sem):
    return pltpu.CompilerParams(dimension_semantics=sem, vmem_limit_bytes=VMEM_LIMIT)


def _const_spec(shape):
    nd = len(shape)
    return pl.BlockSpec(shape, lambda *_: (0,) * nd, pipeline_mode=pl.Buffered(1))


def _layer_spec(shape, layer):
    nd = len(shape)
    return pl.BlockSpec((None,) + tuple(shape), lambda *_: (layer,) + (0,) * nd,
                        pipeline_mode=pl.Buffered(1))


def _rms(x, g):
    return x * lax.rsqrt(jnp.mean(x * x, axis=-1, keepdims=True) + EPS) * g


def _rms_factors(x, g):
    scale = lax.rsqrt(jnp.mean(x * x, axis=-1, keepdims=True) + EPS)
    return (x * g).astype(BF16), jnp.broadcast_to(scale, (x.shape[0], LANE))


def _log_sigmoid(x):
    return jnp.minimum(x, 0.0) - jnp.log(1.0 + jnp.exp(-jnp.abs(x)))


def _split3(x):
    hi = x.astype(BF16)
    r1 = x - hi.astype(F32)
    mid = r1.astype(BF16)
    lo = (r1 - mid.astype(F32)).astype(BF16)
    return hi, mid, lo


def _dot(a, b):
    return jnp.dot(a, b, preferred_element_type=F32)


def _cumsum_rows(x, tri):
    w = x.shape[1]
    y = _dot(tri, jnp.concatenate(_split3(x), axis=1))
    return y[:, :w] + y[:, w:2 * w] + y[:, 2 * w:]


def _dot_nt(a, b):
    return lax.dot_general(a, b, (((1,), (1,)), ((), ())), preferred_element_type=F32)


N_CAST = 4


def _project(x_ref, g_ref, w_ref, wg_ref, cs_ref, o_ref, wb_scr, n_chunk, gate_steps, casts):
    @pl.when(pl.program_id(0) == 0)
    def _():
        for c0 in range(0, MAIN_W, n_chunk):
            wb_scr[:, c0:c0 + n_chunk] = w_ref[c0:c0 + n_chunk, :].T.astype(BF16)

    xg, scale = _rms_factors(x_ref[...], g_ref[...])
    value = _dot(xg, wg_ref[...]) * scale
    scale_n = jnp.concatenate([scale] * (n_chunk // LANE), axis=1)
    steps = list(gate_steps)
    for c0 in range(0, MAIN_W, n_chunk):
        y = _dot(xg, wb_scr[:, c0:c0 + n_chunk]) * cs_ref[:, c0:c0 + n_chunk] * scale_n
        o_ref[:, c0:c0 + n_chunk] = y.astype(o_ref.dtype)
        if steps:
            value = steps.pop(0)(value)
    for step in steps:
        value = step(value)
    for src_ref, dst_ref in casts:
        dst_ref[...] = src_ref[...].astype(BF16)


def _inproj_fox_kernel(x_ref, g_ref, w_ref, wg_ref, cs_ref, bf_ref, tri_ref, eq_ref, ek_ref,
                       oq_ref, ok_ref, *rest, n_chunk, steps_per_seq):
    cast_in, (o_ref, qx_ref, kx_ref), cast_out = rest[:N_CAST], rest[N_CAST:N_CAST + 3], rest[N_CAST + 3:2 * N_CAST + 3]
    wb_scr, carry_ref = rest[2 * N_CAST + 3:]
    @pl.when(pl.program_id(0) % steps_per_seq == 0)
    def _():
        carry_ref[...] = jnp.zeros_like(carry_ref)

    def cumulate(gate):
        logf = _log_sigmoid(gate + bf_ref[...]) * LOG2E
        c = carry_ref[...] + _cumsum_rows(logf, tri_ref[...])
        tm = c.shape[0]
        carry_ref[...] = c[tm - 1:tm, :]
        return c

    def extend(c):
        terms = jnp.concatenate(_split3(c), axis=1)
        qx_ref[...] = (_dot(terms, eq_ref[...]) + oq_ref[...]).astype(BF16)
        kx_ref[...] = (ok_ref[...] - _dot(terms, ek_ref[...])).astype(BF16)

    _project(x_ref, g_ref, w_ref, wg_ref, cs_ref, o_ref, wb_scr, n_chunk, [cumulate, extend],
             list(zip(cast_in, cast_out)))


def _inproj_gla_kernel(x_ref, g_ref, w_ref, wg_ref, cs_ref, wgk_ref, bgk_ref, tri_ref,
                       *rest, n_chunk):
    cast_in, (o_ref, bc_ref), cast_out = rest[:N_CAST], rest[N_CAST:N_CAST + 2], rest[N_CAST + 2:2 * N_CAST + 2]
    wb_scr, = rest[2 * N_CAST + 2:]
    C = GLA_FAST_CHUNK

    def decay(gl):
        x = _dot(gl.astype(BF16), wgk_ref[...]) + bgk_ref[...]
        return _log_sigmoid(x) * (1.0 / GLA_GATE_NORM)

    def cumulate(c):
        def step(gk):
            bc_ref[c * C:(c + 1) * C, :] = _cumsum_rows(gk[c * C:(c + 1) * C, :], tri_ref[...])
            return gk
        return step

    n_row_chunks = x_ref.shape[0] // C
    _project(x_ref, g_ref, w_ref, wg_ref, cs_ref, o_ref, wb_scr, n_chunk,
             [decay] + [cumulate(c) for c in range(n_row_chunks)], list(zip(cast_in, cast_out)))


def _inproj_common(x, g, w_in, layer, n_gate, colscale, tm):
    w_gate = jnp.pad(w_in[layer, :, MAIN_W:], ((0, 0), (0, LANE - n_gate))).astype(BF16)
    args = [x, g.reshape(1, -1).astype(F32), jnp.swapaxes(w_in, 1, 2), w_gate, colscale]
    in_specs = [
        pl.BlockSpec((tm, D_MODEL), lambda i: (i, 0)),
        _const_spec((1, D_MODEL)),
        _layer_spec((MAIN_W, D_MODEL), layer),
        _const_spec((D_MODEL, LANE)),
        _const_spec((1, MAIN_W)),
    ]
    return args, in_specs


def _cast_specs(tail_weights, steps):
    args, in_specs, out_specs, out_shapes = [], [], [], []
    for k, (w, layer) in enumerate(tail_weights):
        _, rows, cols = w.shape
        if k == 0:
            blk = (rows, cols // steps)
            in_specs.append(pl.BlockSpec((None,) + blk, lambda i, layer=layer: (layer, 0, i)))
            out_specs.append(pl.BlockSpec(blk, lambda i: (0, i)))
        else:
            blk = (rows // steps, cols)
            in_specs.append(pl.BlockSpec((None,) + blk, lambda i, layer=layer: (layer, i, 0)))
            out_specs.append(pl.BlockSpec(blk, lambda i: (i, 0)))
        args.append(w)
        out_shapes.append(jax.ShapeDtypeStruct((rows, cols), BF16))
    return args, in_specs, out_specs, out_shapes


def _inproj_fox(x, g, w_in, layer, b_f, colscale, seq, tail_weights, tm=512):
    t = x.shape[0]
    args, in_specs = _inproj_common(x, g, w_in, layer, N_HEADS, colscale, tm)
    c_args, c_in, c_out, c_shapes = _cast_specs(tail_weights, t // tm)
    bf = jnp.pad(b_f.astype(F32).reshape(1, -1), ((0, 0), (0, LANE - N_HEADS)))
    r = np.arange(tm)
    tri = jnp.asarray(r[:, None] >= r[None, :], BF16)
    eq = np.zeros((3 * LANE, LANE), np.float32)
    ek = np.zeros((3 * LANE, LANE), np.float32)
    oq = np.zeros((1, LANE), np.float32)
    ok = np.zeros((1, LANE), np.float32)
    for h in range(N_HEADS):
        for j in range(3):
            eq[j * LANE + h, FOX_EXT_STRIDE * h + j] = 1.0
            ek[j * LANE + h, FOX_EXT_STRIDE * h + 3 + j] = 1.0
            oq[0, FOX_EXT_STRIDE * h + 3 + j] = 1.0
            ok[0, FOX_EXT_STRIDE * h + j] = 1.0
    args += [bf, tri, jnp.asarray(eq, BF16), jnp.asarray(ek, BF16), jnp.asarray(oq), jnp.asarray(ok)]
    args += c_args
    in_specs += [_const_spec((1, LANE)), _const_spec((tm, tm)), _const_spec((3 * LANE, LANE)),
                 _const_spec((3 * LANE, LANE)), _const_spec((1, LANE)), _const_spec((1, LANE))]
    in_specs += c_in
    row_spec = lambda w: pl.BlockSpec((tm, w), lambda i: (i, 0))
    outs = pl.pallas_call(
        functools.partial(_inproj_fox_kernel, n_chunk=1024, steps_per_seq=seq // tm),
        grid=(t // tm,),
        in_specs=in_specs,
        out_specs=[row_spec(MAIN_W), row_spec(LANE), row_spec(LANE)] + c_out,
        out_shape=[jax.ShapeDtypeStruct((t, MAIN_W), BF16), jax.ShapeDtypeStruct((t, LANE), BF16),
                   jax.ShapeDtypeStruct((t, LANE), BF16)] + c_shapes,
        scratch_shapes=[pltpu.VMEM((D_MODEL, MAIN_W), BF16), pltpu.VMEM((1, LANE), F32)],
        compiler_params=_cparams(("arbitrary",)),
        name="inproj_fox",
    )(*args)
    return outs[:3], outs[3:]


def _inproj_gla(x, g, w_in, layer, w_gk, b_gk, colscale, tail_weights, tm=512):
    t = x.shape[0]
    args, in_specs = _inproj_common(x, g, w_in, layer, GLA_LOWRANK, colscale, tm)
    c_args, c_in, c_out, c_shapes = _cast_specs(tail_weights, t // tm)
    wgk = jnp.pad(w_gk, ((0, LANE - GLA_LOWRANK), (0, 0))).astype(BF16)
    r = np.arange(GLA_FAST_CHUNK)
    tri = jnp.asarray(r[:, None] >= r[None, :], BF16)
    args += [wgk, b_gk.reshape(1, -1).astype(F32), tri] + c_args
    in_specs += [_const_spec(wgk.shape), _const_spec((1, N_HEADS * HEAD_DIM)),
                 _const_spec((GLA_FAST_CHUNK, GLA_FAST_CHUNK))] + c_in
    row_spec = lambda w: pl.BlockSpec((tm, w), lambda i: (i, 0))
    outs = pl.pallas_call(
        functools.partial(_inproj_gla_kernel, n_chunk=1024),
        grid=(t // tm,),
        in_specs=in_specs,
        out_specs=[row_spec(MAIN_W), row_spec(N_HEADS * HEAD_DIM)] + c_out,
        out_shape=[jax.ShapeDtypeStruct((t, MAIN_W), BF16),
                   jax.ShapeDtypeStruct((t, N_HEADS * HEAD_DIM), F32)] + c_shapes,
        scratch_shapes=[pltpu.VMEM((D_MODEL, MAIN_W), BF16)],
        compiler_params=_cparams(("arbitrary",)),
        name="inproj_gla",
    )(*args)
    return outs[:2], outs[2:]


def _flash_causal(i, t, u, qs, k_block, v_block, scr):
    n = len(qs)
    g = t // u
    per = g * n
    assert per % 2 == 0
    s_slot, p_slot, (acc, m_ref, alpha_ref) = scr[0:2], scr[2:4], scr[4:]

    def lanes(x, width):
        return jnp.concatenate([x] * (width // LANE), axis=1)

    def logits(a, j, slot, r0=0):
        s_slot[slot][r0:, :] = _dot_nt(qs[a][r0:], k_block(a, j))

    def pv(a, j, slot, r0=0):
        acc[a, r0:, :] = (lanes(alpha_ref[slot, r0:, :], 2 * HEAD_DIM) * acc[a, r0:, :]
                          + _dot(p_slot[slot][r0:, :], v_block(a, jnp.maximum(j, 0))))

    def softmax(a, slot, mask, r0=0):
        s_ref = s_slot[slot]
        if mask is None:
            load = lambda: s_ref[r0:, :]
        else:
            load = lambda: jnp.where(mask, s_ref[r0:, :], NEG_BIG)
        m_old = m_ref[a, r0:, :]
        m_new = jnp.maximum(m_old, jnp.max(load(), axis=-1, keepdims=True))
        m_ref[a, r0:, :] = m_new
        alpha_ref[slot, r0:, :] = jnp.exp2(m_old - m_new)
        p_slot[slot][r0:, :] = jnp.exp2(load() - lanes(m_new, u)).astype(BF16)

    def group(j0, tri, last):
        row0 = lambda e: 0 if tri is None or e < 0 else (e // n) * u
        for e in range(per):
            slot = e & 1
            if e + 1 < per:
                logits((e + 1) % n, j0 + (e + 1) // n, 1 - slot, row0(e + 1))
            elif not last:
                logits(0, j0 + g, 1 - slot)
            if e > 0:
                pv((e - 1) % n, j0 + (e - 1) // n, 1 - slot, row0(e - 1))
            else:
                pv(n - 1, j0 - 1, 1 - slot)
            softmax(e % n, slot, None if tri is None else tri[:t - row0(e)], row0(e))

    m_ref[...] = jnp.full(m_ref.shape, NEG_BIG, F32)
    alpha_ref[1] = jnp.ones(alpha_ref.shape[1:], F32)
    acc[...] = jnp.zeros(acc.shape, F32)
    p_slot[1][...] = jnp.zeros(p_slot[1].shape, BF16)
    logits(0, 0, 0)

    def body(k, _):
        group(k * g, None, False)
        return 0

    lax.fori_loop(0, i, body, 0)
    r = lax.broadcasted_iota(jnp.int32, (t, u), 0)
    c = lax.broadcasted_iota(jnp.int32, (t, u), 1)
    group(i * g, c <= r, True)
    pv(n - 1, i * g + g - 1, (per - 1) & 1, (g - 1) * u)
    return [acc[a] for a in range(n)]


def _flash_scratch(n, t, u):
    return [pltpu.VMEM((t, u), F32), pltpu.VMEM((t, u), F32),
            pltpu.VMEM((t, u), BF16), pltpu.VMEM((t, u), BF16),
            pltpu.VMEM((n, t, 2 * HEAD_DIM), F32),
            pltpu.VMEM((n, t, LANE), F32), pltpu.VMEM((2, t, LANE), F32)]


def _ones_block(u):
    return jnp.ones((u, LANE), BF16)


def _fox_attn_kernel(q_ref, qx_ref, k_ref, kx_ref, v_ref, o_ref, *scr, t, u):
    hp, i = pl.program_id(1), pl.program_id(2)
    qx_all = qx_ref[0]
    lane = lax.broadcasted_iota(jnp.int32, qx_all.shape, 1)
    ones = _ones_block(u)
    head = lambda a: slice(a * HEAD_DIM, (a + 1) * HEAD_DIM)
    qs = []
    for a in range(2):
        qx = jnp.where(lane // FOX_EXT_STRIDE == 2 * hp + a, qx_all, jnp.zeros_like(qx_all))
        qs.append(jnp.concatenate([q_ref[0, :, head(a)], qx], axis=1))

    def k_block(a, j):
        r = pl.ds(pl.multiple_of(j * u, u), u)
        return jnp.concatenate([k_ref[0, r, head(a)], kx_ref[0, r, :]], axis=1)

    def v_block(a, j):
        return jnp.concatenate([v_ref[0, pl.ds(pl.multiple_of(j * u, u), u), head(a)], ones], axis=1)

    accs = _flash_causal(i, t, u, qs, k_block, v_block, scr)
    for a, acc in enumerate(accs):
        o_ref[0, :, head(a)] = (acc[:, :HEAD_DIM] / acc[:, HEAD_DIM:]).astype(o_ref.dtype)


def _fox_attention(p3, qx, kx, t=2048, u=256):
    batch, seq, _ = p3.shape
    return pl.pallas_call(
        functools.partial(_fox_attn_kernel, t=t, u=u),
        grid=(batch, N_HEADS // 2, seq // t),
        in_specs=[
            pl.BlockSpec((1, t, 2 * LANE), lambda b, h, i: (b, i, 6 + h)),
            pl.BlockSpec((1, t, LANE), lambda b, h, i: (b, i, 0)),
            pl.BlockSpec((1, seq, 2 * LANE), lambda b, h, i: (b, 0, 8 + h)),
            pl.BlockSpec((1, seq, LANE), lambda b, h, i: (b, 0, 0)),
            pl.BlockSpec((1, seq, 2 * LANE), lambda b, h, i: (b, 0, 10 + h)),
        ],
        out_specs=pl.BlockSpec((1, t, 2 * LANE), lambda b, h, i: (b, i, h)),
        out_shape=jax.ShapeDtypeStruct((batch, seq, N_HEADS * HEAD_DIM), BF16),
        scratch_shapes=_flash_scratch(2, t, u),
        compiler_params=_cparams(("arbitrary", "arbitrary", "arbitrary")),
        name="fox_attn",
    )(p3, qx, p3, kx, p3)


def _diff_attn_kernel(q_ref, k_ref, v_ref, lq1_ref, lk1_ref, lq2_ref, lk2_ref, gs_ref, o_ref,
                      *scr, t, u, lam_init):
    i = pl.program_id(2)
    head = lambda hd: slice(hd * HEAD_DIM, (hd + 1) * HEAD_DIM)
    lane = lax.broadcasted_iota(jnp.int32, (t, HEAD_DIM), 1)
    qs = []
    for hd in range(2):
        q = q_ref[0, :, head(hd)]
        zero = jnp.zeros_like(q)
        qs += [jnp.where(lane < DIFF_HALF, q, zero), jnp.where(lane >= DIFF_HALF, q, zero)]
    ones = _ones_block(u)
    k_block = lambda a, j: k_ref[0, pl.ds(pl.multiple_of(j * u, u), u), head(a // 2)]
    v_block = lambda a, j: jnp.concatenate(
        [v_ref[0, pl.ds(pl.multiple_of(j * u, u), u), head(a // 2)], ones], axis=1)
    accs = _flash_causal(i, t, u, qs, k_block, v_block, scr)
    lam = (jnp.exp(jnp.sum(lq1_ref[...] * lk1_ref[...], axis=-1, keepdims=True))
           - jnp.exp(jnp.sum(lq2_ref[...] * lk2_ref[...], axis=-1, keepdims=True)) + lam_init)
    for hd in range(2):
        a1, a2 = accs[2 * hd], accs[2 * hd + 1]
        o = a1[:, :HEAD_DIM] / a1[:, HEAD_DIM:] - lam * (a2[:, :HEAD_DIM] / a2[:, HEAD_DIM:])
        o_ref[0, :, head(hd)] = (_rms(o, gs_ref[...]) * (1.0 - lam_init)).astype(o_ref.dtype)


def _diff_attention(p3, lq1, lk1, lq2, lk2, g_sub, lam_init, t=2048, u=256):
    batch, seq, _ = p3.shape
    vec = lambda a: a.reshape(1, -1).astype(F32)
    return pl.pallas_call(
        functools.partial(_diff_attn_kernel, t=t, u=u, lam_init=lam_init),
        grid=(batch, N_HEADS // 2, seq // t),
        in_specs=[
            pl.BlockSpec((1, t, 2 * LANE), lambda b, h, i: (b, i, h)),
            pl.BlockSpec((1, seq, 2 * LANE), lambda b, h, i: (b, 0, 2 + h)),
            pl.BlockSpec((1, seq, 2 * LANE), lambda b, h, i: (b, 0, 4 + h)),
            _const_spec((1, DIFF_HALF)), _const_spec((1, DIFF_HALF)),
            _const_spec((1, DIFF_HALF)), _const_spec((1, DIFF_HALF)),
            _const_spec((1, HEAD_DIM)),
        ],
        out_specs=pl.BlockSpec((1, t, 2 * LANE), lambda b, h, i: (b, i, h)),
        out_shape=jax.ShapeDtypeStruct((batch, seq, N_HEADS * HEAD_DIM), BF16),
        scratch_shapes=_flash_scratch(4, t, u),
        compiler_params=_cparams(("arbitrary", "arbitrary", "arbitrary")),
        name="diff_attn",
    )(p3, p3, p3, vec(lq1), vec(lk1), vec(lq2), vec(lk2), vec(g_sub))


def _gla_kernel(q_ref, k_ref, v_ref, g_ref, bc_ref, bc_next_ref, on_ref, o_ref, st_ref, low_ref,
                *, tc):
    i = pl.program_id(2)

    @pl.when(i == 0)
    def _():
        st_ref[...] = jnp.zeros_like(st_ref)
        low_ref[0] = jnp.min(bc_ref[0])

    C = GLA_FAST_CHUNK

    def finish(o, rows):
        y = _rms(o, on_ref[...])
        g = g_ref[0, rows, :].astype(F32)
        o_ref[0, rows, :] = (y * (g * jax.nn.sigmoid(g))).astype(o_ref.dtype)

    def update_state(st, k, v, b, n):
        b_last = b[n - 1:n, :]
        k_dec = (k * jnp.exp(b_last - b)).astype(BF16)
        vt_k = lax.dot_general(v, k_dec, (((0,), (0,)), ((), ())), preferred_element_type=F32)
        st_ref[...] = st * jnp.exp(b_last) + vt_k

    safe = low_ref[0] >= -GLA_SAFE_DECAY

    def publish_next_low():
        low_ref[0] = jnp.min(bc_next_ref[0])

    @pl.when(safe)
    def _():
        publish_next_low()
        r = lax.broadcasted_iota(jnp.int32, (C, C), 0)
        cc = lax.broadcasted_iota(jnp.int32, (C, C), 1)
        causal = cc <= r
        for c in range(tc // C):
            rows = slice(c * C, (c + 1) * C)
            q = q_ref[0, rows, :].astype(F32)
            k = k_ref[0, rows, :].astype(F32)
            v = v_ref[0, rows, :]
            b = bc_ref[0, rows, :]
            st = st_ref[...]
            qe = (q * jnp.exp(b)).astype(BF16)
            ke = (k * jnp.exp(-b)).astype(BF16)
            o = _dot_nt(qe, st.astype(BF16))
            a = jnp.where(causal, _dot_nt(qe, ke), 0.0)
            o = o + _dot(a.astype(BF16), v)
            update_state(st, k, v, b, C)
            finish(o, rows)

    @pl.when(jnp.logical_not(safe))
    def _():
        publish_next_low()
        L, SB = GLA_CHUNK, GLA_SUB
        row = lax.broadcasted_iota(jnp.int32, (L, HEAD_DIM), 0)
        sub_row = lax.broadcasted_iota(jnp.int32, (SB, 1), 0)

        def chunk(c, _):
            r0 = pl.multiple_of(c * L, L)
            rows = pl.ds(r0, L)
            q = q_ref[0, rows, :].astype(F32)
            k = k_ref[0, rows, :].astype(F32)
            v = v_ref[0, rows, :]
            vf = v.astype(F32)
            before = bc_ref[0, pl.ds(jnp.maximum(r0 - 1, 0), 1), :]
            b = bc_ref[0, rows, :] - jnp.where(r0 % C == 0, 0.0, before)
            st = st_ref[...]

            o = _dot_nt((q * jnp.exp(b)).astype(BF16), st.astype(BF16))

            qs, ks = [], []
            for s in range(1, L // SB):
                ref = b[s * SB - 1:s * SB, :]
                in_sub = (row >= s * SB) & (row < (s + 1) * SB)
                qs.append(jnp.where(in_sub, q * jnp.exp(jnp.minimum(b - ref, 0.0)), 0.0))
                ks.append(jnp.where(row < s * SB, k * jnp.exp(jnp.minimum(ref - b, 0.0)), 0.0))
            a = _dot_nt(jnp.concatenate(qs, axis=1).astype(BF16),
                        jnp.concatenate(ks, axis=1).astype(BF16))
            o = o + _dot(a.astype(BF16), v)

            diag = []
            for s in range(L // SB):
                sl = slice(s * SB, (s + 1) * SB)
                qd, kd, bd, vd = q[sl], k[sl], b[sl], vf[sl]
                acc = jnp.zeros((SB, GLA_DV), F32)
                for j in range(SB):
                    w = qd * kd[j:j + 1] * jnp.exp(jnp.minimum(bd - bd[j:j + 1], 0.0))
                    d = jnp.sum(w, axis=-1, keepdims=True)
                    acc = acc + jnp.where(sub_row >= j, d, 0.0) * vd[j:j + 1]
                diag.append(acc)
            o = o + jnp.concatenate(diag, axis=0)

            update_state(st, k, v, b, L)
            finish(o, rows)
            return 0

        lax.fori_loop(0, tc // L, chunk, 0)


def _gla(p3, bc3, g_onorm, tc=2048):
    batch, seq, _ = p3.shape
    last = seq // tc - 1
    return pl.pallas_call(
        functools.partial(_gla_kernel, tc=tc),
        grid=(batch, N_HEADS, seq // tc),
        in_specs=[
            pl.BlockSpec((1, tc, HEAD_DIM), lambda b, h, i: (b, i, h)),
            pl.BlockSpec((1, tc, HEAD_DIM), lambda b, h, i: (b, i, 4 + h)),
            pl.BlockSpec((1, tc, GLA_DV), lambda b, h, i: (b, i, 4 + h)),
            pl.BlockSpec((1, tc, GLA_DV), lambda b, h, i: (b, i, 8 + h)),
            pl.BlockSpec((1, tc, HEAD_DIM), lambda b, h, i: (b, i, h)),
            pl.BlockSpec((1, tc, HEAD_DIM), lambda b, h, i: (b, jnp.minimum(i + 1, last), h)),
            _const_spec((1, GLA_DV)),
        ],
        out_specs=pl.BlockSpec((1, tc, GLA_DV), lambda b, h, i: (b, i, h)),
        out_shape=jax.ShapeDtypeStruct((batch, seq, N_HEADS * GLA_DV), BF16),
        scratch_shapes=[pltpu.VMEM((GLA_DV, HEAD_DIM), F32), pltpu.SMEM((1,), F32)],
        compiler_params=_cparams(("arbitrary", "arbitrary", "arbitrary")),
        name="gla",
    )(p3, p3, p3, p3, bc3, bc3, g_onorm.reshape(1, -1).astype(F32))


def _tail_kernel(*refs, mix_widths, final_norm):
    n_mix = len(mix_widths)
    h_ref = refs[0]
    o_refs = refs[1:1 + n_mix]
    wo_ref, gm_ref, wu_ref, wd_ref, gp_ref, wg_ref, p_ref, wp_ref = refs[1 + n_mix:9 + n_mix]
    gf_ref = refs[9 + n_mix] if final_norm else None
    out_ref = refs[-1]

    mixed, c0 = None, 0
    for o_ref, w in zip(o_refs, mix_widths):
        term = _dot(o_ref[...], wo_ref[c0:c0 + w, :])
        mixed = term if mixed is None else mixed + term
        c0 += w
    h = h_ref[...] + mixed
    wide = lambda s: jnp.concatenate([s] * (D_MODEL // LANE), axis=1)
    hg, r = _rms_factors(h, gm_ref[...])
    mlp = None
    for c0 in range(0, D_FF, FF_CHUNK):
        u = jnp.square(jnp.maximum(_dot(hg, wu_ref[:, c0:c0 + FF_CHUNK]), 0.0))
        term = _dot(u.astype(BF16), wd_ref[c0:c0 + FF_CHUNK, :])
        mlp = term if mlp is None else mlp + term
    h = h + wide(r * r) * mlp
    hg, r = _rms_factors(h, gp_ref[...])
    gate = jax.nn.sigmoid(wide(r) * _dot(hg, wg_ref[...]))
    h = h + gate * _dot(p_ref[...].astype(BF16), wp_ref[...])
    if final_norm:
        h = _rms(h, gf_ref[...])
    out_ref[...] = h


def _tail(h, mix, w_o, g_mlp, w_up, w_down, g_ple, w_gate, p_all, w_proj, layer, g_final, tm=512):
    t = h.shape[0]
    row = lambda a: a.reshape(1, -1).astype(F32)
    args = [h, *mix, w_o, row(g_mlp), w_up, w_down, row(g_ple), w_gate, p_all, w_proj]
    in_specs = [pl.BlockSpec((tm, D_MODEL), lambda i: (i, 0))]
    in_specs += [pl.BlockSpec((tm, m.shape[1]), lambda i: (i, 0)) for m in mix]
    in_specs += [_const_spec(w_o.shape), _const_spec((1, D_MODEL)),
                 _const_spec(w_up.shape), _const_spec(w_down.shape),
                 _const_spec((1, D_MODEL)), _const_spec(w_gate.shape),
                 pl.BlockSpec((None, tm, PLE_DIM), lambda i: (layer, i, 0)),
                 _layer_spec(w_proj.shape[1:], layer)]
    if g_final is not None:
        args.append(row(g_final))
        in_specs.append(_const_spec((1, D_MODEL)))
    return pl.pallas_call(
        functools.partial(_tail_kernel, mix_widths=tuple(m.shape[1] for m in mix),
                          final_norm=g_final is not None),
        grid=(t // tm,),
        in_specs=in_specs,
        out_specs=pl.BlockSpec((tm, D_MODEL), lambda i: (i, 0)),
        out_shape=jax.ShapeDtypeStruct((t, D_MODEL), F32),
        compiler_params=_cparams(("arbitrary",)),
        name="tail",
    )(*args)


def _col_scale(scales):
    cs = np.ones((1, MAIN_W), np.float32)
    for (c0, c1), v in scales.items():
        cs[0, c0:c1] = v
    return jnp.asarray(cs)


def kernel(x, p, norm_mix, norm_mlp, norm_ple, norm_final, even_w_in, even_b_f, lam_q1, lam_k1, lam_q2, lam_k2, diff_subln, even_w_o, odd_w_in, odd_w_gk, odd_b_gk, gla_onorm, odd_w_o, w_up, w_down, w_ple_gate, w_ple_proj):
    batch, seq, _ = x.shape
    depth = p.shape[0]
    t = batch * seq
    h = x.reshape(t, D_MODEL)
    p_all = p.reshape(depth, t, PLE_DIM)
    w_proj = w_ple_proj.astype(BF16)
    for i in range(depth):
        j = i // 2
        if i % 2 == 0:
            cs = _col_scale({(0, 512): DIFF_HALF ** -0.5 * LOG2E,
                             (1536, 2048): HEAD_DIM ** -0.5 * LOG2E})
            tail_w = ((w_up, i), (w_down, i), (w_ple_gate, i), (even_w_o, j))
            (proj, qx, kx), tail_w = _inproj_fox(h, norm_mix[i], even_w_in, j, even_b_f[j], cs, seq,
                                                 tail_w)
            p3 = proj.reshape(batch, seq, MAIN_W)
            lam_init = 0.8 - 0.6 * math.exp(-0.3 * i)
            oa = _diff_attention(p3, lam_q1[j], lam_k1[j], lam_q2[j], lam_k2[j],
                                 diff_subln[j], lam_init)
            ob = _fox_attention(p3, qx.reshape(batch, seq, LANE), kx.reshape(batch, seq, LANE))
            mix = [oa.reshape(t, -1), ob.reshape(t, -1)]
        else:
            cs = _col_scale({(0, 512): HEAD_DIM ** -0.5})
            tail_w = ((w_up, i), (w_down, i), (w_ple_gate, i), (odd_w_o, j))
            (proj, bc), tail_w = _inproj_gla(h, norm_mix[i], odd_w_in, j, odd_w_gk[j], odd_b_gk[j], cs,
                                             tail_w)
            o = _gla(proj.reshape(batch, seq, MAIN_W), bc.reshape(batch, seq, -1), gla_onorm[j])
            mix = [o.reshape(t, -1)]
        w_up_b, w_down_b, w_gate_b, w_o_b = tail_w
        h = _tail(h, mix, w_o_b, norm_mlp[i], w_up_b, w_down_b, norm_ple[i], w_gate_b, p_all,
                  w_proj, i, norm_final if i == depth - 1 else None)
    return h.reshape(batch, seq, D_MODEL)
```
